```python
import jax, jax.numpy as jnp
from jax import lax
import numpy as np

D_MODEL = 2048
BATCH = 4
SEQ = 2048
DEPTH = 1
DEC_BATCH = 128
DEC_SEQ = 8
PAST_LEN = 16384
PAGE_SIZE = 128

MIX_WIDTH = D_MODEL
GDN_HEADS = 8
GDN_DK = 128
GDN_DV = 128
RET_HEADS = 4
RET_DK = 256
RET_DV = 256
CONV_WIDTH = 4
CHUNK = 64
N_MEM = 256
MEM_HEADS = 4
MEM_HEAD_DIM = D_MODEL // MEM_HEADS
D_FF = -(-8 * D_MODEL // (3 * 256)) * 256
EPS = 1e-6
ROPE_BASE = 10000.0

GDN_QK = GDN_HEADS * GDN_DK
GDN_V = GDN_HEADS * GDN_DV
GDN_CONV_DIM = 2 * GDN_QK + GDN_V
RET_QK = RET_HEADS * RET_DK
RET_V = RET_HEADS * RET_DV
IN_WIDTH = GDN_CONV_DIM + GDN_V + 2 * GDN_HEADS + 2 * RET_QK + 2 * RET_V

kernel_name = "hybrid_gdn_retention_memory_step"


def rms_norm(x, g):
    xf = x.astype(jnp.float32)
    y = xf * lax.rsqrt(jnp.mean(xf * xf, -1, keepdims=True) + EPS)
    return (y * g.astype(jnp.float32)).astype(x.dtype)


def head_rms(x, g):
    return x * lax.rsqrt(jnp.mean(x * x, -1, keepdims=True) + EPS) * g.astype(jnp.float32)


def l2norm(x):
    return x * lax.rsqrt(jnp.sum(x * x, -1, keepdims=True) + EPS)


def short_conv(u, buf, w):
    L = u.shape[1]
    full = jnp.concatenate([buf, u], axis=1)
    out = full[:, 0:L] * w[:, 0]
    for i in range(1, CONV_WIDTH):
        out = out + full[:, i:i + L] * w[:, i]
    return jax.nn.silu(out), full[:, full.shape[1] - (CONV_WIDTH - 1):]


def rotary(x, pos):
    half = x.shape[-1] // 2
    inv = ROPE_BASE ** (-jnp.linspace(0.0, 1.0, half, dtype=jnp.float32))
    ang = pos[:, None] * inv[None, :]
    cos = jnp.cos(ang)[None, :, None, :]
    sin = jnp.sin(ang)[None, :, None, :]
    x2 = x.reshape(x.shape[:-1] + (half, 2))
    xe, xo = x2[..., 0], x2[..., 1]
    return jnp.stack([xe * cos - xo * sin, xo * cos + xe * sin], -1).reshape(x.shape)


def _pad_len(t, pad):
    return jnp.pad(t, [(0, 0), (0, pad)] + [(0, 0)] * (t.ndim - 2))


def _chunk(t, C):
    B, Lp = t.shape[:2]
    t = t.reshape((B, Lp // C, C) + t.shape[2:])
    return jnp.moveaxis(t, 3, 1)


def _unchunk(o, L):
    B, H, NC, C, D = o.shape
    return jnp.moveaxis(o.reshape(B, H, NC * C, D), 1, 2)[:, :L]


def gated_delta_chunked(q, k, v, g, beta, s0):
    L = q.shape[1]
    C = min(CHUNK, L)
    pad = (-L) % C
    q, k, v, g, beta = [_chunk(_pad_len(t, pad), C) for t in (q, k, v, g, beta)]
    G = jnp.cumsum(g, -1)
    causal = jnp.tril(jnp.ones((C, C), bool))
    strict = jnp.tril(jnp.ones((C, C), bool), -1)
    decay = jnp.exp(jnp.where(causal, G[..., :, None] - G[..., None, :], -jnp.inf))
    kk = jnp.einsum('bhncd,bhnmd->bhncm', k, k)
    lower = jnp.where(strict, beta[..., :, None] * kk * decay, 0.0)
    a_mat = lower + jnp.eye(C, dtype=lower.dtype)
    rhs = jnp.concatenate([v * beta[..., None], k * (beta * jnp.exp(G))[..., None]], -1)
    sol = lax.linalg.triangular_solve(a_mat, rhs, left_side=True, lower=True, unit_diagonal=True)
    u_part, w_part = sol[..., :GDN_DV], sol[..., GDN_DV:]
    qk = jnp.einsum('bhncd,bhnmd->bhncm', q, k) * decay
    q_dec = q * jnp.exp(G)[..., None]
    k_dec = k * jnp.exp(G[..., -1:] - G)[..., None]
    g_last = jnp.exp(G[..., -1])

    def step(S, xs):
        u_c, w_c, qk_c, qd_c, kd_c, gl_c = xs
        u = u_c - jnp.einsum('bhck,bhkv->bhcv', w_c, S)
        o = jnp.einsum('bhck,bhkv->bhcv', qd_c, S) + jnp.einsum('bhcm,bhmv->bhcv', qk_c, u)
        S = S * gl_c[..., None, None] + jnp.einsum('bhck,bhcv->bhkv', kd_c, u)
        return S, o

    xs = tuple(jnp.moveaxis(t, 2, 0) for t in (u_part, w_part, qk, q_dec, k_dec, g_last))
    S, o = lax.scan(step, s0, xs)
    return _unchunk(jnp.moveaxis(o, 0, 2), L), S


def retention_chunked(q, k, v, log_gamma, s0):
    B, L, H, _ = q.shape
    C = min(CHUNK, L)
    pad = (-L) % C
    lg = jnp.broadcast_to(log_gamma, (B, L, H))
    q, k, v, lg = [_chunk(_pad_len(t, pad), C) for t in (q, k, v, lg)]
    G = jnp.cumsum(lg, -1)
    causal = jnp.tril(jnp.ones((C, C), bool))
    decay = jnp.exp(jnp.where(causal, G[..., :, None] - G[..., None, :], -jnp.inf))
    intra = jnp.einsum('bhncm,bhnmv->bhncv', jnp.einsum('bhncd,bhnmd->bhncm', q, k) * decay, v)
    q_dec = q * jnp.exp(G)[..., None]
    k_dec = k * jnp.exp(G[..., -1:] - G)[..., None]
    g_last = jnp.exp(G[..., -1])

    def step(S, xs):
        in_c, qd_c, kd_c, v_c, gl_c = xs
        o = in_c + jnp.einsum('bhck,bhkv->bhcv', qd_c, S)
        S = S * gl_c[..., None, None] + jnp.einsum('bhck,bhcv->bhkv', kd_c, v_c)
        return S, o

    xs = tuple(jnp.moveaxis(t, 2, 0) for t in (intra, q_dec, k_dec, v, g_last))
    S, o = lax.scan(step, s0, xs)
    return _unchunk(jnp.moveaxis(o, 0, 2), L), S


def hybrid_mixer(h, pos0, conv_buf, s_gdn, s_ret, w_in, conv_w, a_log, dt_bias, gdn_norm, ret_norm, w_out):
    B, L, _ = h.shape
    f32 = jnp.float32
    proj = jnp.einsum('bld,de->ble', h, w_in).astype(f32)
    i0 = GDN_CONV_DIM
    i1 = i0 + GDN_V
    i2 = i1 + GDN_HEADS
    i3 = i2 + GDN_HEADS
    i4 = i3 + RET_QK
    i5 = i4 + RET_QK
    i6 = i5 + RET_V
    conv_in, z, b, a, rq, rk, rv, rg = jnp.split(proj, [i0, i1, i2, i3, i4, i5, i6], axis=-1)
    conv_out, new_buf = short_conv(conv_in, conv_buf.astype(f32), conv_w.astype(f32))
    gq, gk, gv = jnp.split(conv_out, [GDN_QK, 2 * GDN_QK], axis=-1)
    gq = l2norm(gq.reshape(B, L, GDN_HEADS, GDN_DK)) * (GDN_DK ** -0.5)
    gk = l2norm(gk.reshape(B, L, GDN_HEADS, GDN_DK))
    gv = gv.reshape(B, L, GDN_HEADS, GDN_DV)
    beta = jax.nn.sigmoid(b)
    g = -jnp.exp(a_log.astype(f32)) * jax.nn.softplus(a + dt_bias.astype(f32))
    o_g, s_gdn_new = gated_delta_chunked(gq, gk, gv, g, beta, s_gdn.astype(f32))
    o_g = head_rms(o_g, gdn_norm) * jax.nn.silu(z.reshape(B, L, GDN_HEADS, GDN_DV))
    pos = jnp.arange(L, dtype=f32) + pos0
    rq = rotary(rq.reshape(B, L, RET_HEADS, RET_DK), pos)
    rk = rotary(rk.reshape(B, L, RET_HEADS, RET_DK), pos) * (RET_DK ** -0.5)
    rv = rv.reshape(B, L, RET_HEADS, RET_DV)
    log_gamma = jnp.log1p(-jnp.exp2(-5.0 - jnp.arange(RET_HEADS, dtype=f32)))
    o_r, s_ret_new = retention_chunked(rq, rk, rv, log_gamma, s_ret.astype(f32))
    o_r = head_rms(o_r, ret_norm.reshape(RET_HEADS, RET_DV)) * jax.nn.silu(rg.reshape(B, L, RET_HEADS, RET_DV))
    mix = jnp.concatenate([o_g.reshape(B, L, GDN_V), o_r.reshape(B, L, RET_V)], -1).astype(h.dtype)
    return jnp.einsum('ble,ed->bld', mix, w_out), new_buf, s_gdn_new, s_ret_new


def memory_kv(mem, norm_mem_in, w_mem_k, w_mem_v):
    B = mem.shape[0]
    m = rms_norm(mem, norm_mem_in)
    k = jnp.einsum('bmd,de->bme', m, w_mem_k).reshape(B, N_MEM, MEM_HEADS, MEM_HEAD_DIM)
    v = jnp.einsum('bmd,de->bme', m, w_mem_v).reshape(B, N_MEM, MEM_HEADS, MEM_HEAD_DIM)
    return k, v


def memory_attention(h, mem_k, mem_v, w_mem_q, w_mem_o):
    B, L, _ = h.shape
    q = jnp.einsum('bld,de->ble', h, w_mem_q).reshape(B, L, MEM_HEADS, MEM_HEAD_DIM)
    s = jnp.einsum('blhd,bmhd->bhlm', q, mem_k.astype(q.dtype)).astype(jnp.float32) * (MEM_HEAD_DIM ** -0.5)
    p = jax.nn.softmax(s, axis=-1).astype(q.dtype)
    o = jnp.einsum('bhlm,bmhd->blhd', p, mem_v.astype(q.dtype)).reshape(B, L, MEM_HEADS * MEM_HEAD_DIM)
    return jnp.einsum('ble,ed->bld', o, w_mem_o)


def swiglu(h, w_gate, w_up, w_down):
    a = jnp.einsum('bld,df->blf', h, w_gate)
    u = jnp.einsum('bld,df->blf', h, w_up)
    return jnp.einsum('blf,fd->bld', jax.nn.silu(a) * u, w_down)


def layer(x, pos0, conv_buf, s_gdn, s_ret, mem_k, mem_v,
          norm_mix, w_in, conv_w, gdn_a_log, gdn_dt_bias, gdn_norm, ret_norm, w_out,
          norm_cross, w_mem_q, w_mem_o, norm_ffn, w_gate, w_up, w_down):
    mix, conv_buf, s_gdn, s_ret = hybrid_mixer(rms_norm(x, norm_mix), pos0, conv_buf, s_gdn, s_ret,
                                               w_in, conv_w, gdn_a_log, gdn_dt_bias, gdn_norm, ret_norm, w_out)
    x = x + mix
    x = x + memory_attention(rms_norm(x, norm_cross), mem_k, mem_v, w_mem_q, w_mem_o)
    x = x + swiglu(rms_norm(x, norm_ffn), w_gate, w_up, w_down)
    return x, conv_buf, s_gdn, s_ret


def setup_inputs(seed: int = 0) -> dict:
    key = jax.random.key(seed)
    ks = jax.random.split(key, 32)
    f32 = jnp.float32
    nrm = lambda k, shape, s: jax.random.normal(k, shape, f32) * s
    gain = lambda k, shape: 1.0 + 0.02 * jax.random.normal(k, shape, f32)
    L = DEPTH
    return {
        "x_prompt": nrm(ks[0], (BATCH, SEQ, D_MODEL), 1.0),
        "x_sample": nrm(ks[1], (DEC_BATCH, DEC_SEQ, D_MODEL), 1.0),
        "mem_prompt": nrm(ks[2], (BATCH, N_MEM, D_MODEL), 1.0),
        "cache_mem_k": nrm(ks[3], (L, DEC_BATCH, N_MEM, MEM_HEADS, MEM_HEAD_DIM), 1.0),
        "cache_mem_v": nrm(ks[4], (L, DEC_BATCH, N_MEM, MEM_HEADS, MEM_HEAD_DIM), 1.0),
        "state_gdn": nrm(ks[5], (L, DEC_BATCH, GDN_HEADS, GDN_DK, GDN_DV), 0.1),
        "state_gdn_conv": nrm(ks[6], (L, DEC_BATCH, CONV_WIDTH - 1, GDN_CONV_DIM), 1.0),
        "state_ret": nrm(ks[7], (L, DEC_BATCH, RET_HEADS, RET_DK, RET_DV), 0.1),
        "norm_mix": gain(ks[8], (L, D_MODEL)),
        "w_in": nrm(ks[9], (L, D_MODEL, IN_WIDTH), D_MODEL ** -0.5),
        "conv_w": nrm(ks[10], (L, GDN_CONV_DIM, CONV_WIDTH), CONV_WIDTH ** -0.5),
        "gdn_a_log": jnp.log(jax.random.uniform(ks[11], (L, GDN_HEADS), f32, 1.0, 16.0)),
        "gdn_dt_bias": nrm(ks[12], (L, GDN_HEADS), 0.1),
        "gdn_norm": gain(ks[13], (L, GDN_DV)),
        "ret_norm": gain(ks[14], (L, RET_V)),
        "w_out": nrm(ks[15], (L, MIX_WIDTH, D_MODEL), MIX_WIDTH ** -0.5),
        "norm_mem_in": gain(ks[16], (L, D_MODEL)),
        "norm_cross": gain(ks[17], (L, D_MODEL)),
        "w_mem_q": nrm(ks[18], (L, D_MODEL, MEM_HEADS * MEM_HEAD_DIM), D_MODEL ** -0.5),
        "w_mem_k": nrm(ks[19], (L, D_MODEL, MEM_HEADS * MEM_HEAD_DIM), D_MODEL ** -0.5),
        "w_mem_v": nrm(ks[20], (L, D_MODEL, MEM_HEADS * MEM_HEAD_DIM), D_MODEL ** -0.5),
        "w_mem_o": nrm(ks[21], (L, MEM_HEADS * MEM_HEAD_DIM, D_MODEL), (MEM_HEADS * MEM_HEAD_DIM) ** -0.5),
        "norm_ffn": gain(ks[22], (L, D_MODEL)),
        "w_gate": nrm(ks[23], (L, D_MODEL, D_FF), D_MODEL ** -0.5),
        "w_up": nrm(ks[24], (L, D_MODEL, D_FF), D_MODEL ** -0.5),
        "w_down": nrm(ks[25], (L, D_FF, D_MODEL), D_FF ** -0.5),
        "norm_final": gain(ks[26], (D_MODEL,)),
    }


def reference(x_prompt, x_sample, mem_prompt, cache_mem_k, cache_mem_v, state_gdn, state_gdn_conv, state_ret,
              norm_mix, w_in, conv_w, gdn_a_log, gdn_dt_bias, gdn_norm, ret_norm, w_out,
              norm_mem_in, norm_cross, w_mem_q, w_mem_k, w_mem_v, w_mem_o,
              norm_ffn, w_gate, w_up, w_down, norm_final):
    f32 = jnp.float32

    def weights(l):
        return (norm_mix[l], w_in[l], conv_w[l], gdn_a_log[l], gdn_dt_bias[l], gdn_norm[l], ret_norm[l], w_out[l],
                norm_cross[l], w_mem_q[l], w_mem_o[l], norm_ffn[l], w_gate[l], w_up[l], w_down[l])

    pdt = x_prompt.dtype
    y = x_prompt
    p_gdn, p_conv, p_ret, p_mk, p_mv = [], [], [], [], []
    for l in range(DEPTH):
        mk, mv = memory_kv(mem_prompt, norm_mem_in[l], w_mem_k[l], w_mem_v[l])
        conv0 = jnp.zeros((BATCH, CONV_WIDTH - 1, GDN_CONV_DIM), f32)
        sg0 = jnp.zeros((BATCH, GDN_HEADS, GDN_DK, GDN_DV), f32)
        sr0 = jnp.zeros((BATCH, RET_HEADS, RET_DK, RET_DV), f32)
        y, cb, sg, sr = layer(y, 0, conv0, sg0, sr0, mk, mv, *weights(l))
        p_gdn.append(sg.astype(pdt))
        p_conv.append(cb.astype(pdt))
        p_ret.append(sr.astype(pdt))
        p_mk.append(mk)
        p_mv.append(mv)
    y_prompt = rms_norm(y, norm_final)

    sdt = x_sample.dtype
    ys = x_sample
    s_gdn, s_conv, s_ret = [], [], []
    for l in range(DEPTH):
        ys, cb, sg, sr = layer(ys, PAST_LEN, state_gdn_conv[l], state_gdn[l], state_ret[l],
                               cache_mem_k[l], cache_mem_v[l], *weights(l))
        s_gdn.append(sg.astype(sdt))
        s_conv.append(cb.astype(sdt))
        s_ret.append(sr.astype(sdt))
    y_sample = rms_norm(ys, norm_final)

    return (y_prompt, y_sample,
            jnp.stack(p_gdn), jnp.stack(p_conv), jnp.stack(p_ret), jnp.stack(p_mk), jnp.stack(p_mv),
            jnp.stack(s_gdn), jnp.stack(s_conv), jnp.stack(s_ret))
```

```python
import functools
import math

import jax
import jax.numpy as jnp
from jax import lax
from jax.experimental import pallas as pl
from jax.experimental.pallas import tpu as pltpu

f32 = jnp.float32
bf16 = jnp.bfloat16

D_MODEL = 2048
GDN_HEADS = 8
GDN_DK = 128
GDN_DV = 128
RET_HEADS = 4
RET_DK = 256
RET_DV = 256
CONV_WIDTH = 4
N_MEM = 256
MEM_HEADS = 4
MEM_HEAD_DIM = D_MODEL // MEM_HEADS
D_FF = -(-8 * D_MODEL // (3 * 256)) * 256
EPS = 1e-6
ROPE_BASE = 10000.0
PAST_LEN = 16384

GDN_QK = GDN_HEADS * GDN_DK
GDN_V = GDN_HEADS * GDN_DV
GDN_CONV_DIM = 2 * GDN_QK + GDN_V
RET_QK = RET_HEADS * RET_DK
RET_V = RET_HEADS * RET_DV

LANES = 128
SUBLANES = 8
VMEM_LIMIT = 56 * 1024 * 1024

GDN_CHUNK = 64
RET_CHUNK = 256

COL_Z = GDN_CONV_DIM // GDN_V
COL_RQ = COL_Z + 1
COL_RK = COL_Z + 2
COL_RV = COL_Z + 3
COL_RG = COL_Z + 4
MAIN_WIDTH = GDN_CONV_DIM + GDN_V + 2 * RET_QK + 2 * RET_V


def _params(sem):
    return pltpu.CompilerParams(dimension_semantics=sem, vmem_limit_bytes=VMEM_LIMIT)


def _dot(a, b):
    return jnp.dot(a, b, preferred_element_type=f32)


def _dot_nt(a, b):
    return lax.dot_general(a, b, (((1,), (1,)), ((), ())), preferred_element_type=f32)


def _split2(a):
    hi = a.astype(bf16)
    lo = (a - hi.astype(f32)).astype(bf16)
    return hi, lo


def _dot3(a, b):
    ah, al = _split2(a)
    bh, bl = _split2(b)
    return _dot(ah, bh) + (_dot(ah, bl) + _dot(al, bh))


def _dot_exact_lhs(a_bf16, b):
    b1 = b.astype(bf16)
    r1 = b - b1.astype(f32)
    b2 = r1.astype(bf16)
    b3 = (r1 - b2.astype(f32)).astype(bf16)
    return _dot(a_bf16, b1) + (_dot(a_bf16, b2) + _dot(a_bf16, b3))


def _rms(x, g):
    return x * lax.rsqrt(jnp.mean(x * x, axis=-1, keepdims=True) + EPS) * g


def _nm_body(*refs, norm, residual):
    if residual:
        x_ref, g_ref, w_ref, r_ref, o_ref, xn_ref = refs
    else:
        x_ref, g_ref, w_ref, o_ref, xn_ref = refs

    @pl.when(pl.program_id(1) == 0)
    def _():
        x = x_ref[...].astype(f32)
        if norm:
            x = _rms(x, g_ref[...])
        xn_ref[...] = x.astype(bf16)

    acc = _dot(xn_ref[...], w_ref[...])
    if residual:
        acc = r_ref[...] + acc
    o_ref[...] = acc.astype(o_ref.dtype)


def _norm_matmul(x, gain, w, *, name, residual=None, norm=True, tm=512, tn=512, out_dtype=f32):
    m, k = x.shape
    n = w.shape[1]
    tm = min(tm, m)
    tn = min(tn, n)
    assert m % tm == 0 and n % tn == 0
    in_specs = [
        pl.BlockSpec((tm, k), lambda i, j: (i, 0)),
        pl.BlockSpec((1, k), lambda i, j: (0, 0)),
        pl.BlockSpec((k, tn), lambda i, j: (0, j)),
    ]
    args = [x, gain.reshape(1, k).astype(f32), w]
    if residual is not None:
        in_specs.append(pl.BlockSpec((tm, tn), lambda i, j: (i, j)))
        args.append(residual)
    return pl.pallas_call(
        functools.partial(_nm_body, norm=norm, residual=residual is not None),
        grid=(m // tm, n // tn),
        in_specs=in_specs,
        out_specs=pl.BlockSpec((tm, tn), lambda i, j: (i, j)),
        out_shape=jax.ShapeDtypeStruct((m, n), out_dtype),
        scratch_shapes=[pltpu.VMEM((tm, k), bf16)],
        compiler_params=_params(("parallel", "arbitrary")),
        name=name,
    )(*args)


def _gdn_pre_body(u_ref, ba_ref, hp_ref, cw_ref, gp_ref, q_ref, k_ref, v_ref, gt_ref, carry_ref, *, nb, lb):
    t = pl.program_id(1)

    @pl.when(t == 0)
    def _():
        carry_ref[...] = hp_ref[...]

    u = u_ref[...]
    c = u.shape[-1]
    ext = jnp.concatenate([carry_ref[...], u], axis=1)
    carry_ref[...] = u[:, lb - SUBLANES:, :]
    ext2 = ext.reshape(nb * (SUBLANES + lb), c)
    cw = cw_ref[...]
    acc = ext2 * cw[CONV_WIDTH - 1:CONV_WIDTH, :]
    for s in range(1, CONV_WIDTH):
        acc = acc + pltpu.roll(ext2, s, axis=0) * cw[CONV_WIDTH - 1 - s:CONV_WIDTH - s, :]
    conv = acc.reshape(nb, SUBLANES + lb, c)[:, SUBLANES:, :].reshape(nb * lb, c)
    act = jax.nn.silu(conv)
    for h in range(GDN_HEADS):
        sl = slice(h * GDN_DK, (h + 1) * GDN_DK)
        qh = act[:, sl]
        kh = act[:, GDN_QK + h * GDN_DK:GDN_QK + (h + 1) * GDN_DK]
        qn = qh * lax.rsqrt(jnp.sum(qh * qh, axis=-1, keepdims=True) + EPS) * (GDN_DK ** -0.5)
        kn = kh * lax.rsqrt(jnp.sum(kh * kh, axis=-1, keepdims=True) + EPS)
        q_ref[:, sl] = qn
        k_ref[:, sl] = kn
    v_ref[...] = act[:, 2 * GDN_QK:]
    ba = ba_ref[...]
    gp = gp_ref[...]
    beta = jax.nn.sigmoid(ba)
    g = -jnp.exp(gp[0:1, :]) * jax.nn.softplus(ba + gp[1:2, :])
    lane = lax.broadcasted_iota(jnp.int32, ba.shape, 1)
    gt_ref[...] = jnp.where(lane < GDN_HEADS, beta, g)


def _gdn_pre(proj3, ba, hp, conv_w_t, gparams, *, nb, lb):
    nseq, l, _ = proj3.shape
    t = nseq * l
    rows = nb * lb
    nt = l // lb
    assert nseq % nb == 0 and l % lb == 0 and (nb == 1 or nt == 1)
    row_map = lambda s, j: (s * nt + j, 0)
    outs = pl.pallas_call(
        functools.partial(_gdn_pre_body, nb=nb, lb=lb),
        grid=(nseq // nb, nt),
        in_specs=[
            pl.BlockSpec((nb, lb, GDN_CONV_DIM), lambda s, j: (s, j, 0)),
            pl.BlockSpec((rows, LANES), row_map),
            pl.BlockSpec((nb, SUBLANES, GDN_CONV_DIM), lambda s, j: (s, 0, 0)),
            pl.BlockSpec((CONV_WIDTH, GDN_CONV_DIM), lambda s, j: (0, 0)),
            pl.BlockSpec((SUBLANES, LANES), lambda s, j: (0, 0)),
        ],
        out_specs=[
            pl.BlockSpec((rows, GDN_QK), row_map),
            pl.BlockSpec((rows, GDN_QK), row_map),
            pl.BlockSpec((rows, GDN_V), row_map),
            pl.BlockSpec((rows, LANES), row_map),
        ],
        out_shape=[
            jax.ShapeDtypeStruct((t, GDN_QK), f32),
            jax.ShapeDtypeStruct((t, GDN_QK), f32),
            jax.ShapeDtypeStruct((t, GDN_V), f32),
            jax.ShapeDtypeStruct((t, LANES), f32),
        ],
        scratch_shapes=[pltpu.VMEM((nb, SUBLANES, GDN_CONV_DIM), f32)],
        compiler_params=_params(("parallel", "arbitrary")),
        name="gdn_pre",
    )(proj3, ba, hp, conv_w_t, gparams)
    return outs


def _gdn_chunk_body(q_ref, k_ref, v_ref, gt_ref, s0_ref, o_ref, s_ref, *, c):
    @pl.when(pl.program_id(1) == 0)
    def _():
        s_ref[...] = s0_ref[...]

    nfac = int(round(math.log2(c)))
    assert 2 ** nfac == c
    gates = gt_ref[...]
    row = lax.broadcasted_iota(jnp.int32, (c, c), 0)
    col = lax.broadcasted_iota(jnp.int32, (c, c), 1)
    causal = row >= col
    strict = row > col
    eye = (row == col).astype(f32)
    g_cum = _dot_exact_lhs(causal.astype(bf16), gates)
    g_cum_t = g_cum.T
    for h in range(GDN_HEADS):
        sl = slice(h * GDN_DK, (h + 1) * GDN_DK)
        q = q_ref[:, sl]
        k = k_ref[:, sl]
        v = v_ref[:, sl]
        beta = gates[:, h:h + 1]
        gc = g_cum[:, GDN_HEADS + h:GDN_HEADS + h + 1]
        gr = g_cum_t[GDN_HEADS + h:GDN_HEADS + h + 1, :]
        decay = jnp.exp(jnp.where(causal, gc - gr, -jnp.inf))
        qb = q.astype(bf16)
        kb = k.astype(bf16)
        kk = _dot_nt(kb, kb)
        qk = _dot_nt(qb, kb) * decay
        a = jnp.where(strict, -(beta * kk * decay), 0.0)
        p = eye + a
        m = a
        for _ in range(nfac - 1):
            m = _dot3(m, m)
            p = p + _dot3(p, m)
        eg = jnp.exp(gc)
        rhs = jnp.concatenate([v * beta, k * (beta * eg)], axis=1)
        sol = _dot3(p, rhs)
        u_part = sol[:, :GDN_DV]
        w_part = sol[:, GDN_DV:]
        s = s_ref[0, h]
        sb = s.astype(bf16)
        wq = jnp.concatenate([w_part, q * eg], axis=0).astype(bf16)
        wqs = _dot(wq, sb)
        u = u_part - wqs[:c]
        ub = u.astype(bf16)
        o = wqs[c:] + _dot(qk.astype(bf16), ub)
        g_last = gc[c - 1:c, :]
        k_dec = k * jnp.exp(g_last - gc)
        s_ref[0, h] = s * jnp.exp(g_last) + _dot(k_dec.T.astype(bf16), ub)
        o_ref[:, sl] = o


def _gdn_chunk(q, k, v, gates, s0, *, l, c):
    t = q.shape[0]
    nseq = t // l
    nc = l // c
    row_map = lambda b, n: (b * nc + n, 0)
    st_spec = pl.BlockSpec((1, GDN_HEADS, GDN_DK, GDN_DV), lambda b, n: (b, 0, 0, 0))
    return pl.pallas_call(
        functools.partial(_gdn_chunk_body, c=c),
        grid=(nseq, nc),
        in_specs=[
            pl.BlockSpec((c, GDN_QK), row_map),
            pl.BlockSpec((c, GDN_QK), row_map),
            pl.BlockSpec((c, GDN_V), row_map),
            pl.BlockSpec((c, LANES), row_map),
            st_spec,
        ],
        out_specs=[pl.BlockSpec((c, GDN_V), row_map), st_spec],
        out_shape=[
            jax.ShapeDtypeStruct((t, GDN_V), f32),
            jax.ShapeDtypeStruct((nseq, GDN_HEADS, GDN_DK, GDN_DV), f32),
        ],
        compiler_params=_params(("parallel", "arbitrary")),
        name="gdn_chunk",
    )(q, k, v, gates, s0)


def _ret_chunk_body(q_ref, k_ref, v_ref, cos_ref, sin_ref, s0_ref, o_ref, s_ref, *, c):
    @pl.when(pl.program_id(1) == 0)
    def _():
        s_ref[...] = s0_ref[...]

    cos = cos_ref[...]
    sin = sin_ref[...]
    lane = lax.broadcasted_iota(jnp.int32, (c, RET_DK), 1)
    even = (lane % 2) == 0
    row = lax.broadcasted_iota(jnp.int32, (c, c), 0)
    col = lax.broadcasted_iota(jnp.int32, (c, c), 1)
    causal = row >= col
    dist = (row - col).astype(f32)
    tpos = lax.broadcasted_iota(jnp.int32, (c, 1), 0).astype(f32)

    def rot(x):
        swapped = jnp.where(even, pltpu.roll(x, RET_DK - 1, axis=1), pltpu.roll(x, 1, axis=1))
        return x * cos + swapped * sin

    for h in range(RET_HEADS):
        sl = slice(h * RET_DK, (h + 1) * RET_DK)
        lg = math.log1p(-(2.0 ** (-5.0 - h)))
        q = rot(q_ref[0, :, sl])
        k = rot(k_ref[0, :, sl]) * (RET_DK ** -0.5)
        vb = v_ref[0, :, sl].astype(bf16)
        decay = jnp.exp(jnp.where(causal, dist * lg, -jnp.inf))
        qk = _dot_nt(q.astype(bf16), k.astype(bf16)) * decay
        s = s_ref[0, h]
        q_dec = q * jnp.exp((tpos + 1.0) * lg)
        o = _dot(qk.astype(bf16), vb) + _dot(q_dec.astype(bf16), s.astype(bf16))
        k_dec = k * jnp.exp((float(c - 1) - tpos) * lg)
        s_ref[0, h] = s * math.exp(c * lg) + _dot(k_dec.T.astype(bf16), vb)
        o_ref[:, sl] = o


def _ret_chunk(proj3, cos2, sin2, s0, *, c):
    nseq, l, _ = proj3.shape
    nc = l // c
    st_spec = pl.BlockSpec((1, RET_HEADS, RET_DK, RET_DV), lambda b, n: (b, 0, 0, 0))
    return pl.pallas_call(
        functools.partial(_ret_chunk_body, c=c),
        grid=(nseq, nc),
        in_specs=[
            pl.BlockSpec((1, c, RET_QK), lambda b, n: (b, n, COL_RQ)),
            pl.BlockSpec((1, c, RET_QK), lambda b, n: (b, n, COL_RK)),
            pl.BlockSpec((1, c, RET_V), lambda b, n: (b, n, COL_RV)),
            pl.BlockSpec((c, RET_DK), lambda b, n: (n, 0)),
            pl.BlockSpec((c, RET_DK), lambda b, n: (n, 0)),
            st_spec,
        ],
        out_specs=[pl.BlockSpec((c, RET_V), lambda b, n: (b * nc + n, 0)), st_spec],
        out_shape=[
            jax.ShapeDtypeStruct((nseq * l, RET_V), f32),
            jax.ShapeDtypeStruct((nseq, RET_HEADS, RET_DK, RET_DV), f32),
        ],
        compiler_params=_params(("parallel", "arbitrary")),
        name="ret_chunk",
    )(proj3, proj3, proj3, cos2, sin2, s0)


def _mix_out_body(og_ref, z_ref, or_ref, rg_ref, gn_ref, rn_ref, w_ref, x_ref, o_ref, mix_ref):
    @pl.when(pl.program_id(1) == 0)
    def _():
        gn = gn_ref[...]
        for h in range(GDN_HEADS):
            sl = slice(h * GDN_DV, (h + 1) * GDN_DV)
            y = _rms(og_ref[:, sl], gn) * jax.nn.silu(z_ref[:, sl])
            mix_ref[:, sl] = y.astype(bf16)
        for h in range(RET_HEADS):
            sl = slice(h * RET_DV, (h + 1) * RET_DV)
            y = _rms(or_ref[:, sl], rn_ref[:, sl]) * jax.nn.silu(rg_ref[:, sl])
            mix_ref[:, GDN_V + h * RET_DV:GDN_V + (h + 1) * RET_DV] = y.astype(bf16)

    o_ref[...] = x_ref[...] + _dot(mix_ref[...], w_ref[...])


def _mix_out(o_g, o_r, proj, gdn_norm, ret_norm, w_out, x, *, tm=512, tn=512):
    t = x.shape[0]
    tm = min(tm, t)
    return pl.pallas_call(
        _mix_out_body,
        grid=(t // tm, D_MODEL // tn),
        in_specs=[
            pl.BlockSpec((tm, GDN_V), lambda i, j: (i, 0)),
            pl.BlockSpec((tm, GDN_V), lambda i, j: (i, COL_Z)),
            pl.BlockSpec((tm, RET_V), lambda i, j: (i, 0)),
            pl.BlockSpec((tm, RET_V), lambda i, j: (i, COL_RG)),
            pl.BlockSpec((1, GDN_DV), lambda i, j: (0, 0)),
            pl.BlockSpec((1, RET_V), lambda i, j: (0, 0)),
            pl.BlockSpec((GDN_V + RET_V, tn), lambda i, j: (0, j)),
            pl.BlockSpec((tm, tn), lambda i, j: (i, j)),
        ],
        out_specs=pl.BlockSpec((tm, tn), lambda i, j: (i, j)),
        out_shape=jax.ShapeDtypeStruct((t, D_MODEL), f32),
        scratch_shapes=[pltpu.VMEM((tm, GDN_V + RET_V), bf16)],
        compiler_params=_params(("parallel", "arbitrary")),
        name="mix_out",
    )(o_g, proj, o_r, proj, gdn_norm.reshape(1, GDN_DV).astype(f32), ret_norm.reshape(1, RET_V).astype(f32), w_out, x)


def _attn_body(q_ref, k_ref, v_ref, o_ref):
    for h in range(MEM_HEADS):
        sl = slice(h * MEM_HEAD_DIM, (h + 1) * MEM_HEAD_DIM)
        qh = q_ref[:, sl].astype(bf16)
        kh = k_ref[0, :, sl].astype(bf16)
        vh = v_ref[0, :, sl].astype(bf16)
        s = _dot_nt(qh, kh) * (MEM_HEAD_DIM ** -0.5)
        e = jnp.exp(s - jnp.max(s, axis=-1, keepdims=True))
        p = e / jnp.sum(e, axis=-1, keepdims=True)
        o_ref[:, sl] = _dot(p.astype(bf16), vh)


def _attention(q, mem_k, mem_v, *, l, tq):
    t = q.shape[0]
    nq = l // tq
    width = MEM_HEADS * MEM_HEAD_DIM
    return pl.pallas_call(
        _attn_body,
        grid=(t // l, nq),
        in_specs=[
            pl.BlockSpec((tq, width), lambda b, i: (b * nq + i, 0)),
            pl.BlockSpec((1, N_MEM, width), lambda b, i: (b, 0, 0)),
            pl.BlockSpec((1, N_MEM, width), lambda b, i: (b, 0, 0)),
        ],
        out_specs=pl.BlockSpec((tq, width), lambda b, i: (b * nq + i, 0)),
        out_shape=jax.ShapeDtypeStruct((t, width), f32),
        compiler_params=_params(("parallel", "arbitrary")),
        name="mem_attn",
    )(q, mem_k, mem_v)


def _ffn_body(x_ref, g_ref, wg_ref, wu_ref, wd_ref, gf_ref, o_ref, xn_ref, acc_ref):
    f = pl.program_id(1)

    @pl.when(f == 0)
    def _():
        xn_ref[...] = _rms(x_ref[...], g_ref[...]).astype(bf16)
        acc_ref[...] = jnp.zeros_like(acc_ref)

    xn = xn_ref[...]
    a = _dot(xn, wg_ref[...])
    u = _dot(xn, wu_ref[...])
    hidden = (jax.nn.silu(a) * u).astype(bf16)
    acc_ref[...] += _dot(hidden, wd_ref[...])

    @pl.when(f == pl.num_programs(1) - 1)
    def _():
        o_ref[...] = _rms(x_ref[...] + acc_ref[...], gf_ref[...])


def _ffn(x, norm_ffn, w_gate, w_up, w_down, norm_final, *, tm=512, tf=512):
    t = x.shape[0]
    tm = min(tm, t)
    assert D_FF % tf == 0
    return pl.pallas_call(
        _ffn_body,
        grid=(t // tm, D_FF // tf),
        in_specs=[
            pl.BlockSpec((tm, D_MODEL), lambda i, f: (i, 0)),
            pl.BlockSpec((1, D_MODEL), lambda i, f: (0, 0)),
            pl.BlockSpec((D_MODEL, tf), lambda i, f: (0, f)),
            pl.BlockSpec((D_MODEL, tf), lambda i, f: (0, f)),
            pl.BlockSpec((tf, D_MODEL), lambda i, f: (f, 0)),
            pl.BlockSpec((1, D_MODEL), lambda i, f: (0, 0)),
        ],
        out_specs=pl.BlockSpec((tm, D_MODEL), lambda i, f: (i, 0)),
        out_shape=jax.ShapeDtypeStruct((t, D_MODEL), f32),
        scratch_shapes=[pltpu.VMEM((tm, D_MODEL), bf16), pltpu.VMEM((tm, D_MODEL), f32)],
        compiler_params=_params(("parallel", "arbitrary")),
        name="ffn",
    )(x, norm_ffn.reshape(1, D_MODEL).astype(f32), w_gate, w_up, w_down, norm_final.reshape(1, D_MODEL).astype(f32))


def _rotary_tables(l, pos0):
    half = RET_DK // 2
    inv = ROPE_BASE ** (-jnp.linspace(0.0, 1.0, half, dtype=f32))
    pos = jnp.arange(l, dtype=f32) + pos0
    ang = pos[:, None] * inv[None, :]
    cos2 = jnp.repeat(jnp.cos(ang), 2, axis=1)
    sign = jnp.tile(jnp.array([-1.0, 1.0], f32), half)
    sin2 = jnp.repeat(jnp.sin(ang), 2, axis=1) * sign[None, :]
    return cos2, sin2


def _layer(x3, pos0, conv_hist, s_gdn, s_ret, mem_k, mem_v, wts, *, pre_nb, pre_lb, gdn_c, ret_c, attn_tq):
    nseq, l, _ = x3.shape
    t = nseq * l
    x = x3.reshape(t, D_MODEL)
    proj = _norm_matmul(x, wts["norm_mix"], wts["w_in_main"], name="in_proj")
    ba = _norm_matmul(x, wts["norm_mix"], wts["w_in_ba"], name="in_proj_ba")
    proj3 = proj.reshape(nseq, l, MAIN_WIDTH)
    hp = jnp.pad(conv_hist.astype(f32), ((0, 0), (SUBLANES - (CONV_WIDTH - 1), 0), (0, 0)))
    q, k, v, gates = _gdn_pre(proj3, ba, hp, wts["conv_w_t"], wts["gparams"], nb=pre_nb, lb=pre_lb)
    o_g, s_gdn_new = _gdn_chunk(q, k, v, gates, s_gdn.astype(f32), l=l, c=gdn_c)
    cos2, sin2 = _rotary_tables(l, pos0)
    o_r, s_ret_new = _ret_chunk(proj3, cos2, sin2, s_ret.astype(f32), c=ret_c)
    conv_new = proj3[:, l - (CONV_WIDTH - 1):, :GDN_CONV_DIM]
    x1 = _mix_out(o_g, o_r, proj, wts["gdn_norm"], wts["ret_norm"], wts["w_out"], x)
    qm = _norm_matmul(x1, wts["norm_cross"], wts["w_mem_q"], name="mem_q")
    om = _attention(qm, mem_k, mem_v, l=l, tq=attn_tq)
    x2 = _norm_matmul(om, wts["norm_cross"], wts["w_mem_o"], residual=x1, norm=False, name="mem_o")
    y = _ffn(x2, wts["norm_ffn"], wts["w_gate"], wts["w_up"], wts["w_down"], wts["norm_final"])
    return y.reshape(nseq, l, D_MODEL), conv_new, s_gdn_new, s_ret_new


def _prep_weights(norm_mix, w_in, conv_w, gdn_a_log, gdn_dt_bias, gdn_norm, ret_norm, w_out,
                  norm_cross, w_mem_q, w_mem_o, norm_ffn, w_gate, w_up, w_down, norm_final):
    i0 = GDN_CONV_DIM + GDN_V
    i1 = i0 + 2 * GDN_HEADS
    w_main = jnp.concatenate([w_in[:, :i0], w_in[:, i1:]], axis=1).astype(bf16)
    w_ba = jnp.pad(w_in[:, i0:i1], ((0, 0), (0, LANES - 2 * GDN_HEADS))).astype(bf16)
    gparams = jnp.zeros((SUBLANES, LANES), f32)
    gparams = gparams.at[0, GDN_HEADS:2 * GDN_HEADS].set(gdn_a_log.astype(f32))
    gparams = gparams.at[1, GDN_HEADS:2 * GDN_HEADS].set(gdn_dt_bias.astype(f32))
    return {
        "norm_mix": norm_mix, "w_in_main": w_main, "w_in_ba": w_ba,
        "conv_w_t": conv_w.astype(f32).T, "gparams": gparams,
        "gdn_norm": gdn_norm, "ret_norm": ret_norm, "w_out": w_out.astype(bf16),
        "norm_cross": norm_cross, "w_mem_q": w_mem_q.astype(bf16), "w_mem_o": w_mem_o.astype(bf16),
        "norm_ffn": norm_ffn, "w_gate": w_gate.astype(bf16), "w_up": w_up.astype(bf16),
        "w_down": w_down.astype(bf16), "norm_final": norm_final,
    }


def kernel(x_prompt, x_sample, mem_prompt, cache_mem_k, cache_mem_v, state_gdn, state_gdn_conv, state_ret,
           norm_mix, w_in, conv_w, gdn_a_log, gdn_dt_bias, gdn_norm, ret_norm, w_out,
           norm_mem_in, norm_cross, w_mem_q, w_mem_k, w_mem_v, w_mem_o,
           norm_ffn, w_gate, w_up, w_down, norm_final):
    depth = w_in.shape[0]
    assert depth == 1
    batch, seq, _ = x_prompt.shape
    dec_batch, dec_seq, _ = x_sample.shape
    width = MEM_HEADS * MEM_HEAD_DIM
    wts = _prep_weights(norm_mix[0], w_in[0], conv_w[0], gdn_a_log[0], gdn_dt_bias[0], gdn_norm[0], ret_norm[0],
                        w_out[0], norm_cross[0], w_mem_q[0], w_mem_o[0], norm_ffn[0], w_gate[0], w_up[0], w_down[0],
                        norm_final)

    mem2 = mem_prompt.reshape(batch * N_MEM, D_MODEL)
    mk = _norm_matmul(mem2, norm_mem_in[0], w_mem_k[0].astype(bf16), name="mem_k")
    mv = _norm_matmul(mem2, norm_mem_in[0], w_mem_v[0].astype(bf16), name="mem_v")
    y_p, conv_p, sg_p, sr_p = _layer(
        x_prompt, 0.0,
        jnp.zeros((batch, CONV_WIDTH - 1, GDN_CONV_DIM), f32),
        jnp.zeros((batch, GDN_HEADS, GDN_DK, GDN_DV), f32),
        jnp.zeros((batch, RET_HEADS, RET_DK, RET_DV), f32),
        mk.reshape(batch, N_MEM, width), mv.reshape(batch, N_MEM, width), wts,
        pre_nb=1, pre_lb=min(256, seq), gdn_c=min(GDN_CHUNK, seq), ret_c=min(RET_CHUNK, seq), attn_tq=min(512, seq))

    y_s, conv_s, sg_s, sr_s = _layer(
        x_sample, float(PAST_LEN), state_gdn_conv[0], state_gdn[0], state_ret[0],
        cache_mem_k[0].reshape(dec_batch, N_MEM, width), cache_mem_v[0].reshape(dec_batch, N_MEM, width), wts,
        pre_nb=min(16, dec_batch), pre_lb=dec_seq, gdn_c=dec_seq, ret_c=dec_seq, attn_tq=dec_seq)

    pdt = x_prompt.dtype
    sdt = x_sample.dtype
    return (y_p.astype(pdt), y_s.astype(sdt),
            sg_p.astype(pdt)[None], conv_p.astype(pdt)[None], sr_p.astype(pdt)[None],
            mk.reshape(batch, N_MEM, MEM_HEADS, MEM_HEAD_DIM)[None], mv.reshape(batch, N_MEM, MEM_HEADS, MEM_HEAD_DIM)[None],
            sg_s.astype(sdt)[None], conv_s.astype(sdt)[None], sr_s.astype(sdt)[None])
```

```python
import functools
import math

import jax
import jax.numpy as jnp
from jax import lax
from jax.experimental import pallas as pl
from jax.experimental.pallas import tpu as pltpu

f32 = jnp.float32
bf16 = jnp.bfloat16

D_MODEL = 2048
GDN_HEADS = 8
GDN_DK = 128
GDN_DV = 128
RET_HEADS = 4
RET_DK = 256
RET_DV = 256
CONV_WIDTH = 4
N_MEM = 256
MEM_HEADS = 4
MEM_HEAD_DIM = D_MODEL // MEM_HEADS
D_FF = -(-8 * D_MODEL // (3 * 256)) * 256
EPS = 1e-6
ROPE_BASE = 10000.0
PAST_LEN = 16384

GDN_QK = GDN_HEADS * GDN_DK
GDN_V = GDN_HEADS * GDN_DV
GDN_CONV_DIM = 2 * GDN_QK + GDN_V
RET_QK = RET_HEADS * RET_DK
RET_V = RET_HEADS * RET_DV

LANES = 128
SUBLANES = 8
VMEM_LIMIT = 56 * 1024 * 1024

GDN_CHUNK = 64
RET_CHUNK = 256

COL_Z = GDN_CONV_DIM // GDN_V
COL_RQ = COL_Z + 1
COL_RK = COL_Z + 2
COL_RV = COL_Z + 3
COL_RG = COL_Z + 4
MAIN_WIDTH = GDN_CONV_DIM + GDN_V + 2 * RET_QK + 2 * RET_V
PROJ_WIDTH = MAIN_WIDTH + LANES
COL_BA = MAIN_WIDTH // LANES


def _params(sem):
    return pltpu.CompilerParams(dimension_semantics=sem, vmem_limit_bytes=VMEM_LIMIT)


def _dot(a, b):
    return jnp.dot(a, b, preferred_element_type=f32)


def _dot_nt(a, b):
    return lax.dot_general(a, b, (((1,), (1,)), ((), ())), preferred_element_type=f32)


def _split2(a):
    hi = a.astype(bf16)
    lo = (a - hi.astype(f32)).astype(bf16)
    return hi, lo


def _dot3(a, b):
    ah, al = _split2(a)
    bh, bl = _split2(b)
    return _dot(ah, bh) + (_dot(ah, bl) + _dot(al, bh))


def _dot_exact_lhs(a_bf16, b):
    b1 = b.astype(bf16)
    r1 = b - b1.astype(f32)
    b2 = r1.astype(bf16)
    b3 = (r1 - b2.astype(f32)).astype(bf16)
    return _dot(a_bf16, b1) + (_dot(a_bf16, b2) + _dot(a_bf16, b3))


def _rms(x, g):
    return x * lax.rsqrt(jnp.mean(x * x, axis=-1, keepdims=True) + EPS) * g


def _nm_body(*refs, norm, residual):
    if residual:
        x_ref, g_ref, w_ref, r_ref, o_ref, xn_ref = refs
    else:
        x_ref, g_ref, w_ref, o_ref, xn_ref = refs

    @pl.when(pl.program_id(1) == 0)
    def _():
        x = x_ref[...].astype(f32)
        if norm:
            x = _rms(x, g_ref[...])
        xn_ref[...] = x.astype(bf16)

    acc = _dot(xn_ref[...], w_ref[...])
    if residual:
        acc = r_ref[...] + acc
    o_ref[...] = acc.astype(o_ref.dtype)


def _norm_matmul(x, gain, w, *, name, residual=None, norm=True, tm=1024, tn=1024, out_dtype=f32):
    m, k = x.shape
    n = w.shape[1]
    tm = min(tm, m)
    tn = min(tn, n)
    assert m % tm == 0 and n % tn == 0
    in_specs = [
        pl.BlockSpec((tm, k), lambda i, j: (i, 0)),
        pl.BlockSpec((1, k), lambda i, j: (0, 0)),
        pl.BlockSpec((k, tn), lambda i, j: (0, j)),
    ]
    args = [x, gain.reshape(1, k).astype(f32), w]
    if residual is not None:
        in_specs.append(pl.BlockSpec((tm, tn), lambda i, j: (i, j)))
        args.append(residual)
    return pl.pallas_call(
        functools.partial(_nm_body, norm=norm, residual=residual is not None),
        grid=(m // tm, n // tn),
        in_specs=in_specs,
        out_specs=pl.BlockSpec((tm, tn), lambda i, j: (i, j)),
        out_shape=jax.ShapeDtypeStruct((m, n), out_dtype),
        scratch_shapes=[pltpu.VMEM((tm, k), bf16)],
        compiler_params=_params(("parallel", "arbitrary")),
        name=name,
    )(*args)


def _gdn_pre_body(u_ref, ba_ref, hp_ref, cw_ref, gp_ref, q_ref, k_ref, v_ref, gt_ref, carry_ref, *, nb, lb):
    t = pl.program_id(1)

    @pl.when(t == 0)
    def _():
        carry_ref[...] = hp_ref[...]

    u = u_ref[...]
    c = u.shape[-1]
    ext = jnp.concatenate([carry_ref[...], u], axis=1)
    carry_ref[...] = u[:, lb - SUBLANES:, :]
    ext2 = ext.reshape(nb * (SUBLANES + lb), c)
    cw = cw_ref[...]
    acc = ext2 * cw[CONV_WIDTH - 1:CONV_WIDTH, :]
    for s in range(1, CONV_WIDTH):
        acc = acc + pltpu.roll(ext2, s, axis=0) * cw[CONV_WIDTH - 1 - s:CONV_WIDTH - s, :]
    conv = acc.reshape(nb, SUBLANES + lb, c)[:, SUBLANES:, :].reshape(nb * lb, c)
    act = jax.nn.silu(conv)
    for h in range(GDN_HEADS):
        sl = slice(h * GDN_DK, (h + 1) * GDN_DK)
        qh = act[:, sl]
        kh = act[:, GDN_QK + h * GDN_DK:GDN_QK + (h + 1) * GDN_DK]
        qn = qh * lax.rsqrt(jnp.sum(qh * qh, axis=-1, keepdims=True) + EPS) * (GDN_DK ** -0.5)
        kn = kh * lax.rsqrt(jnp.sum(kh * kh, axis=-1, keepdims=True) + EPS)
        q_ref[:, sl] = qn
        k_ref[:, sl] = kn
    v_ref[...] = act[:, 2 * GDN_QK:]
    ba = ba_ref[...]
    gp = gp_ref[...]
    beta = jax.nn.sigmoid(ba)
    g = -jnp.exp(gp[0:1, :]) * jax.nn.softplus(ba + gp[1:2, :])
    lane = lax.broadcasted_iota(jnp.int32, ba.shape, 1)
    gt_ref[...] = jnp.where(lane < GDN_HEADS, beta, g)


def _gdn_pre(proj, hp, conv_w_t, gparams, *, nseq, nb, lb):
    l = proj.shape[0] // nseq
    proj3 = proj.reshape(nseq, l, PROJ_WIDTH)
    t = nseq * l
    rows = nb * lb
    nt = l // lb
    assert nseq % nb == 0 and l % lb == 0 and (nb == 1 or nt == 1)
    row_map = lambda s, j: (s * nt + j, 0)
    outs = pl.pallas_call(
        functools.partial(_gdn_pre_body, nb=nb, lb=lb),
        grid=(nseq // nb, nt),
        in_specs=[
            pl.BlockSpec((nb, lb, GDN_CONV_DIM), lambda s, j: (s, j, 0)),
            pl.BlockSpec((rows, LANES), lambda s, j: (s * nt + j, COL_BA)),
            pl.BlockSpec((nb, SUBLANES, GDN_CONV_DIM), lambda s, j: (s, 0, 0)),
            pl.BlockSpec((CONV_WIDTH, GDN_CONV_DIM), lambda s, j: (0, 0)),
            pl.BlockSpec((SUBLANES, LANES), lambda s, j: (0, 0)),
        ],
        out_specs=[
            pl.BlockSpec((rows, GDN_QK), row_map),
            pl.BlockSpec((rows, GDN_QK), row_map),
            pl.BlockSpec((rows, GDN_V), row_map),
            pl.BlockSpec((rows, LANES), row_map),
        ],
        out_shape=[
            jax.ShapeDtypeStruct((t, GDN_QK), f32),
            jax.ShapeDtypeStruct((t, GDN_QK), f32),
            jax.ShapeDtypeStruct((t, GDN_V), f32),
            jax.ShapeDtypeStruct((t, LANES), f32),
        ],
        scratch_shapes=[pltpu.VMEM((nb, SUBLANES, GDN_CONV_DIM), f32)],
        compiler_params=_params(("parallel", "arbitrary")),
        name="gdn_pre",
    )(proj3, proj, hp, conv_w_t, gparams)
    return outs


def _gdn_chunk_body(q_ref, k_ref, v_ref, gt_ref, s0_ref, o_ref, s_ref, *, c, ns):
    @pl.when(pl.program_id(1) == 0)
    def _():
        s_ref[...] = s0_ref[...]

    r = ns * c
    nfac = int(round(math.log2(c)))
    assert 2 ** nfac == c
    heads = range(GDN_HEADS)
    gates = gt_ref[...]
    row = lax.broadcasted_iota(jnp.int32, (r, r), 0)
    col = lax.broadcasted_iota(jnp.int32, (r, r), 1)
    same = (row // c) == (col // c)
    causal = same & (row >= col)
    strict = same & (row > col)
    eye = (row == col).astype(f32)
    g_cum = _dot_exact_lhs(causal.astype(bf16), gates)
    g_cum_t = g_cum.T
    seq_of_row = lax.broadcasted_iota(jnp.int32, (r, 1), 0) // c
    last = ((row // c) == (col // c)) & ((col % c) == c - 1)
    g_last_rows = _dot_exact_lhs(last.astype(bf16), g_cum)

    q, k, v, beta, gc, eg, decay, qk, p, m = ({} for _ in range(10))
    for h in heads:
        sl = slice(h * GDN_DK, (h + 1) * GDN_DK)
        q[h] = q_ref[:, sl]
        k[h] = k_ref[:, sl]
        v[h] = v_ref[:, sl]
        beta[h] = gates[:, h:h + 1]
        gc[h] = g_cum[:, GDN_HEADS + h:GDN_HEADS + h + 1]
        gr = g_cum_t[GDN_HEADS + h:GDN_HEADS + h + 1, :]
        decay[h] = jnp.exp(jnp.where(causal, gc[h] - gr, -jnp.inf))
        kb = k[h].astype(bf16)
        prod = _dot_nt(jnp.concatenate([q[h], k[h]], axis=0).astype(bf16), kb)
        qk[h] = prod[:r] * decay[h]
        m[h] = jnp.where(strict, -(beta[h] * prod[r:] * decay[h]), 0.0)
        p[h] = eye + m[h]
    for _ in range(nfac - 1):
        for h in heads:
            mb = m[h].astype(bf16)
            m[h] = _dot(mb, mb)
        for h in heads:
            p[h] = p[h] + _dot(p[h].astype(bf16), m[h].astype(bf16))
    u_part, w_part, q_dec = {}, {}, {}
    for h in heads:
        eg[h] = jnp.exp(gc[h])
        rhs = jnp.concatenate([v[h] * beta[h], k[h] * (beta[h] * eg[h])], axis=1)
        sol = _dot(p[h].astype(bf16), rhs.astype(bf16))
        u_part[h] = sol[:, :GDN_DV]
        w_part[h] = sol[:, GDN_DV:]
        q_dec[h] = q[h] * eg[h]
    u, o_state, s_old = {}, {}, {}
    for h in heads:
        us, os_ = [], []
        for j in range(ns):
            rows = slice(j * c, (j + 1) * c)
            s_old[j, h] = s_ref[j, h]
            wq = jnp.concatenate([w_part[h][rows], q_dec[h][rows]], axis=0).astype(bf16)
            wqs = _dot(wq, s_old[j, h].astype(bf16))
            us.append(u_part[h][rows] - wqs[:c])
            os_.append(wqs[c:])
        u[h] = us[0] if ns == 1 else jnp.concatenate(us, axis=0)
        o_state[h] = os_[0] if ns == 1 else jnp.concatenate(os_, axis=0)
    outs, s_new = {}, {}
    for h in heads:
        ub = u[h].astype(bf16)
        outs[h] = o_state[h] + _dot(qk[h].astype(bf16), ub)
        gl = g_last_rows[:, GDN_HEADS + h:GDN_HEADS + h + 1]
        kdt = (k[h] * jnp.exp(gl - gc[h])).T.astype(bf16)
        for j in range(ns):
            uj = ub if ns == 1 else jnp.where(seq_of_row == j, u[h], 0.0).astype(bf16)
            s_new[j, h] = s_old[j, h] * jnp.exp(gl[j * c:j * c + 1, :]) + _dot(kdt, uj)
    for h in heads:
        o_ref[:, h * GDN_DV:(h + 1) * GDN_DV] = outs[h]
        for j in range(ns):
            s_ref[j, h] = s_new[j, h]


def _gdn_chunk(q, k, v, gates, s0, *, l, c, ns):
    t = q.shape[0]
    nseq = t // l
    nc = l // c
    r = ns * c
    assert nseq % ns == 0 and (ns == 1 or nc == 1)
    row_map = lambda b, n: (b * nc + n, 0)
    st_spec = pl.BlockSpec((ns, GDN_HEADS, GDN_DK, GDN_DV), lambda b, n: (b, 0, 0, 0))
    return pl.pallas_call(
        functools.partial(_gdn_chunk_body, c=c, ns=ns),
        grid=(nseq // ns, nc),
        in_specs=[
            pl.BlockSpec((r, GDN_QK), row_map),
            pl.BlockSpec((r, GDN_QK), row_map),
            pl.BlockSpec((r, GDN_V), row_map),
            pl.BlockSpec((r, LANES), row_map),
            st_spec,
        ],
        out_specs=[pl.BlockSpec((r, GDN_V), row_map), st_spec],
        out_shape=[
            jax.ShapeDtypeStruct((t, GDN_V), f32),
            jax.ShapeDtypeStruct((nseq, GDN_HEADS, GDN_DK, GDN_DV), f32),
        ],
        compiler_params=_params(("parallel", "arbitrary")),
        name="gdn_chunk",
    )(q, k, v, gates, s0)


def _ret_chunk_body(q_ref, k_ref, v_ref, cos_ref, sin_ref, s0_ref, o_ref, s_ref, *, c):
    @pl.when(pl.program_id(1) == 0)
    def _():
        s_ref[...] = s0_ref[...]

    cos = cos_ref[...]
    sin = sin_ref[...]
    lane = lax.broadcasted_iota(jnp.int32, (c, RET_DK), 1)
    even = (lane % 2) == 0
    row = lax.broadcasted_iota(jnp.int32, (c, c), 0)
    col = lax.broadcasted_iota(jnp.int32, (c, c), 1)
    causal = row >= col
    dist = (row - col).astype(f32)
    tpos = lax.broadcasted_iota(jnp.int32, (c, 1), 0).astype(f32)

    def rot(x):
        swapped = jnp.where(even, pltpu.roll(x, RET_DK - 1, axis=1), pltpu.roll(x, 1, axis=1))
        return x * cos + swapped * sin

    for h in range(RET_HEADS):
        sl = slice(h * RET_DK, (h + 1) * RET_DK)
        lg = math.log1p(-(2.0 ** (-5.0 - h)))
        q = rot(q_ref[0, :, sl])
        k = rot(k_ref[0, :, sl]) * (RET_DK ** -0.5)
        vb = v_ref[0, :, sl].astype(bf16)
        decay = jnp.exp(jnp.where(causal, dist * lg, -jnp.inf))
        qk = _dot_nt(q.astype(bf16), k.astype(bf16)) * decay
        s = s_ref[0, h]
        q_dec = q * jnp.exp((tpos + 1.0) * lg)
        o = _dot(qk.astype(bf16), vb) + _dot(q_dec.astype(bf16), s.astype(bf16))
        k_dec = k * jnp.exp((float(c - 1) - tpos) * lg)
        s_ref[0, h] = s * math.exp(c * lg) + _dot(k_dec.T.astype(bf16), vb)
        o_ref[:, sl] = o


def _ret_chunk(proj3, cos2, sin2, s0, *, c):
    nseq, l, _ = proj3.shape
    nc = l // c
    st_spec = pl.BlockSpec((1, RET_HEADS, RET_DK, RET_DV), lambda b, n: (b, 0, 0, 0))
    return pl.pallas_call(
        functools.partial(_ret_chunk_body, c=c),
        grid=(nseq, nc),
        in_specs=[
            pl.BlockSpec((1, c, RET_QK), lambda b, n: (b, n, COL_RQ)),
            pl.BlockSpec((1, c, RET_QK), lambda b, n: (b, n, COL_RK)),
            pl.BlockSpec((1, c, RET_V), lambda b, n: (b, n, COL_RV)),
            pl.BlockSpec((c, RET_DK), lambda b, n: (n, 0)),
            pl.BlockSpec((c, RET_DK), lambda b, n: (n, 0)),
            st_spec,
        ],
        out_specs=[pl.BlockSpec((c, RET_V), lambda b, n: (b * nc + n, 0)), st_spec],
        out_shape=[
            jax.ShapeDtypeStruct((nseq * l, RET_V), f32),
            jax.ShapeDtypeStruct((nseq, RET_HEADS, RET_DK, RET_DV), f32),
        ],
        compiler_params=_params(("parallel", "arbitrary")),
        name="ret_chunk",
    )(proj3, proj3, proj3, cos2, sin2, s0)


def _mix_out_body(og_ref, z_ref, or_ref, rg_ref, gn_ref, rn_ref, w_ref, x_ref, o_ref, mix_ref):
    @pl.when(pl.program_id(1) == 0)
    def _():
        gn = gn_ref[...]
        for h in range(GDN_HEADS):
            sl = slice(h * GDN_DV, (h + 1) * GDN_DV)
            y = _rms(og_ref[:, sl], gn) * jax.nn.silu(z_ref[:, sl])
            mix_ref[:, sl] = y.astype(bf16)
        for h in range(RET_HEADS):
            sl = slice(h * RET_DV, (h + 1) * RET_DV)
            y = _rms(or_ref[:, sl], rn_ref[:, sl]) * jax.nn.silu(rg_ref[:, sl])
            mix_ref[:, GDN_V + h * RET_DV:GDN_V + (h + 1) * RET_DV] = y.astype(bf16)

    o_ref[...] = x_ref[...] + _dot(mix_ref[...], w_ref[...])


def _mix_out(o_g, o_r, proj, gdn_norm, ret_norm, w_out, x, *, tm=512, tn=1024):
    t = x.shape[0]
    tm = min(tm, t)
    return pl.pallas_call(
        _mix_out_body,
        grid=(t // tm, D_MODEL // tn),
        in_specs=[
            pl.BlockSpec((tm, GDN_V), lambda i, j: (i, 0)),
            pl.BlockSpec((tm, GDN_V), lambda i, j: (i, COL_Z)),
            pl.BlockSpec((tm, RET_V), lambda i, j: (i, 0)),
            pl.BlockSpec((tm, RET_V), lambda i, j: (i, COL_RG)),
            pl.BlockSpec((1, GDN_DV), lambda i, j: (0, 0)),
            pl.BlockSpec((1, RET_V), lambda i, j: (0, 0)),
            pl.BlockSpec((GDN_V + RET_V, tn), lambda i, j: (0, j)),
            pl.BlockSpec((tm, tn), lambda i, j: (i, j)),
        ],
        out_specs=pl.BlockSpec((tm, tn), lambda i, j: (i, j)),
        out_shape=jax.ShapeDtypeStruct((t, D_MODEL), f32),
        scratch_shapes=[pltpu.VMEM((tm, GDN_V + RET_V), bf16)],
        compiler_params=_params(("parallel", "arbitrary")),
        name="mix_out",
    )(o_g, proj, o_r, proj, gdn_norm.reshape(1, GDN_DV).astype(f32), ret_norm.reshape(1, RET_V).astype(f32), w_out, x)


def _attn_body(q_ref, k_ref, v_ref, o_ref):
    for h in range(MEM_HEADS):
        sl = slice(h * MEM_HEAD_DIM, (h + 1) * MEM_HEAD_DIM)
        qh = q_ref[:, sl].astype(bf16)
        kh = k_ref[0, :, sl].astype(bf16)
        vh = v_ref[0, :, sl].astype(bf16)
        s = _dot_nt(qh, kh) * (MEM_HEAD_DIM ** -0.5)
        e = jnp.exp(s - jnp.max(s, axis=-1, keepdims=True))
        p = e / jnp.sum(e, axis=-1, keepdims=True)
        o_ref[:, sl] = _dot(p.astype(bf16), vh)


def _attn_cache_body(q_ref, k_ref, v_ref, o_ref):
    tq = q_ref.shape[0]
    rows = N_MEM * MEM_HEADS
    k2 = k_ref[0].reshape(rows, MEM_HEAD_DIM).astype(bf16)
    v2 = v_ref[0].reshape(rows, MEM_HEAD_DIM).astype(bf16)
    qs = jnp.concatenate([q_ref[:, h * MEM_HEAD_DIM:(h + 1) * MEM_HEAD_DIM] for h in range(MEM_HEADS)], axis=0)
    s = _dot_nt(qs.astype(bf16), k2) * (MEM_HEAD_DIM ** -0.5)
    q_head = lax.broadcasted_iota(jnp.int32, s.shape, 0) // tq
    k_head = lax.broadcasted_iota(jnp.int32, s.shape, 1) % MEM_HEADS
    s = jnp.where(q_head == k_head, s, -jnp.inf)
    e = jnp.exp(s - jnp.max(s, axis=-1, keepdims=True))
    p = e / jnp.sum(e, axis=-1, keepdims=True)
    o = _dot(p.astype(bf16), v2)
    for h in range(MEM_HEADS):
        o_ref[:, h * MEM_HEAD_DIM:(h + 1) * MEM_HEAD_DIM] = o[h * tq:(h + 1) * tq]


def _attention(q, mem_k, mem_v, *, l, tq):
    t = q.shape[0]
    nq = l // tq
    width = MEM_HEADS * MEM_HEAD_DIM
    if mem_k.ndim == 4:
        body = _attn_cache_body
        mem_spec = pl.BlockSpec((1, N_MEM, MEM_HEADS, MEM_HEAD_DIM), lambda b, i: (b, 0, 0, 0))
    else:
        body = _attn_body
        mem_spec = pl.BlockSpec((1, N_MEM, width), lambda b, i: (b, 0, 0))
    return pl.pallas_call(
        body,
        grid=(t // l, nq),
        in_specs=[
            pl.BlockSpec((tq, width), lambda b, i: (b * nq + i, 0)),
            mem_spec,
            mem_spec,
        ],
        out_specs=pl.BlockSpec((tq, width), lambda b, i: (b * nq + i, 0)),
        out_shape=jax.ShapeDtypeStruct((t, width), f32),
        compiler_params=_params(("parallel", "arbitrary")),
        name="mem_attn",
    )(q, mem_k, mem_v)


def _ffn_body(x_ref, g_ref, wg_ref, wu_ref, wd_ref, gf_ref, o_ref, xn_ref, acc_ref):
    f = pl.program_id(1)

    @pl.when(f == 0)
    def _():
        xn_ref[...] = _rms(x_ref[...], g_ref[...]).astype(bf16)
        acc_ref[...] = jnp.zeros_like(acc_ref)

    xn = xn_ref[...]
    a = _dot(xn, wg_ref[...])
    u = _dot(xn, wu_ref[...])
    hidden = (jax.nn.silu(a) * u).astype(bf16)
    acc_ref[...] += _dot(hidden, wd_ref[...])

    @pl.when(f == pl.num_programs(1) - 1)
    def _():
        o_ref[...] = _rms(x_ref[...] + acc_ref[...], gf_ref[...])


def _ffn(x, norm_ffn, w_gate, w_up, w_down, norm_final, *, tm=512, tf=512):
    t = x.shape[0]
    tm = min(tm, t)
    assert D_FF % tf == 0
    return pl.pallas_call(
        _ffn_body,
        grid=(t // tm, D_FF // tf),
        in_specs=[
            pl.BlockSpec((tm, D_MODEL), lambda i, f: (i, 0)),
            pl.BlockSpec((1, D_MODEL), lambda i, f: (0, 0)),
            pl.BlockSpec((D_MODEL, tf), lambda i, f: (0, f)),
            pl.BlockSpec((D_MODEL, tf), lambda i, f: (0, f)),
            pl.BlockSpec((tf, D_MODEL), lambda i, f: (f, 0)),
            pl.BlockSpec((1, D_MODEL), lambda i, f: (0, 0)),
        ],
        out_specs=pl.BlockSpec((tm, D_MODEL), lambda i, f: (i, 0)),
        out_shape=jax.ShapeDtypeStruct((t, D_MODEL), f32),
        scratch_shapes=[pltpu.VMEM((tm, D_MODEL), bf16), pltpu.VMEM((tm, D_MODEL), f32)],
        compiler_params=_params(("parallel", "arbitrary")),
        name="ffn",
    )(x, norm_ffn.reshape(1, D_MODEL).astype(f32), w_gate, w_up, w_down, norm_final.reshape(1, D_MODEL).astype(f32))


def _rotary_tables(l, pos0):
    half = RET_DK // 2
    inv = ROPE_BASE ** (-jnp.linspace(0.0, 1.0, half, dtype=f32))
    pos = jnp.arange(l, dtype=f32) + pos0
    ang = pos[:, None] * inv[None, :]
    cos2 = jnp.repeat(jnp.cos(ang), 2, axis=1)
    sign = jnp.tile(jnp.array([-1.0, 1.0], f32), half)
    sin2 = jnp.repeat(jnp.sin(ang), 2, axis=1) * sign[None, :]
    return cos2, sin2


def _layer(x3, pos0, conv_hist, s_gdn, s_ret, mem_k, mem_v, wts, *, pre_nb, pre_lb, gdn_c, gdn_ns, ret_c, attn_tq):
    nseq, l, _ = x3.shape
    t = nseq * l
    x = x3.reshape(t, D_MODEL)
    proj = _norm_matmul(x, wts["norm_mix"], wts["w_in"], name="in_proj", tn=PROJ_WIDTH // 5)
    proj3 = proj.reshape(nseq, l, PROJ_WIDTH)
    hp = jnp.pad(conv_hist.astype(f32), ((0, 0), (SUBLANES - (CONV_WIDTH - 1), 0), (0, 0)))
    q, k, v, gates = _gdn_pre(proj, hp, wts["conv_w_t"], wts["gparams"], nseq=nseq, nb=pre_nb, lb=pre_lb)
    o_g, s_gdn_new = _gdn_chunk(q, k, v, gates, s_gdn.astype(f32), l=l, c=gdn_c, ns=gdn_ns)
    cos2, sin2 = _rotary_tables(l, pos0)
    o_r, s_ret_new = _ret_chunk(proj3, cos2, sin2, s_ret.astype(f32), c=ret_c)
    conv_new = proj3[:, l - (CONV_WIDTH - 1):, :GDN_CONV_DIM]
    x1 = _mix_out(o_g, o_r, proj, wts["gdn_norm"], wts["ret_norm"], wts["w_out"], x)
    qm = _norm_matmul(x1, wts["norm_cross"], wts["w_mem_q"], name="mem_q")
    om = _attention(qm, mem_k, mem_v, l=l, tq=attn_tq)
    x2 = _norm_matmul(om, wts["norm_cross"], wts["w_mem_o"], residual=x1, norm=False, name="mem_o")
    y = _ffn(x2, wts["norm_ffn"], wts["w_gate"], wts["w_up"], wts["w_down"], wts["norm_final"])
    return y.reshape(nseq, l, D_MODEL), conv_new, s_gdn_new, s_ret_new


def _prep_weights(norm_mix, w_in, conv_w, gdn_a_log, gdn_dt_bias, gdn_norm, ret_norm, w_out,
                  norm_cross, w_mem_q, w_mem_o, norm_ffn, w_gate, w_up, w_down, norm_final):
    i0 = GDN_CONV_DIM + GDN_V
    i1 = i0 + 2 * GDN_HEADS
    w_main = jnp.concatenate([w_in[:, :i0], w_in[:, i1:]], axis=1).astype(bf16)
    w_ba = jnp.pad(w_in[:, i0:i1], ((0, 0), (0, LANES - 2 * GDN_HEADS))).astype(bf16)
    w_all = jnp.concatenate([w_main, w_ba], axis=1)
    gparams = jnp.zeros((SUBLANES, LANES), f32)
    gparams = gparams.at[0, GDN_HEADS:2 * GDN_HEADS].set(gdn_a_log.astype(f32))
    gparams = gparams.at[1, GDN_HEADS:2 * GDN_HEADS].set(gdn_dt_bias.astype(f32))
    return {
        "norm_mix": norm_mix, "w_in": w_all,
        "conv_w_t": conv_w.astype(f32).T, "gparams": gparams,
        "gdn_norm": gdn_norm, "ret_norm": ret_norm, "w_out": w_out.astype(bf16),
        "norm_cross": norm_cross, "w_mem_q": w_mem_q.astype(bf16), "w_mem_o": w_mem_o.astype(bf16),
        "norm_ffn": norm_ffn, "w_gate": w_gate.astype(bf16), "w_up": w_up.astype(bf16),
        "w_down": w_down.astype(bf16), "norm_final": norm_final,
    }


def kernel(x_prompt, x_sample, mem_prompt, cache_mem_k, cache_mem_v, state_gdn, state_gdn_conv, state_ret,
           norm_mix, w_in, conv_w, gdn_a_log, gdn_dt_bias, gdn_norm, ret_norm, w_out,
           norm_mem_in, norm_cross, w_mem_q, w_mem_k, w_mem_v, w_mem_o,
           norm_ffn, w_gate, w_up, w_down, norm_final):
    depth = w_in.shape[0]
    assert depth == 1
    batch, seq, _ = x_prompt.shape
    dec_batch, dec_seq, _ = x_sample.shape
    width = MEM_HEADS * MEM_HEAD_DIM
    wts = _prep_weights(norm_mix[0], w_in[0], conv_w[0], gdn_a_log[0], gdn_dt_bias[0], gdn_norm[0], ret_norm[0],
                        w_out[0], norm_cross[0], w_mem_q[0], w_mem_o[0], norm_ffn[0], w_gate[0], w_up[0], w_down[0],
                        norm_final)

    mem2 = mem_prompt.reshape(batch * N_MEM, D_MODEL)
    mk = _norm_matmul(mem2, norm_mem_in[0], w_mem_k[0].astype(bf16), name="mem_k")
    mv = _norm_matmul(mem2, norm_mem_in[0], w_mem_v[0].astype(bf16), name="mem_v")
    mk4 = mk.reshape(batch, N_MEM, MEM_HEADS, MEM_HEAD_DIM)
    mv4 = mv.reshape(batch, N_MEM, MEM_HEADS, MEM_HEAD_DIM)
    y_p, conv_p, sg_p, sr_p = _layer(
        x_prompt, 0.0,
        jnp.zeros((batch, CONV_WIDTH - 1, GDN_CONV_DIM), f32),
        jnp.zeros((batch, GDN_HEADS, GDN_DK, GDN_DV), f32),
        jnp.zeros((batch, RET_HEADS, RET_DK, RET_DV), f32),
        mk.reshape(batch, N_MEM, width), mv.reshape(batch, N_MEM, width), wts,
        pre_nb=1, pre_lb=min(256, seq), gdn_c=min(GDN_CHUNK, seq), gdn_ns=1, ret_c=min(RET_CHUNK, seq),
        attn_tq=min(512, seq))

    y_s, conv_s, sg_s, sr_s = _layer(
        x_sample, float(PAST_LEN), state_gdn_conv[0], state_gdn[0], state_ret[0],
        cache_mem_k[0], cache_mem_v[0], wts,
        pre_nb=min(16, dec_batch), pre_lb=dec_seq, gdn_c=dec_seq, gdn_ns=min(GDN_CHUNK // dec_seq, dec_batch),
        ret_c=dec_seq, attn_tq=dec_seq)

    pdt = x_prompt.dtype
    sdt = x_sample.dtype
    return (y_p.astype(pdt), y_s.astype(sdt),
            sg_p.astype(pdt)[None], conv_p.astype(pdt)[None], sr_p.astype(pdt)[None],
            mk4[None], mv4[None],
            sg_s.astype(sdt)[None], conv_s.astype(sdt)[None], sr_s.astype(sdt)[None])
```

```python
import functools
import math

import jax
import jax.numpy as jnp
from jax import lax
from jax.experimental import pallas as pl
from jax.experimental.pallas import tpu as pltpu

f32 = jnp.float32
bf16 = jnp.bfloat16

D_MODEL = 2048
GDN_HEADS = 8
GDN_DK = 128
GDN_DV = 128
RET_HEADS = 4
RET_DK = 256
RET_DV = 256
CONV_WIDTH = 4
N_MEM = 256
MEM_HEADS = 4
MEM_HEAD_DIM = D_MODEL // MEM_HEADS
D_FF = -(-8 * D_MODEL // (3 * 256)) * 256
EPS = 1e-6
ROPE_BASE = 10000.0
PAST_LEN = 16384

GDN_QK = GDN_HEADS * GDN_DK
GDN_V = GDN_HEADS * GDN_DV
GDN_CONV_DIM = 2 * GDN_QK + GDN_V
RET_QK = RET_HEADS * RET_DK
RET_V = RET_HEADS * RET_DV

LANES = 128
SUBLANES = 8
VMEM_LIMIT = 56 * 1024 * 1024

GDN_CHUNK = 64
RET_CHUNK = 256

COL_Z = GDN_CONV_DIM // GDN_V
COL_RQ = COL_Z + 1
COL_RK = COL_Z + 2
COL_RV = COL_Z + 3
COL_RG = COL_Z + 4
MAIN_WIDTH = GDN_CONV_DIM + GDN_V + 2 * RET_QK + 2 * RET_V
PROJ_WIDTH = MAIN_WIDTH + LANES
COL_BA = MAIN_WIDTH // LANES


def _params(sem):
    return pltpu.CompilerParams(dimension_semantics=sem, vmem_limit_bytes=VMEM_LIMIT)


def _dot(a, b):
    return jnp.dot(a, b, preferred_element_type=f32)


def _dot_nt(a, b):
    return lax.dot_general(a, b, (((1,), (1,)), ((), ())), preferred_element_type=f32)


def _split2(a):
    hi = a.astype(bf16)
    lo = (a - hi.astype(f32)).astype(bf16)
    return hi, lo


def _dot3(a, b):
    ah, al = _split2(a)
    bh, bl = _split2(b)
    return _dot(ah, bh) + (_dot(ah, bl) + _dot(al, bh))


def _dot_exact_lhs(a_bf16, b):
    b1 = b.astype(bf16)
    r1 = b - b1.astype(f32)
    b2 = r1.astype(bf16)
    b3 = (r1 - b2.astype(f32)).astype(bf16)
    return _dot(a_bf16, b1) + (_dot(a_bf16, b2) + _dot(a_bf16, b3))


def _rms(x, g):
    return x * lax.rsqrt(jnp.mean(x * x, axis=-1, keepdims=True) + EPS) * g


def _nm_body(*refs, norm, residual):
    if residual:
        x_ref, g_ref, w_ref, r_ref, o_ref, xn_ref = refs
    else:
        x_ref, g_ref, w_ref, o_ref, xn_ref = refs

    @pl.when(pl.program_id(1) == 0)
    def _():
        x = x_ref[...].astype(f32)
        if norm:
            x = _rms(x, g_ref[...])
        xn_ref[...] = x.astype(bf16)

    acc = _dot(xn_ref[...], w_ref[...])
    if residual:
        acc = r_ref[...] + acc
    o_ref[...] = acc.astype(o_ref.dtype)


def _norm_matmul(x, gain, w, *, name, residual=None, norm=True, tm=1024, tn=1024, out_dtype=f32):
    m, k = x.shape
    n = w.shape[1]
    tm = min(tm, m)
    tn = min(tn, n)
    assert m % tm == 0 and n % tn == 0
    in_specs = [
        pl.BlockSpec((tm, k), lambda i, j: (i, 0)),
        pl.BlockSpec((1, k), lambda i, j: (0, 0)),
        pl.BlockSpec((k, tn), lambda i, j: (0, j)),
    ]
    args = [x, gain.reshape(1, k).astype(f32), w]
    if residual is not None:
        in_specs.append(pl.BlockSpec((tm, tn), lambda i, j: (i, j)))
        args.append(residual)
    return pl.pallas_call(
        functools.partial(_nm_body, norm=norm, residual=residual is not None),
        grid=(m // tm, n // tn),
        in_specs=in_specs,
        out_specs=pl.BlockSpec((tm, tn), lambda i, j: (i, j)),
        out_shape=jax.ShapeDtypeStruct((m, n), out_dtype),
        scratch_shapes=[pltpu.VMEM((tm, k), bf16)],
        compiler_params=_params(("parallel", "arbitrary")),
        name=name,
    )(*args)


def _gdn_pre_body(u_ref, ba_ref, hp_ref, cw_ref, gp_ref, q_ref, k_ref, v_ref, gt_ref, carry_ref, *, nb, lb):
    t = pl.program_id(1)

    @pl.when(t == 0)
    def _():
        carry_ref[...] = hp_ref[...]

    u = u_ref[...]
    c = u.shape[-1]
    ext = jnp.concatenate([carry_ref[...], u], axis=1)
    carry_ref[...] = u[:, lb - SUBLANES:, :]
    ext2 = ext.reshape(nb * (SUBLANES + lb), c)
    cw = cw_ref[...]
    acc = ext2 * cw[CONV_WIDTH - 1:CONV_WIDTH, :]
    for s in range(1, CONV_WIDTH):
        acc = acc + pltpu.roll(ext2, s, axis=0) * cw[CONV_WIDTH - 1 - s:CONV_WIDTH - s, :]
    conv = acc.reshape(nb, SUBLANES + lb, c)[:, SUBLANES:, :].reshape(nb * lb, c)
    act = jax.nn.silu(conv)
    for h in range(GDN_HEADS):
        sl = slice(h * GDN_DK, (h + 1) * GDN_DK)
        qh = act[:, sl]
        kh = act[:, GDN_QK + h * GDN_DK:GDN_QK + (h + 1) * GDN_DK]
        qn = qh * lax.rsqrt(jnp.sum(qh * qh, axis=-1, keepdims=True) + EPS) * (GDN_DK ** -0.5)
        kn = kh * lax.rsqrt(jnp.sum(kh * kh, axis=-1, keepdims=True) + EPS)
        q_ref[:, sl] = qn
        k_ref[:, sl] = kn
    v_ref[...] = act[:, 2 * GDN_QK:]
    ba = ba_ref[...]
    gp = gp_ref[...]
    beta = jax.nn.sigmoid(ba)
    g = -jnp.exp(gp[0:1, :]) * jax.nn.softplus(ba + gp[1:2, :])
    lane = lax.broadcasted_iota(jnp.int32, ba.shape, 1)
    gt_ref[...] = jnp.where(lane < GDN_HEADS, beta, g)


def _gdn_pre(proj, hp, conv_w_t, gparams, *, nseq, nb, lb):
    l = proj.shape[0] // nseq
    proj3 = proj.reshape(nseq, l, PROJ_WIDTH)
    t = nseq * l
    rows = nb * lb
    nt = l // lb
    assert nseq % nb == 0 and l % lb == 0 and (nb == 1 or nt == 1)
    row_map = lambda s, j: (s * nt + j, 0)
    outs = pl.pallas_call(
        functools.partial(_gdn_pre_body, nb=nb, lb=lb),
        grid=(nseq // nb, nt),
        in_specs=[
            pl.BlockSpec((nb, lb, GDN_CONV_DIM), lambda s, j: (s, j, 0)),
            pl.BlockSpec((rows, LANES), lambda s, j: (s * nt + j, COL_BA)),
            pl.BlockSpec((nb, SUBLANES, GDN_CONV_DIM), lambda s, j: (s, 0, 0)),
            pl.BlockSpec((CONV_WIDTH, GDN_CONV_DIM), lambda s, j: (0, 0)),
            pl.BlockSpec((SUBLANES, LANES), lambda s, j: (0, 0)),
        ],
        out_specs=[
            pl.BlockSpec((rows, GDN_QK), row_map),
            pl.BlockSpec((rows, GDN_QK), row_map),
            pl.BlockSpec((rows, GDN_V), row_map),
            pl.BlockSpec((rows, LANES), row_map),
        ],
        out_shape=[
            jax.ShapeDtypeStruct((t, GDN_QK), f32),
            jax.ShapeDtypeStruct((t, GDN_QK), f32),
            jax.ShapeDtypeStruct((t, GDN_V), f32),
            jax.ShapeDtypeStruct((t, LANES), f32),
        ],
        scratch_shapes=[pltpu.VMEM((nb, SUBLANES, GDN_CONV_DIM), f32)],
        compiler_params=_params(("parallel", "arbitrary")),
        name="gdn_pre",
    )(proj3, proj, hp, conv_w_t, gparams)
    return outs


def _gdn_chunk_body(q_ref, k_ref, v_ref, gt_ref, z_ref, gn_ref, s0_ref, o_ref, s_ref, *, c, ns):
    @pl.when(pl.program_id(1) == 0)
    def _():
        s_ref[...] = s0_ref[...]

    r = ns * c
    nfac = int(round(math.log2(c)))
    assert 2 ** nfac == c
    heads = range(GDN_HEADS)
    gates = gt_ref[...]
    row = lax.broadcasted_iota(jnp.int32, (r, r), 0)
    col = lax.broadcasted_iota(jnp.int32, (r, r), 1)
    same = (row // c) == (col // c)
    causal = same & (row >= col)
    strict = same & (row > col)
    eye = (row == col).astype(f32)
    g_cum = _dot_exact_lhs(causal.astype(bf16), gates)
    g_cum_t = g_cum.T
    seq_of_row = lax.broadcasted_iota(jnp.int32, (r, 1), 0) // c
    last = ((row // c) == (col // c)) & ((col % c) == c - 1)
    g_last_rows = _dot_exact_lhs(last.astype(bf16), g_cum)

    q, k, v, beta, gc, eg, decay, qk, p, m = ({} for _ in range(10))
    for h in heads:
        sl = slice(h * GDN_DK, (h + 1) * GDN_DK)
        q[h] = q_ref[:, sl]
        k[h] = k_ref[:, sl]
        v[h] = v_ref[:, sl]
        beta[h] = gates[:, h:h + 1]
        gc[h] = g_cum[:, GDN_HEADS + h:GDN_HEADS + h + 1]
        gr = g_cum_t[GDN_HEADS + h:GDN_HEADS + h + 1, :]
        decay[h] = jnp.exp(jnp.where(causal, gc[h] - gr, -jnp.inf))
        kb = k[h].astype(bf16)
        prod = _dot_nt(jnp.concatenate([q[h], k[h]], axis=0).astype(bf16), kb)
        qk[h] = prod[:r] * decay[h]
        m[h] = jnp.where(strict, -(beta[h] * prod[r:] * decay[h]), 0.0)
        p[h] = eye + m[h]
    for _ in range(nfac - 1):
        for h in heads:
            mb = m[h].astype(bf16)
            m[h] = _dot(mb, mb)
        for h in heads:
            p[h] = p[h] + _dot(p[h].astype(bf16), m[h].astype(bf16))
    u_part, w_part, q_dec = {}, {}, {}
    for h in heads:
        eg[h] = jnp.exp(gc[h])
        rhs = jnp.concatenate([v[h] * beta[h], k[h] * (beta[h] * eg[h])], axis=1)
        sol = _dot(p[h].astype(bf16), rhs.astype(bf16))
        u_part[h] = sol[:, :GDN_DV]
        w_part[h] = sol[:, GDN_DV:]
        q_dec[h] = q[h] * eg[h]
    u, o_state, s_old = {}, {}, {}
    for h in heads:
        us, os_ = [], []
        for j in range(ns):
            rows = slice(j * c, (j + 1) * c)
            s_old[j, h] = s_ref[j, h]
            wq = jnp.concatenate([w_part[h][rows], q_dec[h][rows]], axis=0).astype(bf16)
            wqs = _dot(wq, s_old[j, h].astype(bf16))
            us.append(u_part[h][rows] - wqs[:c])
            os_.append(wqs[c:])
        u[h] = us[0] if ns == 1 else jnp.concatenate(us, axis=0)
        o_state[h] = os_[0] if ns == 1 else jnp.concatenate(os_, axis=0)
    outs, s_new = {}, {}
    for h in heads:
        ub = u[h].astype(bf16)
        outs[h] = o_state[h] + _dot(qk[h].astype(bf16), ub)
        gl = g_last_rows[:, GDN_HEADS + h:GDN_HEADS + h + 1]
        kd = (k[h] * jnp.exp(gl - gc[h])).astype(bf16)
        for j in range(ns):
            uj = ub if ns == 1 else jnp.where(seq_of_row == j, u[h], 0.0).astype(bf16)
            s_new[j, h] = s_old[j, h] * jnp.exp(gl[j * c:j * c + 1, :]) + lax.dot_general(
                kd, uj, (((0,), (0,)), ((), ())), preferred_element_type=f32)
    gn = gn_ref[...]
    for h in heads:
        sl = slice(h * GDN_DV, (h + 1) * GDN_DV)
        o_ref[:, sl] = (_rms(outs[h], gn) * jax.nn.silu(z_ref[:, sl])).astype(o_ref.dtype)
        for j in range(ns):
            s_ref[j, h] = s_new[j, h]


def _gdn_chunk(q, k, v, gates, proj, gdn_norm, s0, *, l, c, ns):
    t = q.shape[0]
    nseq = t // l
    nc = l // c
    r = ns * c
    assert nseq % ns == 0 and (ns == 1 or nc == 1)
    row_map = lambda b, n: (b * nc + n, 0)
    st_spec = pl.BlockSpec((ns, GDN_HEADS, GDN_DK, GDN_DV), lambda b, n: (b, 0, 0, 0))
    return pl.pallas_call(
        functools.partial(_gdn_chunk_body, c=c, ns=ns),
        grid=(nseq // ns, nc),
        in_specs=[
            pl.BlockSpec((r, GDN_QK), row_map),
            pl.BlockSpec((r, GDN_QK), row_map),
            pl.BlockSpec((r, GDN_V), row_map),
            pl.BlockSpec((r, LANES), row_map),
            pl.BlockSpec((r, GDN_V), lambda b, n: (b * nc + n, COL_Z)),
            pl.BlockSpec((1, GDN_DV), lambda b, n: (0, 0)),
            st_spec,
        ],
        out_specs=[pl.BlockSpec((r, GDN_V), row_map), st_spec],
        out_shape=[
            jax.ShapeDtypeStruct((t, GDN_V), bf16),
            jax.ShapeDtypeStruct((nseq, GDN_HEADS, GDN_DK, GDN_DV), f32),
        ],
        compiler_params=_params(("parallel", "arbitrary")),
        name="gdn_chunk",
    )(q, k, v, gates, proj, gdn_norm.reshape(1, GDN_DV).astype(f32), s0)


def _ret_chunk_body(q_ref, k_ref, v_ref, g_ref, cos_ref, sin_ref, rn_ref, s0_ref, o_ref, s_ref, *, c, ns):
    @pl.when(pl.program_id(1) == 0)
    def _():
        s_ref[...] = s0_ref[...]

    half = RET_DK // 2
    cos = cos_ref[...]
    sin = sin_ref[...]
    row = lax.broadcasted_iota(jnp.int32, (c, c), 0)
    col = lax.broadcasted_iota(jnp.int32, (c, c), 1)
    causal = row >= col
    dist = (row - col).astype(f32)
    tpos = lax.broadcasted_iota(jnp.int32, (c, 1), 0).astype(f32)
    src = lax.broadcasted_iota(jnp.int32, (RET_DK, RET_DK), 0)
    dst = lax.broadcasted_iota(jnp.int32, (RET_DK, RET_DK), 1)
    orig = jnp.where(src < half, 2 * src, 2 * (src - half) + 1)
    unperm = (orig == dst).astype(bf16)

    def rot(x):
        xe, xo = x[:, :half], x[:, half:]
        return jnp.concatenate([xe * cos - xo * sin, xo * cos + xe * sin], axis=1)

    heads = range(RET_HEADS)
    lgs = [math.log1p(-(2.0 ** (-5.0 - h))) for h in heads]
    qk, vb, dec = {}, {}, []
    for j in range(ns):
        for h in heads:
            sl = slice(h * RET_DK, (h + 1) * RET_DK)
            q = rot(q_ref[j, :, sl])
            k = rot(k_ref[j, :, sl]) * (RET_DK ** -0.5)
            vb[j, h] = v_ref[j, :, sl].astype(bf16)
            decay = jnp.exp(jnp.where(causal, dist * lgs[h], -jnp.inf))
            qk[j, h] = (_dot_nt(q.astype(bf16), k.astype(bf16)) * decay).astype(bf16)
            dec.append(q * jnp.exp((tpos + 1.0) * lgs[h]))
            dec.append(k * jnp.exp((float(c - 1) - tpos) * lgs[h]))
    dec_orig = _dot(jnp.concatenate(dec, axis=0).astype(bf16), unperm)
    rn = rn_ref[...]
    for j in range(ns):
        outs = []
        for h in heads:
            base = 2 * (j * RET_HEADS + h) * c
            q_dec = dec_orig[base:base + c].astype(bf16)
            k_dec = dec_orig[base + c:base + 2 * c].astype(bf16)
            sl = slice(h * RET_DV, (h + 1) * RET_DV)
            s = s_ref[j, h]
            o = _dot(qk[j, h], vb[j, h]) + _dot(q_dec, s.astype(bf16))
            outs.append(_rms(o, rn[:, sl]) * jax.nn.silu(g_ref[j, :, sl]))
            s_ref[j, h] = s * math.exp(c * lgs[h]) + lax.dot_general(
                k_dec, vb[j, h], (((0,), (0,)), ((), ())), preferred_element_type=f32)
        o_ref[j * c:(j + 1) * c, :] = jnp.concatenate(outs, axis=1).astype(o_ref.dtype)


def _ret_chunk(proj3, cos, sin, ret_norm, s0, *, c, ns):
    nseq, l, _ = proj3.shape
    nc = l // c
    assert nseq % ns == 0 and (ns == 1 or nc == 1)
    st_spec = pl.BlockSpec((ns, RET_HEADS, RET_DK, RET_DV), lambda b, n: (b, 0, 0, 0))
    col_spec = lambda col: pl.BlockSpec((ns, c, RET_QK), lambda b, n: (b, n, col))
    tab_spec = pl.BlockSpec((c, RET_DK // 2), lambda b, n: (n, 0))
    return pl.pallas_call(
        functools.partial(_ret_chunk_body, c=c, ns=ns),
        grid=(nseq // ns, nc),
        in_specs=[col_spec(COL_RQ), col_spec(COL_RK), col_spec(COL_RV), col_spec(COL_RG), tab_spec, tab_spec,
                  pl.BlockSpec((1, RET_V), lambda b, n: (0, 0)), st_spec],
        out_specs=[pl.BlockSpec((ns * c, RET_V), lambda b, n: (b * nc + n, 0)), st_spec],
        out_shape=[
            jax.ShapeDtypeStruct((nseq * l, RET_V), bf16),
            jax.ShapeDtypeStruct((nseq, RET_HEADS, RET_DK, RET_DV), f32),
        ],
        compiler_params=_params(("parallel", "arbitrary")),
        name="ret_chunk",
    )(proj3, proj3, proj3, proj3, cos, sin, ret_norm.reshape(1, RET_V).astype(f32), s0)


def _out_proj_body(mg_ref, mr_ref, wg_ref, wr_ref, x_ref, o_ref):
    o_ref[...] = x_ref[...] + (_dot(mg_ref[...], wg_ref[...]) + _dot(mr_ref[...], wr_ref[...]))


def _out_proj(mix_g, mix_r, w_out, x, *, tm=1024, tn=1024):
    t = x.shape[0]
    tm = min(tm, t)
    assert GDN_V == RET_V and t % tm == 0 and D_MODEL % tn == 0
    return pl.pallas_call(
        _out_proj_body,
        grid=(t // tm, D_MODEL // tn),
        in_specs=[
            pl.BlockSpec((tm, GDN_V), lambda i, j: (i, 0)),
            pl.BlockSpec((tm, RET_V), lambda i, j: (i, 0)),
            pl.BlockSpec((GDN_V, tn), lambda i, j: (0, j)),
            pl.BlockSpec((RET_V, tn), lambda i, j: (1, j)),
            pl.BlockSpec((tm, tn), lambda i, j: (i, j)),
        ],
        out_specs=pl.BlockSpec((tm, tn), lambda i, j: (i, j)),
        out_shape=jax.ShapeDtypeStruct((t, D_MODEL), f32),
        compiler_params=_params(("parallel", "parallel")),
        name="out_proj",
    )(mix_g, mix_r, w_out, w_out, x)


def _attn_body(q_ref, k_ref, v_ref, o_ref):
    for h in range(MEM_HEADS):
        sl = slice(h * MEM_HEAD_DIM, (h + 1) * MEM_HEAD_DIM)
        qh = q_ref[:, sl].astype(bf16)
        kh = k_ref[0, :, sl].astype(bf16)
        vh = v_ref[0, :, sl].astype(bf16)
        s = _dot_nt(qh, kh) * (MEM_HEAD_DIM ** -0.5)
        e = jnp.exp(s - jnp.max(s, axis=-1, keepdims=True))
        p = e / jnp.sum(e, axis=-1, keepdims=True)
        o_ref[:, sl] = _dot(p.astype(bf16), vh).astype(o_ref.dtype)


def _attn_cache_body(q_ref, k_ref, v_ref, o_ref):
    tq = q_ref.shape[0]
    rows = N_MEM * MEM_HEADS
    k2 = k_ref[0].reshape(rows, MEM_HEAD_DIM).astype(bf16)
    v2 = v_ref[0].reshape(rows, MEM_HEAD_DIM).astype(bf16)
    qs = jnp.concatenate([q_ref[:, h * MEM_HEAD_DIM:(h + 1) * MEM_HEAD_DIM] for h in range(MEM_HEADS)], axis=0)
    s = _dot_nt(qs.astype(bf16), k2) * (MEM_HEAD_DIM ** -0.5)
    q_head = lax.broadcasted_iota(jnp.int32, s.shape, 0) // tq
    k_head = lax.broadcasted_iota(jnp.int32, s.shape, 1) % MEM_HEADS
    s = jnp.where(q_head == k_head, s, -jnp.inf)
    e = jnp.exp(s - jnp.max(s, axis=-1, keepdims=True))
    p = e / jnp.sum(e, axis=-1, keepdims=True)
    o = _dot(p.astype(bf16), v2)
    for h in range(MEM_HEADS):
        o_ref[:, h * MEM_HEAD_DIM:(h + 1) * MEM_HEAD_DIM] = o[h * tq:(h + 1) * tq].astype(o_ref.dtype)


def _attention(q, mem_k, mem_v, *, l, tq):
    t = q.shape[0]
    nq = l // tq
    width = MEM_HEADS * MEM_HEAD_DIM
    if mem_k.ndim == 4:
        body = _attn_cache_body
        mem_spec = pl.BlockSpec((1, N_MEM, MEM_HEADS, MEM_HEAD_DIM), lambda b, i: (b, 0, 0, 0))
    else:
        body = _attn_body
        mem_spec = pl.BlockSpec((1, N_MEM, width), lambda b, i: (b, 0, 0))
    return pl.pallas_call(
        body,
        grid=(t // l, nq),
        in_specs=[
            pl.BlockSpec((tq, width), lambda b, i: (b * nq + i, 0)),
            mem_spec,
            mem_spec,
        ],
        out_specs=pl.BlockSpec((tq, width), lambda b, i: (b * nq + i, 0)),
        out_shape=jax.ShapeDtypeStruct((t, width), q.dtype),
        compiler_params=_params(("parallel", "arbitrary")),
        name="mem_attn",
    )(q, mem_k, mem_v)


def _ffn_body(x_ref, g_ref, wg_ref, wu_ref, wd_ref, gf_ref, o_ref, xn_ref, acc_ref):
    f = pl.program_id(1)

    @pl.when(f == 0)
    def _():
        xn_ref[...] = _rms(x_ref[...], g_ref[...]).astype(bf16)
        acc_ref[...] = jnp.zeros_like(acc_ref)

    xn = xn_ref[...]
    a = _dot(xn, wg_ref[...])
    u = _dot(xn, wu_ref[...])
    hidden = (jax.nn.silu(a) * u).astype(bf16)
    acc_ref[...] += _dot(hidden, wd_ref[...])

    @pl.when(f == pl.num_programs(1) - 1)
    def _():
        o_ref[...] = _rms(x_ref[...] + acc_ref[...], gf_ref[...])


def _ffn(x, norm_ffn, w_gate, w_up, w_down, norm_final, *, tm=512, tf=512):
    t = x.shape[0]
    tm = min(tm, t)
    assert D_FF % tf == 0
    return pl.pallas_call(
        _ffn_body,
        grid=(t // tm, D_FF // tf),
        in_specs=[
            pl.BlockSpec((tm, D_MODEL), lambda i, f: (i, 0)),
            pl.BlockSpec((1, D_MODEL), lambda i, f: (0, 0)),
            pl.BlockSpec((D_MODEL, tf), lambda i, f: (0, f)),
            pl.BlockSpec((D_MODEL, tf), lambda i, f: (0, f)),
            pl.BlockSpec((tf, D_MODEL), lambda i, f: (f, 0)),
            pl.BlockSpec((1, D_MODEL), lambda i, f: (0, 0)),
        ],
        out_specs=pl.BlockSpec((tm, D_MODEL), lambda i, f: (i, 0)),
        out_shape=jax.ShapeDtypeStruct((t, D_MODEL), f32),
        scratch_shapes=[pltpu.VMEM((tm, D_MODEL), bf16), pltpu.VMEM((tm, D_MODEL), f32)],
        compiler_params=_params(("parallel", "arbitrary")),
        name="ffn",
    )(x, norm_ffn.reshape(1, D_MODEL).astype(f32), w_gate, w_up, w_down, norm_final.reshape(1, D_MODEL).astype(f32))


def _rotary_tables(l, pos0):
    half = RET_DK // 2
    inv = ROPE_BASE ** (-jnp.linspace(0.0, 1.0, half, dtype=f32))
    pos = jnp.arange(l, dtype=f32) + pos0
    ang = pos[:, None] * inv[None, :]
    return jnp.cos(ang), jnp.sin(ang)


def _layer(x3, pos0, conv_hist, s_gdn, s_ret, mem_k, mem_v, wts, *, pre_nb, pre_lb, gdn_c, gdn_ns, ret_c, ret_ns, attn_tq):
    nseq, l, _ = x3.shape
    t = nseq * l
    x = x3.reshape(t, D_MODEL)
    proj = _norm_matmul(x, wts["norm_mix"], wts["w_in"], name="in_proj", tn=PROJ_WIDTH // 5)
    proj3 = proj.reshape(nseq, l, PROJ_WIDTH)
    hp = jnp.pad(conv_hist.astype(f32), ((0, 0), (SUBLANES - (CONV_WIDTH - 1), 0), (0, 0)))
    q, k, v, gates = _gdn_pre(proj, hp, wts["conv_w_t"], wts["gparams"], nseq=nseq, nb=pre_nb, lb=pre_lb)
    mix_g, s_gdn_new = _gdn_chunk(q, k, v, gates, proj, wts["gdn_norm"], s_gdn.astype(f32), l=l, c=gdn_c, ns=gdn_ns)
    cos, sin = _rotary_tables(l, pos0)
    mix_r, s_ret_new = _ret_chunk(proj3, cos, sin, wts["ret_norm"], s_ret.astype(f32), c=ret_c, ns=ret_ns)
    conv_new = proj3[:, l - (CONV_WIDTH - 1):, :GDN_CONV_DIM]
    x1 = _out_proj(mix_g, mix_r, wts["w_out"], x)
    act_dtype = bf16 if attn_tq % (2 * SUBLANES) == 0 else f32
    qm = _norm_matmul(x1, wts["norm_cross"], wts["w_mem_q"], name="mem_q", out_dtype=act_dtype)
    om = _attention(qm, mem_k, mem_v, l=l, tq=attn_tq)
    x2 = _norm_matmul(om, wts["norm_cross"], wts["w_mem_o"], residual=x1, norm=False, name="mem_o")
    y = _ffn(x2, wts["norm_ffn"], wts["w_gate"], wts["w_up"], wts["w_down"], wts["norm_final"])
    return y.reshape(nseq, l, D_MODEL), conv_new, s_gdn_new, s_ret_new


def _prep_weights(norm_mix, w_in, conv_w, gdn_a_log, gdn_dt_bias, gdn_norm, ret_norm, w_out,
                  norm_cross, w_mem_q, w_mem_o, norm_ffn, w_gate, w_up, w_down, norm_final):
    i0 = GDN_CONV_DIM + GDN_V
    i1 = i0 + 2 * GDN_HEADS
    def deint(w):
        w4 = w.reshape(D_MODEL, RET_HEADS, RET_DK // 2, 2)
        return jnp.swapaxes(w4, 2, 3).reshape(D_MODEL, RET_QK)

    i2, i3 = i1 + RET_QK, i1 + 2 * RET_QK
    w_main = jnp.concatenate([w_in[:, :i0], deint(w_in[:, i1:i2]), deint(w_in[:, i2:i3]), w_in[:, i3:]],
                             axis=1).astype(bf16)
    w_ba = jnp.pad(w_in[:, i0:i1], ((0, 0), (0, LANES - 2 * GDN_HEADS))).astype(bf16)
    w_all = jnp.concatenate([w_main, w_ba], axis=1)
    gparams = jnp.zeros((SUBLANES, LANES), f32)
    gparams = gparams.at[0, GDN_HEADS:2 * GDN_HEADS].set(gdn_a_log.astype(f32))
    gparams = gparams.at[1, GDN_HEADS:2 * GDN_HEADS].set(gdn_dt_bias.astype(f32))
    return {
        "norm_mix": norm_mix, "w_in": w_all,
        "conv_w_t": conv_w.astype(f32).T, "gparams": gparams,
        "gdn_norm": gdn_norm, "ret_norm": ret_norm, "w_out": w_out.astype(bf16),
        "norm_cross": norm_cross, "w_mem_q": w_mem_q.astype(bf16), "w_mem_o": w_mem_o.astype(bf16),
        "norm_ffn": norm_ffn, "w_gate": w_gate.astype(bf16), "w_up": w_up.astype(bf16),
        "w_down": w_down.astype(bf16), "norm_final": norm_final,
    }


def kernel(x_prompt, x_sample, mem_prompt, cache_mem_k, cache_mem_v, state_gdn, state_gdn_conv, state_ret,
           norm_mix, w_in, conv_w, gdn_a_log, gdn_dt_bias, gdn_norm, ret_norm, w_out,
           norm_mem_in, norm_cross, w_mem_q, w_mem_k, w_mem_v, w_mem_o,
           norm_ffn, w_gate, w_up, w_down, norm_final):
    depth = w_in.shape[0]
    assert depth == 1
    batch, seq, _ = x_prompt.shape
    dec_batch, dec_seq, _ = x_sample.shape
    width = MEM_HEADS * MEM_HEAD_DIM
    wts = _prep_weights(norm_mix[0], w_in[0], conv_w[0], gdn_a_log[0], gdn_dt_bias[0], gdn_norm[0], ret_norm[0],
                        w_out[0], norm_cross[0], w_mem_q[0], w_mem_o[0], norm_ffn[0], w_gate[0], w_up[0], w_down[0],
                        norm_final)

    mem2 = mem_prompt.reshape(batch * N_MEM, D_MODEL)
    mk = _norm_matmul(mem2, norm_mem_in[0], w_mem_k[0].astype(bf16), name="mem_k")
    mv = _norm_matmul(mem2, norm_mem_in[0], w_mem_v[0].astype(bf16), name="mem_v")
    mk4 = mk.reshape(batch, N_MEM, MEM_HEADS, MEM_HEAD_DIM)
    mv4 = mv.reshape(batch, N_MEM, MEM_HEADS, MEM_HEAD_DIM)
    y_p, conv_p, sg_p, sr_p = _layer(
        x_prompt, 0.0,
        jnp.zeros((batch, CONV_WIDTH - 1, GDN_CONV_DIM), f32),
        jnp.zeros((batch, GDN_HEADS, GDN_DK, GDN_DV), f32),
        jnp.zeros((batch, RET_HEADS, RET_DK, RET_DV), f32),
        mk.reshape(batch, N_MEM, width), mv.reshape(batch, N_MEM, width), wts,
        pre_nb=1, pre_lb=min(256, seq), gdn_c=min(GDN_CHUNK, seq), gdn_ns=1, ret_c=min(RET_CHUNK, seq), ret_ns=1,
        attn_tq=min(512, seq))

    y_s, conv_s, sg_s, sr_s = _layer(
        x_sample, float(PAST_LEN), state_gdn_conv[0], state_gdn[0], state_ret[0],
        cache_mem_k[0], cache_mem_v[0], wts,
        pre_nb=min(16, dec_batch), pre_lb=dec_seq, gdn_c=dec_seq, gdn_ns=min(GDN_CHUNK // dec_seq, dec_batch),
        ret_c=dec_seq, ret_ns=min(2, dec_batch), attn_tq=dec_seq)

    pdt = x_prompt.dtype
    sdt = x_sample.dtype
    return (y_p.astype(pdt), y_s.astype(sdt),
            sg_p.astype(pdt)[None], conv_p.astype(pdt)[None], sr_p.astype(pdt)[None],
            mk4[None], mv4[None],
            sg_s.astype(sdt)[None], conv_s.astype(sdt)[None], sr_s.astype(sdt)[None])
```

```python
import functools
import math

import jax
import jax.numpy as jnp
from jax import lax
from jax.experimental import pallas as pl
from jax.experimental.pallas import tpu as pltpu

f32 = jnp.float32
bf16 = jnp.bfloat16

D_MODEL = 2048
GDN_HEADS = 8
GDN_DK = 128
GDN_DV = 128
RET_HEADS = 4
RET_DK = 256
RET_DV = 256
CONV_WIDTH = 4
N_MEM = 256
MEM_HEADS = 4
MEM_HEAD_DIM = D_MODEL // MEM_HEADS
D_FF = -(-8 * D_MODEL // (3 * 256)) * 256
EPS = 1e-6
ROPE_BASE = 10000.0
PAST_LEN = 16384

GDN_QK = GDN_HEADS * GDN_DK
GDN_V = GDN_HEADS * GDN_DV
GDN_CONV_DIM = 2 * GDN_QK + GDN_V
RET_QK = RET_HEADS * RET_DK
RET_V = RET_HEADS * RET_DV

LANES = 128
SUBLANES = 8
VMEM_LIMIT = 56 * 1024 * 1024

GDN_CHUNK = 64
RET_CHUNK = 256

COL_Z = GDN_CONV_DIM // GDN_V
COL_RQ = COL_Z + 1
COL_RK = COL_Z + 2
COL_RV = COL_Z + 3
COL_RG = COL_Z + 4
MAIN_WIDTH = GDN_CONV_DIM + GDN_V + 2 * RET_QK + 2 * RET_V


def _params(sem):
    return pltpu.CompilerParams(dimension_semantics=sem, vmem_limit_bytes=VMEM_LIMIT)


def _dot(a, b):
    return jnp.dot(a, b, preferred_element_type=f32)


def _dot_nt(a, b):
    return lax.dot_general(a, b, (((1,), (1,)), ((), ())), preferred_element_type=f32)


def _split2(a):
    hi = a.astype(bf16)
    lo = (a - hi.astype(f32)).astype(bf16)
    return hi, lo


def _dot3(a, b):
    ah, al = _split2(a)
    bh, bl = _split2(b)
    return _dot(ah, bh) + (_dot(ah, bl) + _dot(al, bh))


def _dot_exact_lhs(a_bf16, b):
    b1 = b.astype(bf16)
    r1 = b - b1.astype(f32)
    b2 = r1.astype(bf16)
    b3 = (r1 - b2.astype(f32)).astype(bf16)
    return _dot(a_bf16, b1) + (_dot(a_bf16, b2) + _dot(a_bf16, b3))


def _rms(x, g):
    return x * lax.rsqrt(jnp.mean(x * x, axis=-1, keepdims=True) + EPS) * g


def _nm_body(*refs, norm, residual):
    if residual:
        x_ref, g_ref, w_ref, r_ref, o_ref, xn_ref = refs
    else:
        x_ref, g_ref, w_ref, o_ref, xn_ref = refs

    @pl.when(pl.program_id(1) == 0)
    def _():
        x = x_ref[...].astype(f32)
        if norm:
            x = _rms(x, g_ref[...])
        xn_ref[...] = x.astype(bf16)

    acc = _dot(xn_ref[...], w_ref[...])
    if residual:
        acc = r_ref[...] + acc
    o_ref[...] = acc.astype(o_ref.dtype)


def _norm_matmul(x, gain, w, *, name, residual=None, norm=True, tm=1024, tn=1024, out_dtype=f32):
    m, k = x.shape
    n = w.shape[1]
    tm = min(tm, m)
    tn = min(tn, n)
    assert m % tm == 0 and n % tn == 0
    in_specs = [
        pl.BlockSpec((tm, k), lambda i, j: (i, 0)),
        pl.BlockSpec((1, k), lambda i, j: (0, 0)),
        pl.BlockSpec((k, tn), lambda i, j: (0, j)),
    ]
    args = [x, gain.reshape(1, k).astype(f32), w]
    if residual is not None:
        in_specs.append(pl.BlockSpec((tm, tn), lambda i, j: (i, j)))
        args.append(residual)
    return pl.pallas_call(
        functools.partial(_nm_body, norm=norm, residual=residual is not None),
        grid=(m // tm, n // tn),
        in_specs=in_specs,
        out_specs=pl.BlockSpec((tm, tn), lambda i, j: (i, j)),
        out_shape=jax.ShapeDtypeStruct((m, n), out_dtype),
        scratch_shapes=[pltpu.VMEM((tm, k), bf16)],
        compiler_params=_params(("parallel", "arbitrary")),
        name=name,
    )(*args)


def _in_proj_body(x_ref, g_ref, w_ref, wba_ref, o_ref, ba_ref, xn_ref):
    @pl.when(pl.program_id(1) == 0)
    def _():
        xn = _rms(x_ref[...], g_ref[...]).astype(bf16)
        xn_ref[...] = xn
        ba_ref[...] = _dot(xn, wba_ref[...])

    o_ref[...] = _dot(xn_ref[...], w_ref[...]).astype(o_ref.dtype)


def _in_proj(x, gain, w_main, w_ba, *, tm=1024, tn=2048):
    m, k = x.shape
    tm = min(tm, m)
    assert m % tm == 0 and MAIN_WIDTH % tn == 0
    return pl.pallas_call(
        _in_proj_body,
        grid=(m // tm, MAIN_WIDTH // tn),
        in_specs=[
            pl.BlockSpec((tm, k), lambda i, j: (i, 0)),
            pl.BlockSpec((1, k), lambda i, j: (0, 0)),
            pl.BlockSpec((k, tn), lambda i, j: (0, j)),
            pl.BlockSpec((k, LANES), lambda i, j: (0, 0)),
        ],
        out_specs=[pl.BlockSpec((tm, tn), lambda i, j: (i, j)), pl.BlockSpec((tm, LANES), lambda i, j: (i, 0))],
        out_shape=[jax.ShapeDtypeStruct((m, MAIN_WIDTH), bf16), jax.ShapeDtypeStruct((m, LANES), f32)],
        scratch_shapes=[pltpu.VMEM((tm, k), bf16)],
        compiler_params=_params(("parallel", "arbitrary")),
        name="in_proj",
    )(x, gain.reshape(1, k).astype(f32), w_main, w_ba)


def _gdn_pre_body(u_ref, ba_ref, hp_ref, cw_ref, gp_ref, q_ref, k_ref, v_ref, gt_ref, carry_ref, *, nb, lb):
    t = pl.program_id(1)

    @pl.when(t == 0)
    def _():
        carry_ref[...] = hp_ref[...]

    u = u_ref[...].astype(f32)
    c = u.shape[-1]
    ext = jnp.concatenate([carry_ref[...], u], axis=1)
    carry_ref[...] = u[:, lb - SUBLANES:, :]
    ext2 = ext.reshape(nb * (SUBLANES + lb), c)
    cw = cw_ref[...]
    acc = ext2 * cw[CONV_WIDTH - 1:CONV_WIDTH, :]
    for s in range(1, CONV_WIDTH):
        acc = acc + pltpu.roll(ext2, s, axis=0) * cw[CONV_WIDTH - 1 - s:CONV_WIDTH - s, :]
    conv = acc.reshape(nb, SUBLANES + lb, c)[:, SUBLANES:, :].reshape(nb * lb, c)
    act = jax.nn.silu(conv)
    for h in range(GDN_HEADS):
        sl = slice(h * GDN_DK, (h + 1) * GDN_DK)
        qh = act[:, sl]
        kh = act[:, GDN_QK + h * GDN_DK:GDN_QK + (h + 1) * GDN_DK]
        qn = qh * lax.rsqrt(jnp.sum(qh * qh, axis=-1, keepdims=True) + EPS) * (GDN_DK ** -0.5)
        kn = kh * lax.rsqrt(jnp.sum(kh * kh, axis=-1, keepdims=True) + EPS)
        q_ref[:, sl] = qn.astype(q_ref.dtype)
        k_ref[:, sl] = kn.astype(k_ref.dtype)
    v_ref[...] = act[:, 2 * GDN_QK:].astype(v_ref.dtype)
    ba = ba_ref[...]
    gp = gp_ref[...]
    beta = jax.nn.sigmoid(ba)
    g = -jnp.exp(gp[0:1, :]) * jax.nn.softplus(ba + gp[1:2, :])
    lane = lax.broadcasted_iota(jnp.int32, ba.shape, 1)
    gt_ref[...] = jnp.where(lane < GDN_HEADS, beta, g)


def _gdn_pre(proj, ba, hp, conv_w_t, gparams, *, nseq, nb, lb):
    l = proj.shape[0] // nseq
    proj3 = proj.reshape(nseq, l, MAIN_WIDTH)
    t = nseq * l
    rows = nb * lb
    nt = l // lb
    assert nseq % nb == 0 and l % lb == 0 and (nb == 1 or nt == 1)
    row_map = lambda s, j: (s * nt + j, 0)
    outs = pl.pallas_call(
        functools.partial(_gdn_pre_body, nb=nb, lb=lb),
        grid=(nseq // nb, nt),
        in_specs=[
            pl.BlockSpec((nb, lb, GDN_CONV_DIM), lambda s, j: (s, j, 0)),
            pl.BlockSpec((rows, LANES), row_map),
            pl.BlockSpec((nb, SUBLANES, GDN_CONV_DIM), lambda s, j: (s, 0, 0)),
            pl.BlockSpec((CONV_WIDTH, GDN_CONV_DIM), lambda s, j: (0, 0)),
            pl.BlockSpec((SUBLANES, LANES), lambda s, j: (0, 0)),
        ],
        out_specs=[
            pl.BlockSpec((rows, GDN_QK), row_map),
            pl.BlockSpec((rows, GDN_QK), row_map),
            pl.BlockSpec((rows, GDN_V), row_map),
            pl.BlockSpec((rows, LANES), row_map),
        ],
        out_shape=[
            jax.ShapeDtypeStruct((t, GDN_QK), bf16),
            jax.ShapeDtypeStruct((t, GDN_QK), bf16),
            jax.ShapeDtypeStruct((t, GDN_V), bf16),
            jax.ShapeDtypeStruct((t, LANES), f32),
        ],
        scratch_shapes=[pltpu.VMEM((nb, SUBLANES, GDN_CONV_DIM), f32)],
        compiler_params=_params(("parallel", "arbitrary")),
        name="gdn_pre",
    )(proj3, ba, hp, conv_w_t, gparams)
    return outs


def _gdn_chunk_body(q_ref, k_ref, v_ref, gt_ref, z_ref, gn_ref, s0_ref, o_ref, s_ref, *, c, ns):
    @pl.when(pl.program_id(1) == 0)
    def _():
        s_ref[...] = s0_ref[...]

    r = ns * c
    nfac = int(round(math.log2(c)))
    assert 2 ** nfac == c
    heads = range(GDN_HEADS)
    gates = gt_ref[...].reshape(r, LANES)
    row = lax.broadcasted_iota(jnp.int32, (r, r), 0)
    col = lax.broadcasted_iota(jnp.int32, (r, r), 1)
    same = (row // c) == (col // c)
    causal = same & (row >= col)
    strict = same & (row > col)
    eye = (row == col).astype(f32)
    g_cum = _dot_exact_lhs(causal.astype(bf16), gates)
    g_cum_t = g_cum.T
    seq_of_row = lax.broadcasted_iota(jnp.int32, (r, 1), 0) // c
    last = ((row // c) == (col // c)) & ((col % c) == c - 1)
    g_last_rows = _dot_exact_lhs(last.astype(bf16), g_cum)

    q, k, v, beta, gc, eg, decay, qk, p, m = ({} for _ in range(10))
    for h in heads:
        sl = slice(h * GDN_DK, (h + 1) * GDN_DK)
        q[h] = q_ref[:, :, sl].astype(f32).reshape(r, GDN_DK)
        k[h] = k_ref[:, :, sl].astype(f32).reshape(r, GDN_DK)
        v[h] = v_ref[:, :, sl].astype(f32).reshape(r, GDN_DV)
        beta[h] = gates[:, h:h + 1]
        gc[h] = g_cum[:, GDN_HEADS + h:GDN_HEADS + h + 1]
        gr = g_cum_t[GDN_HEADS + h:GDN_HEADS + h + 1, :]
        decay[h] = jnp.exp(jnp.where(causal, gc[h] - gr, -jnp.inf))
        kb = k[h].astype(bf16)
        prod = _dot_nt(jnp.concatenate([q[h], k[h]], axis=0).astype(bf16), kb)
        qk[h] = prod[:r] * decay[h]
        m[h] = jnp.where(strict, -(beta[h] * prod[r:] * decay[h]), 0.0)
        p[h] = eye + m[h]
    for _ in range(nfac - 1):
        for h in heads:
            mb = m[h].astype(bf16)
            m[h] = _dot(mb, mb)
        for h in heads:
            p[h] = p[h] + _dot(p[h].astype(bf16), m[h].astype(bf16))
    u_part, w_part, q_dec = {}, {}, {}
    for h in heads:
        eg[h] = jnp.exp(gc[h])
        rhs = jnp.concatenate([v[h] * beta[h], k[h] * (beta[h] * eg[h])], axis=1)
        sol = _dot(p[h].astype(bf16), rhs.astype(bf16))
        u_part[h] = sol[:, :GDN_DV]
        w_part[h] = sol[:, GDN_DV:]
        q_dec[h] = q[h] * eg[h]
    u, o_state, s_old = {}, {}, {}
    for h in heads:
        us, os_ = [], []
        for j in range(ns):
            rows = slice(j * c, (j + 1) * c)
            s_old[j, h] = s_ref[j, h]
            wq = jnp.concatenate([w_part[h][rows], q_dec[h][rows]], axis=0).astype(bf16)
            wqs = _dot(wq, s_old[j, h].astype(bf16))
            us.append(u_part[h][rows] - wqs[:c])
            os_.append(wqs[c:])
        u[h] = us[0] if ns == 1 else jnp.concatenate(us, axis=0)
        o_state[h] = os_[0] if ns == 1 else jnp.concatenate(os_, axis=0)
    outs, s_new = {}, {}
    for h in heads:
        ub = u[h].astype(bf16)
        outs[h] = o_state[h] + _dot(qk[h].astype(bf16), ub)
        gl = g_last_rows[:, GDN_HEADS + h:GDN_HEADS + h + 1]
        kd = (k[h] * jnp.exp(gl - gc[h])).astype(bf16)
        for j in range(ns):
            uj = ub if ns == 1 else jnp.where(seq_of_row == j, u[h], 0.0).astype(bf16)
            s_new[j, h] = s_old[j, h] * jnp.exp(gl[j * c:j * c + 1, :]) + lax.dot_general(
                kd, uj, (((0,), (0,)), ((), ())), preferred_element_type=f32)
    gn = gn_ref[...]
    for h in heads:
        sl = slice(h * GDN_DV, (h + 1) * GDN_DV)
        y = _rms(outs[h], gn) * jax.nn.silu(z_ref[:, :, sl].astype(f32).reshape(r, GDN_DV))
        o_ref[:, :, sl] = y.reshape(ns, c, GDN_DV).astype(o_ref.dtype)
        for j in range(ns):
            s_ref[j, h] = s_new[j, h]


def _gdn_chunk(q, k, v, gates, proj, gdn_norm, s0, *, l, c, ns):
    t = q.shape[0]
    nseq = t // l
    nc = l // c
    assert nseq % ns == 0 and l % c == 0
    tok_spec = lambda width, col: pl.BlockSpec((ns, c, width), lambda b, n: (b, n, col))
    st_spec = pl.BlockSpec((ns, GDN_HEADS, GDN_DK, GDN_DV), lambda b, n: (b, 0, 0, 0))
    as3 = lambda a: a.reshape(nseq, l, a.shape[-1])
    out, s_new = pl.pallas_call(
        functools.partial(_gdn_chunk_body, c=c, ns=ns),
        grid=(nseq // ns, nc),
        in_specs=[
            tok_spec(GDN_QK, 0), tok_spec(GDN_QK, 0), tok_spec(GDN_V, 0), tok_spec(LANES, 0), tok_spec(GDN_V, COL_Z),
            pl.BlockSpec((1, GDN_DV), lambda b, n: (0, 0)),
            st_spec,
        ],
        out_specs=[tok_spec(GDN_V, 0), st_spec],
        out_shape=[
            jax.ShapeDtypeStruct((nseq, l, GDN_V), bf16),
            jax.ShapeDtypeStruct((nseq, GDN_HEADS, GDN_DK, GDN_DV), f32),
        ],
        compiler_params=_params(("parallel", "arbitrary")),
        name="gdn_chunk",
    )(as3(q), as3(k), as3(v), as3(gates), as3(proj), gdn_norm.reshape(1, GDN_DV).astype(f32), s0)
    return out.reshape(t, GDN_V), s_new


def _ret_chunk_body(q_ref, k_ref, v_ref, g_ref, cos_ref, sin_ref, rn_ref, s0_ref, o_ref, s_ref, *, c, ns):
    @pl.when(pl.program_id(1) == 0)
    def _():
        s_ref[...] = s0_ref[...]

    half = RET_DK // 2
    cos = cos_ref[...]
    sin = sin_ref[...]
    row = lax.broadcasted_iota(jnp.int32, (c, c), 0)
    col = lax.broadcasted_iota(jnp.int32, (c, c), 1)
    causal = row >= col
    dist = (row - col).astype(f32)
    tpos = lax.broadcasted_iota(jnp.int32, (c, 1), 0).astype(f32)
    src = lax.broadcasted_iota(jnp.int32, (RET_DK, RET_DK), 0)
    dst = lax.broadcasted_iota(jnp.int32, (RET_DK, RET_DK), 1)
    orig = jnp.where(src < half, 2 * src, 2 * (src - half) + 1)
    unperm = (orig == dst).astype(bf16)

    def rot(x):
        xe, xo = x[:, :half], x[:, half:]
        return jnp.concatenate([xe * cos - xo * sin, xo * cos + xe * sin], axis=1)

    heads = range(RET_HEADS)
    lgs = [math.log1p(-(2.0 ** (-5.0 - h))) for h in heads]
    qk, vb, dec = {}, {}, []
    for j in range(ns):
        for h in heads:
            sl = slice(h * RET_DK, (h + 1) * RET_DK)
            q = rot(q_ref[j, :, sl].astype(f32))
            k = rot(k_ref[j, :, sl].astype(f32)) * (RET_DK ** -0.5)
            vb[j, h] = v_ref[j, :, sl].astype(bf16)
            decay = jnp.exp(jnp.where(causal, dist * lgs[h], -jnp.inf))
            qk[j, h] = (_dot_nt(q.astype(bf16), k.astype(bf16)) * decay).astype(bf16)
            dec.append(q * jnp.exp((tpos + 1.0) * lgs[h]))
            dec.append(k * jnp.exp((float(c - 1) - tpos) * lgs[h]))
    dec_orig = _dot(jnp.concatenate(dec, axis=0).astype(bf16), unperm)
    rn = rn_ref[...]
    for j in range(ns):
        outs = []
        for h in heads:
            base = 2 * (j * RET_HEADS + h) * c
            q_dec = dec_orig[base:base + c].astype(bf16)
            k_dec = dec_orig[base + c:base + 2 * c].astype(bf16)
            sl = slice(h * RET_DV, (h + 1) * RET_DV)
            s = s_ref[j, h]
            o = _dot(qk[j, h], vb[j, h]) + _dot(q_dec, s.astype(bf16))
            outs.append(_rms(o, rn[:, sl]) * jax.nn.silu(g_ref[j, :, sl].astype(f32)))
            s_ref[j, h] = s * math.exp(c * lgs[h]) + lax.dot_general(
                k_dec, vb[j, h], (((0,), (0,)), ((), ())), preferred_element_type=f32)
        o_ref[j * c:(j + 1) * c, :] = jnp.concatenate(outs, axis=1).astype(o_ref.dtype)


def _ret_chunk(proj3, cos, sin, ret_norm, s0, *, c, ns):
    nseq, l, _ = proj3.shape
    nc = l // c
    assert nseq % ns == 0 and (ns == 1 or nc == 1)
    st_spec = pl.BlockSpec((ns, RET_HEADS, RET_DK, RET_DV), lambda b, n: (b, 0, 0, 0))
    col_spec = lambda col: pl.BlockSpec((ns, c, RET_QK), lambda b, n: (b, n, col))
    tab_spec = pl.BlockSpec((c, RET_DK // 2), lambda b, n: (n, 0))
    return pl.pallas_call(
        functools.partial(_ret_chunk_body, c=c, ns=ns),
        grid=(nseq // ns, nc),
        in_specs=[col_spec(COL_RQ), col_spec(COL_RK), col_spec(COL_RV), col_spec(COL_RG), tab_spec, tab_spec,
                  pl.BlockSpec((1, RET_V), lambda b, n: (0, 0)), st_spec],
        out_specs=[pl.BlockSpec((ns * c, RET_V), lambda b, n: (b * nc + n, 0)), st_spec],
        out_shape=[
            jax.ShapeDtypeStruct((nseq * l, RET_V), bf16),
            jax.ShapeDtypeStruct((nseq, RET_HEADS, RET_DK, RET_DV), f32),
        ],
        compiler_params=_params(("parallel", "arbitrary")),
        name="ret_chunk",
    )(proj3, proj3, proj3, proj3, cos, sin, ret_norm.reshape(1, RET_V).astype(f32), s0)


def _out_proj_body(mg_ref, mr_ref, wg_ref, wr_ref, x_ref, o_ref):
    o_ref[...] = x_ref[...] + (_dot(mg_ref[...], wg_ref[...]) + _dot(mr_ref[...], wr_ref[...]))


def _out_proj(mix_g, mix_r, w_out, x, *, tm=1024, tn=1024):
    t = x.shape[0]
    tm = min(tm, t)
    assert GDN_V == RET_V and t % tm == 0 and D_MODEL % tn == 0
    return pl.pallas_call(
        _out_proj_body,
        grid=(t // tm, D_MODEL // tn),
        in_specs=[
            pl.BlockSpec((tm, GDN_V), lambda i, j: (i, 0)),
            pl.BlockSpec((tm, RET_V), lambda i, j: (i, 0)),
            pl.BlockSpec((GDN_V, tn), lambda i, j: (0, j)),
            pl.BlockSpec((RET_V, tn), lambda i, j: (1, j)),
            pl.BlockSpec((tm, tn), lambda i, j: (i, j)),
        ],
        out_specs=pl.BlockSpec((tm, tn), lambda i, j: (i, j)),
        out_shape=jax.ShapeDtypeStruct((t, D_MODEL), f32),
        compiler_params=_params(("parallel", "parallel")),
        name="out_proj",
    )(mix_g, mix_r, w_out, w_out, x)


def _attn_body(q_ref, k_ref, v_ref, o_ref):
    for h in range(MEM_HEADS):
        sl = slice(h * MEM_HEAD_DIM, (h + 1) * MEM_HEAD_DIM)
        qh = q_ref[:, sl].astype(bf16)
        kh = k_ref[0, :, sl].astype(bf16)
        vh = v_ref[0, :, sl].astype(bf16)
        s = _dot_nt(qh, kh) * (MEM_HEAD_DIM ** -0.5)
        e = jnp.exp(s - jnp.max(s, axis=-1, keepdims=True))
        p = e / jnp.sum(e, axis=-1, keepdims=True)
        o_ref[:, sl] = _dot(p.astype(bf16), vh).astype(o_ref.dtype)


def _attn_cache_body(q_ref, k_ref, v_ref, o_ref, *, tq, ns):
    rows = N_MEM * MEM_HEADS
    shape = (MEM_HEADS * tq, rows)
    own_head = (lax.broadcasted_iota(jnp.int32, shape, 0) // tq) == (lax.broadcasted_iota(jnp.int32, shape, 1) % MEM_HEADS)
    for j in range(ns):
        rj = slice(j * tq, (j + 1) * tq)
        k2 = k_ref[j].reshape(rows, MEM_HEAD_DIM).astype(bf16)
        v2 = v_ref[j].reshape(rows, MEM_HEAD_DIM).astype(bf16)
        qs = jnp.concatenate([q_ref[rj, h * MEM_HEAD_DIM:(h + 1) * MEM_HEAD_DIM] for h in range(MEM_HEADS)], axis=0)
        s = _dot_nt(qs.astype(bf16), k2) * (MEM_HEAD_DIM ** -0.5)
        s = jnp.where(own_head, s, -jnp.inf)
        e = jnp.exp(s - jnp.max(s, axis=-1, keepdims=True))
        p = e / jnp.sum(e, axis=-1, keepdims=True)
        o = _dot(p.astype(bf16), v2)
        for h in range(MEM_HEADS):
            o_ref[rj, h * MEM_HEAD_DIM:(h + 1) * MEM_HEAD_DIM] = o[h * tq:(h + 1) * tq].astype(o_ref.dtype)


def _attention(q, mem_k, mem_v, *, l, tq, ns=1):
    t = q.shape[0]
    nq = l // tq
    width = MEM_HEADS * MEM_HEAD_DIM
    if mem_k.ndim == 4:
        assert nq == 1 and (t // l) % ns == 0
        body = functools.partial(_attn_cache_body, tq=tq, ns=ns)
        mem_spec = pl.BlockSpec((ns, N_MEM, MEM_HEADS, MEM_HEAD_DIM), lambda b, i: (b, 0, 0, 0))
    else:
        assert ns == 1
        body = _attn_body
        mem_spec = pl.BlockSpec((1, N_MEM, width), lambda b, i: (b, 0, 0))
    return pl.pallas_call(
        body,
        grid=(t // (l * ns), nq),
        in_specs=[
            pl.BlockSpec((ns * tq, width), lambda b, i: (b * nq + i, 0)),
            mem_spec,
            mem_spec,
        ],
        out_specs=pl.BlockSpec((ns * tq, width), lambda b, i: (b * nq + i, 0)),
        out_shape=jax.ShapeDtypeStruct((t, width), q.dtype),
        compiler_params=_params(("parallel", "arbitrary")),
        name="mem_attn",
    )(q, mem_k, mem_v)


def _ffn_body(x_ref, g_ref, wg_ref, wu_ref, wd_ref, gf_ref, o_ref, xn_ref, acc_ref):
    f = pl.program_id(1)

    @pl.when(f == 0)
    def _():
        xn_ref[...] = _rms(x_ref[...], g_ref[...]).astype(bf16)
        acc_ref[...] = jnp.zeros_like(acc_ref)

    xn = xn_ref[...]
    a = _dot(xn, wg_ref[...])
    u = _dot(xn, wu_ref[...])
    hidden = (jax.nn.silu(a) * u).astype(bf16)
    acc_ref[...] += _dot(hidden, wd_ref[...])

    @pl.when(f == pl.num_programs(1) - 1)
    def _():
        o_ref[...] = _rms(x_ref[...] + acc_ref[...], gf_ref[...])


def _ffn(x, norm_ffn, w_gate, w_up, w_down, norm_final, *, tm=512, tf=512):
    t = x.shape[0]
    tm = min(tm, t)
    assert D_FF % tf == 0
    return pl.pallas_call(
        _ffn_body,
        grid=(t // tm, D_FF // tf),
        in_specs=[
            pl.BlockSpec((tm, D_MODEL), lambda i, f: (i, 0)),
            pl.BlockSpec((1, D_MODEL), lambda i, f: (0, 0)),
            pl.BlockSpec((D_MODEL, tf), lambda i, f: (0, f)),
            pl.BlockSpec((D_MODEL, tf), lambda i, f: (0, f)),
            pl.BlockSpec((tf, D_MODEL), lambda i, f: (f, 0)),
            pl.BlockSpec((1, D_MODEL), lambda i, f: (0, 0)),
        ],
        out_specs=pl.BlockSpec((tm, D_MODEL), lambda i, f: (i, 0)),
        out_shape=jax.ShapeDtypeStruct((t, D_MODEL), f32),
        scratch_shapes=[pltpu.VMEM((tm, D_MODEL), bf16), pltpu.VMEM((tm, D_MODEL), f32)],
        compiler_params=_params(("parallel", "arbitrary")),
        name="ffn",
    )(x, norm_ffn.reshape(1, D_MODEL).astype(f32), w_gate, w_up, w_down, norm_final.reshape(1, D_MODEL).astype(f32))


def _rotary_tables(l, pos0):
    half = RET_DK // 2
    inv = ROPE_BASE ** (-jnp.linspace(0.0, 1.0, half, dtype=f32))
    pos = jnp.arange(l, dtype=f32) + pos0
    ang = pos[:, None] * inv[None, :]
    return jnp.cos(ang), jnp.sin(ang)


def _layer(x3, pos0, conv_hist, s_gdn, s_ret, mem_k, mem_v, wts, *, pre_nb, pre_lb, gdn_c, gdn_ns, ret_c, ret_ns, attn_tq, attn_ns):
    nseq, l, _ = x3.shape
    t = nseq * l
    x = x3.reshape(t, D_MODEL)
    proj, ba = _in_proj(x, wts["norm_mix"], wts["w_in"], wts["w_in_ba"])
    proj3 = proj.reshape(nseq, l, MAIN_WIDTH)
    hp = jnp.pad(conv_hist.astype(f32), ((0, 0), (SUBLANES - (CONV_WIDTH - 1), 0), (0, 0)))
    q, k, v, gates = _gdn_pre(proj, ba, hp, wts["conv_w_t"], wts["gparams"], nseq=nseq, nb=pre_nb, lb=pre_lb)
    mix_g, s_gdn_new = _gdn_chunk(q, k, v, gates, proj, wts["gdn_norm"], s_gdn.astype(f32), l=l, c=gdn_c, ns=gdn_ns)
    cos, sin = _rotary_tables(l, pos0)
    mix_r, s_ret_new = _ret_chunk(proj3, cos, sin, wts["ret_norm"], s_ret.astype(f32), c=ret_c, ns=ret_ns)
    conv_new = proj3[:, l - (CONV_WIDTH - 1):, :GDN_CONV_DIM].astype(f32)
    x1 = _out_proj(mix_g, mix_r, wts["w_out"], x)
    act_dtype = bf16 if attn_tq % (2 * SUBLANES) == 0 else f32
    qm = _norm_matmul(x1, wts["norm_cross"], wts["w_mem_q"], name="mem_q", out_dtype=act_dtype)
    om = _attention(qm, mem_k, mem_v, l=l, tq=attn_tq, ns=attn_ns)
    x2 = _norm_matmul(om, wts["norm_cross"], wts["w_mem_o"], residual=x1, norm=False, name="mem_o")
    y = _ffn(x2, wts["norm_ffn"], wts["w_gate"], wts["w_up"], wts["w_down"], wts["norm_final"])
    return y.reshape(nseq, l, D_MODEL), conv_new, s_gdn_new, s_ret_new


def _prep_weights(norm_mix, w_in, conv_w, gdn_a_log, gdn_dt_bias, gdn_norm, ret_norm, w_out,
                  norm_cross, w_mem_q, w_mem_o, norm_ffn, w_gate, w_up, w_down, norm_final):
    i0 = GDN_CONV_DIM + GDN_V
    i1 = i0 + 2 * GDN_HEADS
    def deint(w):
        w4 = w.reshape(D_MODEL, RET_HEADS, RET_DK // 2, 2)
        return jnp.swapaxes(w4, 2, 3).reshape(D_MODEL, RET_QK)

    i2, i3 = i1 + RET_QK, i1 + 2 * RET_QK
    w_main = jnp.concatenate([w_in[:, :i0], deint(w_in[:, i1:i2]), deint(w_in[:, i2:i3]), w_in[:, i3:]],
                             axis=1).astype(bf16)
    w_ba = jnp.pad(w_in[:, i0:i1], ((0, 0), (0, LANES - 2 * GDN_HEADS))).astype(bf16)
    gparams = jnp.zeros((SUBLANES, LANES), f32)
    gparams = gparams.at[0, GDN_HEADS:2 * GDN_HEADS].set(gdn_a_log.astype(f32))
    gparams = gparams.at[1, GDN_HEADS:2 * GDN_HEADS].set(gdn_dt_bias.astype(f32))
    return {
        "norm_mix": norm_mix, "w_in": w_main, "w_in_ba": w_ba,
        "conv_w_t": conv_w.astype(f32).T, "gparams": gparams,
        "gdn_norm": gdn_norm, "ret_norm": ret_norm, "w_out": w_out.astype(bf16),
        "norm_cross": norm_cross, "w_mem_q": w_mem_q.astype(bf16), "w_mem_o": w_mem_o.astype(bf16),
        "norm_ffn": norm_ffn, "w_gate": w_gate.astype(bf16), "w_up": w_up.astype(bf16),
        "w_down": w_down.astype(bf16), "norm_final": norm_final,
    }


def kernel(x_prompt, x_sample, mem_prompt, cache_mem_k, cache_mem_v, state_gdn, state_gdn_conv, state_ret,
           norm_mix, w_in, conv_w, gdn_a_log, gdn_dt_bias, gdn_norm, ret_norm, w_out,
           norm_mem_in, norm_cross, w_mem_q, w_mem_k, w_mem_v, w_mem_o,
           norm_ffn, w_gate, w_up, w_down, norm_final):
    depth = w_in.shape[0]
    assert depth == 1
    batch, seq, _ = x_prompt.shape
    dec_batch, dec_seq, _ = x_sample.shape
    width = MEM_HEADS * MEM_HEAD_DIM
    wts = _prep_weights(norm_mix[0], w_in[0], conv_w[0], gdn_a_log[0], gdn_dt_bias[0], gdn_norm[0], ret_norm[0],
                        w_out[0], norm_cross[0], w_mem_q[0], w_mem_o[0], norm_ffn[0], w_gate[0], w_up[0], w_down[0],
                        norm_final)

    mem2 = mem_prompt.reshape(batch * N_MEM, D_MODEL)
    mk = _norm_matmul(mem2, norm_mem_in[0], w_mem_k[0].astype(bf16), name="mem_k")
    mv = _norm_matmul(mem2, norm_mem_in[0], w_mem_v[0].astype(bf16), name="mem_v")
    mk4 = mk.reshape(batch, N_MEM, MEM_HEADS, MEM_HEAD_DIM)
    mv4 = mv.reshape(batch, N_MEM, MEM_HEADS, MEM_HEAD_DIM)
    y_p, conv_p, sg_p, sr_p = _layer(
        x_prompt, 0.0,
        jnp.zeros((batch, CONV_WIDTH - 1, GDN_CONV_DIM), f32),
        jnp.zeros((batch, GDN_HEADS, GDN_DK, GDN_DV), f32),
        jnp.zeros((batch, RET_HEADS, RET_DK, RET_DV), f32),
        mk.reshape(batch, N_MEM, width), mv.reshape(batch, N_MEM, width), wts,
        pre_nb=1, pre_lb=min(256, seq), gdn_c=min(GDN_CHUNK, seq), gdn_ns=min(2, batch), ret_c=min(RET_CHUNK, seq), ret_ns=1,
        attn_tq=min(512, seq), attn_ns=1)

    y_s, conv_s, sg_s, sr_s = _layer(
        x_sample, float(PAST_LEN), state_gdn_conv[0], state_gdn[0], state_ret[0],
        cache_mem_k[0], cache_mem_v[0], wts,
        pre_nb=min(16, dec_batch), pre_lb=dec_seq, gdn_c=dec_seq, gdn_ns=min(GDN_CHUNK // dec_seq, dec_batch),
        ret_c=dec_seq, ret_ns=min(2, dec_batch), attn_tq=dec_seq, attn_ns=min(2, dec_batch))

    pdt = x_prompt.dtype
    sdt = x_sample.dtype
    return (y_p.astype(pdt), y_s.astype(sdt),
            sg_p.astype(pdt)[None], conv_p.astype(pdt)[None], sr_p.astype(pdt)[None],
            mk4[None], mv4[None],
            sg_s.astype(sdt)[None], conv_s.astype(sdt)[None], sr_s.astype(sdt)[None])
```

```python
import functools
import math

import jax
import jax.numpy as jnp
from jax import lax
from jax.experimental import pallas as pl
from jax.experimental.pallas import tpu as pltpu

f32 = jnp.float32
bf16 = jnp.bfloat16

D_MODEL = 2048
GDN_HEADS = 8
GDN_DK = 128
GDN_DV = 128
RET_HEADS = 4
RET_DK = 256
RET_DV = 256
CONV_WIDTH = 4
N_MEM = 256
MEM_HEADS = 4
MEM_HEAD_DIM = D_MODEL // MEM_HEADS
D_FF = -(-8 * D_MODEL // (3 * 256)) * 256
EPS = 1e-6
ROPE_BASE = 10000.0
PAST_LEN = 16384

GDN_QK = GDN_HEADS * GDN_DK
GDN_V = GDN_HEADS * GDN_DV
GDN_CONV_DIM = 2 * GDN_QK + GDN_V
RET_QK = RET_HEADS * RET_DK
RET_V = RET_HEADS * RET_DV

LANES = 128
SUBLANES = 8
VMEM_LIMIT = 56 * 1024 * 1024

GDN_CHUNK = 64
RET_CHUNK = 256

COL_Z = GDN_CONV_DIM // GDN_V
COL_RQ = COL_Z + 1
COL_RK = COL_Z + 2
COL_RV = COL_Z + 3
COL_RG = COL_Z + 4
MAIN_WIDTH = GDN_CONV_DIM + GDN_V + 2 * RET_QK + 2 * RET_V


def _params(sem):
    return pltpu.CompilerParams(dimension_semantics=sem, vmem_limit_bytes=VMEM_LIMIT)


def _dot(a, b):
    return jnp.dot(a, b, preferred_element_type=f32)


def _dot_nt(a, b):
    return lax.dot_general(a, b, (((1,), (1,)), ((), ())), preferred_element_type=f32)


def _split2(a):
    hi = a.astype(bf16)
    lo = (a - hi.astype(f32)).astype(bf16)
    return hi, lo


def _dot3(a, b):
    ah, al = _split2(a)
    bh, bl = _split2(b)
    return _dot(ah, bh) + (_dot(ah, bl) + _dot(al, bh))


def _dot_exact_lhs(a_bf16, b):
    b1 = b.astype(bf16)
    r1 = b - b1.astype(f32)
    b2 = r1.astype(bf16)
    b3 = (r1 - b2.astype(f32)).astype(bf16)
    return _dot(a_bf16, b1) + (_dot(a_bf16, b2) + _dot(a_bf16, b3))


def _rms(x, g):
    return x * lax.rsqrt(jnp.mean(x * x, axis=-1, keepdims=True) + EPS) * g


def _nm_body(*refs, norm, residual):
    if residual:
        x_ref, g_ref, w_ref, r_ref, o_ref, xn_ref = refs
    else:
        x_ref, g_ref, w_ref, o_ref, xn_ref = refs

    @pl.when(pl.program_id(1) == 0)
    def _():
        x = x_ref[...].astype(f32)
        if norm:
            x = _rms(x, g_ref[...])
        xn_ref[...] = x.astype(bf16)

    acc = _dot(xn_ref[...], w_ref[...].astype(bf16))
    if residual:
        acc = r_ref[...] + acc
    o_ref[...] = acc.astype(o_ref.dtype)


def _norm_matmul(x, gain, w, *, name, residual=None, norm=True, tm=1024, tn=1024, out_dtype=f32):
    m, k = x.shape
    n = w.shape[1]
    tm = min(tm, m)
    tn = min(tn, n)
    assert m % tm == 0 and n % tn == 0
    in_specs = [
        pl.BlockSpec((tm, k), lambda i, j: (i, 0)),
        pl.BlockSpec((1, k), lambda i, j: (0, 0)),
        pl.BlockSpec((k, tn), lambda i, j: (0, j)),
    ]
    args = [x, gain.reshape(1, k).astype(f32), w]
    if residual is not None:
        in_specs.append(pl.BlockSpec((tm, tn), lambda i, j: (i, j)))
        args.append(residual)
    return pl.pallas_call(
        functools.partial(_nm_body, norm=norm, residual=residual is not None),
        grid=(m // tm, n // tn),
        in_specs=in_specs,
        out_specs=pl.BlockSpec((tm, tn), lambda i, j: (i, j)),
        out_shape=jax.ShapeDtypeStruct((m, n), out_dtype),
        scratch_shapes=[pltpu.VMEM((tm, k), bf16)],
        compiler_params=_params(("parallel", "arbitrary")),
        name=name,
    )(*args)


def _in_proj_body(x_ref, g_ref, w_ref, wba_ref, o_ref, ba_ref, xn_ref):
    @pl.when(pl.program_id(1) == 0)
    def _():
        xn = _rms(x_ref[...], g_ref[...]).astype(bf16)
        xn_ref[...] = xn
        ba_ref[...] = _dot(xn, wba_ref[...])

    o_ref[...] = _dot(xn_ref[...], w_ref[...]).astype(o_ref.dtype)


def _in_proj(x, gain, w_main, w_ba, *, out_dtype, tm=1024, tn=2048):
    m, k = x.shape
    tm = min(tm, m)
    tn = tn * 2 // jnp.dtype(out_dtype).itemsize
    assert m % tm == 0 and MAIN_WIDTH % tn == 0
    return pl.pallas_call(
        _in_proj_body,
        grid=(m // tm, MAIN_WIDTH // tn),
        in_specs=[
            pl.BlockSpec((tm, k), lambda i, j: (i, 0)),
            pl.BlockSpec((1, k), lambda i, j: (0, 0)),
            pl.BlockSpec((k, tn), lambda i, j: (0, j)),
            pl.BlockSpec((k, LANES), lambda i, j: (0, 0)),
        ],
        out_specs=[pl.BlockSpec((tm, tn), lambda i, j: (i, j)), pl.BlockSpec((tm, LANES), lambda i, j: (i, 0))],
        out_shape=[jax.ShapeDtypeStruct((m, MAIN_WIDTH), out_dtype), jax.ShapeDtypeStruct((m, LANES), f32)],
        scratch_shapes=[pltpu.VMEM((tm, k), bf16)],
        compiler_params=_params(("parallel", "arbitrary")),
        name="in_proj",
    )(x, gain.reshape(1, k).astype(f32), w_main, w_ba)


def _gdn_pre_body(u_ref, ba_ref, hp_ref, cw_ref, gp_ref, q_ref, k_ref, v_ref, gt_ref, carry_ref, *, nb, lb):
    t = pl.program_id(1)

    @pl.when(t == 0)
    def _():
        carry_ref[...] = hp_ref[...]

    u = u_ref[...].astype(f32)
    c = u.shape[-1]
    ext = jnp.concatenate([carry_ref[...], u], axis=1)
    carry_ref[...] = u[:, lb - SUBLANES:, :]
    ext2 = ext.reshape(nb * (SUBLANES + lb), c)
    cw = cw_ref[...]
    acc = ext2 * cw[CONV_WIDTH - 1:CONV_WIDTH, :]
    for s in range(1, CONV_WIDTH):
        acc = acc + pltpu.roll(ext2, s, axis=0) * cw[CONV_WIDTH - 1 - s:CONV_WIDTH - s, :]
    conv = acc.reshape(nb, SUBLANES + lb, c)[:, SUBLANES:, :].reshape(nb * lb, c)
    act = jax.nn.silu(conv)
    for h in range(GDN_HEADS):
        sl = slice(h * GDN_DK, (h + 1) * GDN_DK)
        qh = act[:, sl]
        kh = act[:, GDN_QK + h * GDN_DK:GDN_QK + (h + 1) * GDN_DK]
        qn = qh * lax.rsqrt(jnp.sum(qh * qh, axis=-1, keepdims=True) + EPS) * (GDN_DK ** -0.5)
        kn = kh * lax.rsqrt(jnp.sum(kh * kh, axis=-1, keepdims=True) + EPS)
        q_ref[:, sl] = qn.astype(q_ref.dtype)
        k_ref[:, sl] = kn.astype(k_ref.dtype)
    v_ref[...] = act[:, 2 * GDN_QK:].astype(v_ref.dtype)
    ba = ba_ref[...]
    gp = gp_ref[...]
    beta = jax.nn.sigmoid(ba)
    g = -jnp.exp(gp[0:1, :]) * jax.nn.softplus(ba + gp[1:2, :])
    lane = lax.broadcasted_iota(jnp.int32, ba.shape, 1)
    gt_ref[...] = jnp.where(lane < GDN_HEADS, beta, g)


def _gdn_pre(proj, ba, hp, conv_w_t, gparams, *, nseq, nb, lb):
    l = proj.shape[0] // nseq
    proj3 = proj.reshape(nseq, l, MAIN_WIDTH)
    t = nseq * l
    rows = nb * lb
    nt = l // lb
    assert nseq % nb == 0 and l % lb == 0 and (nb == 1 or nt == 1)
    row_map = lambda s, j: (s * nt + j, 0)
    outs = pl.pallas_call(
        functools.partial(_gdn_pre_body, nb=nb, lb=lb),
        grid=(nseq // nb, nt),
        in_specs=[
            pl.BlockSpec((nb, lb, GDN_CONV_DIM), lambda s, j: (s, j, 0)),
            pl.BlockSpec((rows, LANES), row_map),
            pl.BlockSpec((nb, SUBLANES, GDN_CONV_DIM), lambda s, j: (s, 0, 0)),
            pl.BlockSpec((CONV_WIDTH, GDN_CONV_DIM), lambda s, j: (0, 0)),
            pl.BlockSpec((SUBLANES, LANES), lambda s, j: (0, 0)),
        ],
        out_specs=[
            pl.BlockSpec((rows, GDN_QK), row_map),
            pl.BlockSpec((rows, GDN_QK), row_map),
            pl.BlockSpec((rows, GDN_V), row_map),
            pl.BlockSpec((rows, LANES), row_map),
        ],
        out_shape=[
            jax.ShapeDtypeStruct((t, GDN_QK), proj.dtype),
            jax.ShapeDtypeStruct((t, GDN_QK), proj.dtype),
            jax.ShapeDtypeStruct((t, GDN_V), proj.dtype),
            jax.ShapeDtypeStruct((t, LANES), f32),
        ],
        scratch_shapes=[pltpu.VMEM((nb, SUBLANES, GDN_CONV_DIM), f32)],
        compiler_params=_params(("parallel", "arbitrary")),
        name="gdn_pre",
    )(proj3, ba, hp, conv_w_t, gparams)
    return outs


def _gdn_chunk_body(q_ref, k_ref, v_ref, gt_ref, z_ref, gn_ref, s0_ref, o_ref, s_ref, *, c, ns):
    @pl.when(pl.program_id(1) == 0)
    def _():
        s_ref[...] = s0_ref[...]

    r = ns * c
    nfac = int(round(math.log2(c)))
    assert 2 ** nfac == c
    heads = range(GDN_HEADS)
    gates = gt_ref[...].reshape(r, LANES)
    row = lax.broadcasted_iota(jnp.int32, (r, r), 0)
    col = lax.broadcasted_iota(jnp.int32, (r, r), 1)
    same = (row // c) == (col // c)
    causal = same & (row >= col)
    strict = same & (row > col)
    eye = (row == col).astype(f32)
    g_cum = _dot_exact_lhs(causal.astype(bf16), gates)
    g_cum_t = g_cum.T
    seq_of_row = lax.broadcasted_iota(jnp.int32, (r, 1), 0) // c
    last = ((row // c) == (col // c)) & ((col % c) == c - 1)
    g_last_rows = _dot_exact_lhs(last.astype(bf16), g_cum)

    q, k, v, beta, gc, eg, decay, qk, p, m = ({} for _ in range(10))
    for h in heads:
        sl = slice(h * GDN_DK, (h + 1) * GDN_DK)
        q[h] = q_ref[:, :, sl].astype(f32).reshape(r, GDN_DK)
        k[h] = k_ref[:, :, sl].astype(f32).reshape(r, GDN_DK)
        v[h] = v_ref[:, :, sl].astype(f32).reshape(r, GDN_DV)
        beta[h] = gates[:, h:h + 1]
        gc[h] = g_cum[:, GDN_HEADS + h:GDN_HEADS + h + 1]
        gr = g_cum_t[GDN_HEADS + h:GDN_HEADS + h + 1, :]
        decay[h] = jnp.exp(jnp.where(causal, gc[h] - gr, -jnp.inf))
        kb = k[h].astype(bf16)
        prod = _dot_nt(jnp.concatenate([q[h], k[h]], axis=0).astype(bf16), kb)
        qk[h] = prod[:r] * decay[h]
        m[h] = jnp.where(strict, -(beta[h] * prod[r:] * decay[h]), 0.0)
        p[h] = eye + m[h]
    for _ in range(nfac - 1):
        for h in heads:
            mb = m[h].astype(bf16)
            m[h] = _dot(mb, mb)
        for h in heads:
            p[h] = p[h] + _dot(p[h].astype(bf16), m[h].astype(bf16))
    u_part, w_part, q_dec = {}, {}, {}
    for h in heads:
        eg[h] = jnp.exp(gc[h])
        rhs = jnp.concatenate([v[h] * beta[h], k[h] * (beta[h] * eg[h])], axis=1)
        sol = _dot(p[h].astype(bf16), rhs.astype(bf16))
        u_part[h] = sol[:, :GDN_DV]
        w_part[h] = sol[:, GDN_DV:]
        q_dec[h] = q[h] * eg[h]
    u, o_state, s_old = {}, {}, {}
    for h in heads:
        us, os_ = [], []
        for j in range(ns):
            rows = slice(j * c, (j + 1) * c)
            s_old[j, h] = s_ref[j, h]
            wq = jnp.concatenate([w_part[h][rows], q_dec[h][rows]], axis=0).astype(bf16)
            wqs = _dot(wq, s_old[j, h].astype(bf16))
            us.append(u_part[h][rows] - wqs[:c])
            os_.append(wqs[c:])
        u[h] = us[0] if ns == 1 else jnp.concatenate(us, axis=0)
        o_state[h] = os_[0] if ns == 1 else jnp.concatenate(os_, axis=0)
    outs, s_new = {}, {}
    for h in heads:
        ub = u[h].astype(bf16)
        outs[h] = o_state[h] + _dot(qk[h].astype(bf16), ub)
        gl = g_last_rows[:, GDN_HEADS + h:GDN_HEADS + h + 1]
        kd = (k[h] * jnp.exp(gl - gc[h])).astype(bf16)
        for j in range(ns):
            uj = ub if ns == 1 else jnp.where(seq_of_row == j, u[h], 0.0).astype(bf16)
            s_new[j, h] = s_old[j, h] * jnp.exp(gl[j * c:j * c + 1, :]) + lax.dot_general(
                kd, uj, (((0,), (0,)), ((), ())), preferred_element_type=f32)
    gn = gn_ref[...]
    for h in heads:
        sl = slice(h * GDN_DV, (h + 1) * GDN_DV)
        y = _rms(outs[h], gn) * jax.nn.silu(z_ref[:, :, sl].astype(f32).reshape(r, GDN_DV))
        o_ref[:, :, sl] = y.reshape(ns, c, GDN_DV).astype(o_ref.dtype)
        for j in range(ns):
            s_ref[j, h] = s_new[j, h]


def _gdn_chunk(q, k, v, gates, proj, gdn_norm, s0, *, l, c, ns):
    t = q.shape[0]
    nseq = t // l
    nc = l // c
    assert nseq % ns == 0 and l % c == 0
    tok_spec = lambda width, col: pl.BlockSpec((ns, c, width), lambda b, n: (b, n, col))
    st_spec = pl.BlockSpec((ns, GDN_HEADS, GDN_DK, GDN_DV), lambda b, n: (b, 0, 0, 0))
    as3 = lambda a: a.reshape(nseq, l, a.shape[-1])
    out, s_new = pl.pallas_call(
        functools.partial(_gdn_chunk_body, c=c, ns=ns),
        grid=(nseq // ns, nc),
        in_specs=[
            tok_spec(GDN_QK, 0), tok_spec(GDN_QK, 0), tok_spec(GDN_V, 0), tok_spec(LANES, 0), tok_spec(GDN_V, COL_Z),
            pl.BlockSpec((1, GDN_DV), lambda b, n: (0, 0)),
            st_spec,
        ],
        out_specs=[tok_spec(GDN_V, 0), st_spec],
        out_shape=[
            jax.ShapeDtypeStruct((nseq, l, GDN_V), proj.dtype),
            jax.ShapeDtypeStruct((nseq, GDN_HEADS, GDN_DK, GDN_DV), f32),
        ],
        compiler_params=_params(("parallel", "arbitrary")),
        name="gdn_chunk",
    )(as3(q), as3(k), as3(v), as3(gates), as3(proj), gdn_norm.reshape(1, GDN_DV).astype(f32), s0)
    return out.reshape(t, GDN_V), s_new


def _ret_chunk_body(q_ref, k_ref, v_ref, g_ref, cos_ref, sin_ref, rn_ref, s0_ref, o_ref, s_ref, *, c, ns):
    @pl.when(pl.program_id(1) == 0)
    def _():
        s_ref[...] = s0_ref[...]

    half = RET_DK // 2
    cos = cos_ref[...]
    sin = sin_ref[...]
    row = lax.broadcasted_iota(jnp.int32, (c, c), 0)
    col = lax.broadcasted_iota(jnp.int32, (c, c), 1)
    causal = row >= col
    dist = (row - col).astype(f32)
    tpos = lax.broadcasted_iota(jnp.int32, (c, 1), 0).astype(f32)
    src = lax.broadcasted_iota(jnp.int32, (RET_DK, RET_DK), 0)
    dst = lax.broadcasted_iota(jnp.int32, (RET_DK, RET_DK), 1)
    orig = jnp.where(src < half, 2 * src, 2 * (src - half) + 1)
    unperm = (orig == dst).astype(bf16)

    def rot(x):
        xe, xo = x[:, :half], x[:, half:]
        return jnp.concatenate([xe * cos - xo * sin, xo * cos + xe * sin], axis=1)

    heads = range(RET_HEADS)
    lgs = [math.log1p(-(2.0 ** (-5.0 - h))) for h in heads]
    qk, vb, dec = {}, {}, []
    for j in range(ns):
        for h in heads:
            sl = slice(h * RET_DK, (h + 1) * RET_DK)
            q = rot(q_ref[j, :, sl].astype(f32))
            k = rot(k_ref[j, :, sl].astype(f32)) * (RET_DK ** -0.5)
            vb[j, h] = v_ref[j, :, sl].astype(bf16)
            decay = jnp.exp(jnp.where(causal, dist * lgs[h], -jnp.inf))
            qk[j, h] = (_dot_nt(q.astype(bf16), k.astype(bf16)) * decay).astype(bf16)
            dec.append(q * jnp.exp((tpos + 1.0) * lgs[h]))
            dec.append(k * jnp.exp((float(c - 1) - tpos) * lgs[h]))
    dec_orig = _dot(jnp.concatenate(dec, axis=0).astype(bf16), unperm)
    rn = rn_ref[...]
    for j in range(ns):
        outs = []
        for h in heads:
            base = 2 * (j * RET_HEADS + h) * c
            q_dec = dec_orig[base:base + c].astype(bf16)
            k_dec = dec_orig[base + c:base + 2 * c].astype(bf16)
            sl = slice(h * RET_DV, (h + 1) * RET_DV)
            s = s_ref[j, h]
            o = _dot(qk[j, h], vb[j, h]) + _dot(q_dec, s.astype(bf16))
            outs.append(_rms(o, rn[:, sl]) * jax.nn.silu(g_ref[j, :, sl].astype(f32)))
            s_ref[j, h] = s * math.exp(c * lgs[h]) + lax.dot_general(
                k_dec, vb[j, h], (((0,), (0,)), ((), ())), preferred_element_type=f32)
        o_ref[j * c:(j + 1) * c, :] = jnp.concatenate(outs, axis=1).astype(o_ref.dtype)


def _ret_chunk(proj3, cos, sin, ret_norm, s0, *, c, ns):
    nseq, l, _ = proj3.shape
    nc = l // c
    assert nseq % ns == 0 and (ns == 1 or nc == 1)
    st_spec = pl.BlockSpec((ns, RET_HEADS, RET_DK, RET_DV), lambda b, n: (b, 0, 0, 0))
    col_spec = lambda col: pl.BlockSpec((ns, c, RET_QK), lambda b, n: (b, n, col))
    tab_spec = pl.BlockSpec((c, RET_DK // 2), lambda b, n: (n, 0))
    return pl.pallas_call(
        functools.partial(_ret_chunk_body, c=c, ns=ns),
        grid=(nseq // ns, nc),
        in_specs=[col_spec(COL_RQ), col_spec(COL_RK), col_spec(COL_RV), col_spec(COL_RG), tab_spec, tab_spec,
                  pl.BlockSpec((1, RET_V), lambda b, n: (0, 0)), st_spec],
        out_specs=[pl.BlockSpec((ns * c, RET_V), lambda b, n: (b * nc + n, 0)), st_spec],
        out_shape=[
            jax.ShapeDtypeStruct((nseq * l, RET_V), proj3.dtype),
            jax.ShapeDtypeStruct((nseq, RET_HEADS, RET_DK, RET_DV), f32),
        ],
        compiler_params=_params(("parallel", "arbitrary")),
        name="ret_chunk",
    )(proj3, proj3, proj3, proj3, cos, sin, ret_norm.reshape(1, RET_V).astype(f32), s0)


def _out_proj_body(mg_ref, mr_ref, wg_ref, wr_ref, x_ref, o_ref):
    o_ref[...] = x_ref[...] + (_dot(mg_ref[...].astype(bf16), wg_ref[...].astype(bf16))
                               + _dot(mr_ref[...].astype(bf16), wr_ref[...].astype(bf16)))


def _out_proj(mix_g, mix_r, w_out, x, *, tm=1024, tn=1024):
    t = x.shape[0]
    tm = min(tm, t)
    assert GDN_V == RET_V and t % tm == 0 and D_MODEL % tn == 0
    return pl.pallas_call(
        _out_proj_body,
        grid=(t // tm, D_MODEL // tn),
        in_specs=[
            pl.BlockSpec((tm, GDN_V), lambda i, j: (i, 0)),
            pl.BlockSpec((tm, RET_V), lambda i, j: (i, 0)),
            pl.BlockSpec((GDN_V, tn), lambda i, j: (0, j)),
            pl.BlockSpec((RET_V, tn), lambda i, j: (1, j)),
            pl.BlockSpec((tm, tn), lambda i, j: (i, j)),
        ],
        out_specs=pl.BlockSpec((tm, tn), lambda i, j: (i, j)),
        out_shape=jax.ShapeDtypeStruct((t, D_MODEL), f32),
        compiler_params=_params(("parallel", "parallel")),
        name="out_proj",
    )(mix_g, mix_r, w_out, w_out, x)


def _attn_body(q_ref, k_ref, v_ref, o_ref):
    for h in range(MEM_HEADS):
        sl = slice(h * MEM_HEAD_DIM, (h + 1) * MEM_HEAD_DIM)
        qh = q_ref[:, sl].astype(bf16)
        kh = k_ref[0, :, sl].astype(bf16)
        vh = v_ref[0, :, sl].astype(bf16)
        s = _dot_nt(qh, kh) * (MEM_HEAD_DIM ** -0.5)
        e = jnp.exp(s - jnp.max(s, axis=-1, keepdims=True))
        p = e / jnp.sum(e, axis=-1, keepdims=True)
        o_ref[:, sl] = _dot(p.astype(bf16), vh).astype(o_ref.dtype)


def _attn_cache_body(q_ref, k_ref, v_ref, o_ref, *, tq, ns):
    rows = N_MEM * MEM_HEADS
    shape = (MEM_HEADS * tq, rows)
    own_head = (lax.broadcasted_iota(jnp.int32, shape, 0) // tq) == (lax.broadcasted_iota(jnp.int32, shape, 1) % MEM_HEADS)
    for j in range(ns):
        rj = slice(j * tq, (j + 1) * tq)
        k2 = k_ref[j].reshape(rows, MEM_HEAD_DIM).astype(bf16)
        v2 = v_ref[j].reshape(rows, MEM_HEAD_DIM).astype(bf16)
        qs = jnp.concatenate([q_ref[rj, h * MEM_HEAD_DIM:(h + 1) * MEM_HEAD_DIM] for h in range(MEM_HEADS)], axis=0)
        s = _dot_nt(qs.astype(bf16), k2) * (MEM_HEAD_DIM ** -0.5)
        s = jnp.where(own_head, s, -jnp.inf)
        e = jnp.exp(s - jnp.max(s, axis=-1, keepdims=True))
        p = e / jnp.sum(e, axis=-1, keepdims=True)
        o = _dot(p.astype(bf16), v2)
        for h in range(MEM_HEADS):
            o_ref[rj, h * MEM_HEAD_DIM:(h + 1) * MEM_HEAD_DIM] = o[h * tq:(h + 1) * tq].astype(o_ref.dtype)


def _attention(q, mem_k, mem_v, *, l, tq, ns=1):
    t = q.shape[0]
    nq = l // tq
    width = MEM_HEADS * MEM_HEAD_DIM
    if mem_k.ndim == 4:
        assert nq == 1 and (t // l) % ns == 0
        body = functools.partial(_attn_cache_body, tq=tq, ns=ns)
        mem_spec = pl.BlockSpec((ns, N_MEM, MEM_HEADS, MEM_HEAD_DIM), lambda b, i: (b, 0, 0, 0))
    else:
        assert ns == 1
        body = _attn_body
        mem_spec = pl.BlockSpec((1, N_MEM, width), lambda b, i: (b, 0, 0))
    return pl.pallas_call(
        body,
        grid=(t // (l * ns), nq),
        in_specs=[
            pl.BlockSpec((ns * tq, width), lambda b, i: (b * nq + i, 0)),
            mem_spec,
            mem_spec,
        ],
        out_specs=pl.BlockSpec((ns * tq, width), lambda b, i: (b * nq + i, 0)),
        out_shape=jax.ShapeDtypeStruct((t, width), q.dtype),
        compiler_params=_params(("parallel", "arbitrary")),
        name="mem_attn",
    )(q, mem_k, mem_v)


def _ffn_body(x_ref, g_ref, wg_ref, wu_ref, wd_ref, gf_ref, o_ref, xn_ref):
    f = pl.program_id(1)

    @pl.when(f == 0)
    def _():
        x = x_ref[...]
        xn_ref[...] = _rms(x, g_ref[...]).astype(bf16)
        o_ref[...] = x

    xn = xn_ref[...]
    a = _dot(xn, wg_ref[...].astype(bf16))
    u = _dot(xn, wu_ref[...].astype(bf16))
    hidden = (jax.nn.silu(a) * u).astype(bf16)
    o_ref[...] += _dot(hidden, wd_ref[...].astype(bf16))

    @pl.when(f == pl.num_programs(1) - 1)
    def _():
        o_ref[...] = _rms(o_ref[...], gf_ref[...])


def _ffn(x, norm_ffn, w_gate, w_up, w_down, norm_final, *, tm=1024, tf=256):
    t = x.shape[0]
    tm = min(tm, t)
    assert D_FF % tf == 0 and t % tm == 0
    return pl.pallas_call(
        _ffn_body,
        grid=(t // tm, D_FF // tf),
        in_specs=[
            pl.BlockSpec((tm, D_MODEL), lambda i, f: (i, 0)),
            pl.BlockSpec((1, D_MODEL), lambda i, f: (0, 0)),
            pl.BlockSpec((D_MODEL, tf), lambda i, f: (0, f)),
            pl.BlockSpec((D_MODEL, tf), lambda i, f: (0, f)),
            pl.BlockSpec((tf, D_MODEL), lambda i, f: (f, 0)),
            pl.BlockSpec((1, D_MODEL), lambda i, f: (0, 0)),
        ],
        out_specs=pl.BlockSpec((tm, D_MODEL), lambda i, f: (i, 0)),
        out_shape=jax.ShapeDtypeStruct((t, D_MODEL), f32),
        scratch_shapes=[pltpu.VMEM((tm, D_MODEL), bf16)],
        compiler_params=_params(("parallel", "arbitrary")),
        name="ffn",
    )(x, norm_ffn.reshape(1, D_MODEL).astype(f32), w_gate, w_up, w_down, norm_final.reshape(1, D_MODEL).astype(f32))


def _rotary_tables(l, pos0):
    half = RET_DK // 2
    inv = ROPE_BASE ** (-jnp.linspace(0.0, 1.0, half, dtype=f32))
    pos = jnp.arange(l, dtype=f32) + pos0
    ang = pos[:, None] * inv[None, :]
    return jnp.cos(ang), jnp.sin(ang)


def _layer(x3, pos0, conv_hist, s_gdn, s_ret, mem_k, mem_v, wts, *, pre_nb, pre_lb, gdn_c, gdn_ns, ret_c, ret_ns, attn_tq, attn_ns):
    nseq, l, _ = x3.shape
    t = nseq * l
    x = x3.reshape(t, D_MODEL)
    proj_dtype = bf16 if l % (2 * SUBLANES) == 0 else f32
    proj, ba = _in_proj(x, wts["norm_mix"], wts["w_in"], wts["w_in_ba"], out_dtype=proj_dtype)
    proj3 = proj.reshape(nseq, l, MAIN_WIDTH)
    hp = jnp.pad(conv_hist.astype(f32), ((0, 0), (SUBLANES - (CONV_WIDTH - 1), 0), (0, 0)))
    q, k, v, gates = _gdn_pre(proj, ba, hp, wts["conv_w_t"], wts["gparams"], nseq=nseq, nb=pre_nb, lb=pre_lb)
    mix_g, s_gdn_new = _gdn_chunk(q, k, v, gates, proj, wts["gdn_norm"], s_gdn.astype(f32), l=l, c=gdn_c, ns=gdn_ns)
    cos, sin = _rotary_tables(l, pos0)
    mix_r, s_ret_new = _ret_chunk(proj3, cos, sin, wts["ret_norm"], s_ret.astype(f32), c=ret_c, ns=ret_ns)
    conv_new = proj3[:, l - (CONV_WIDTH - 1):, :GDN_CONV_DIM].astype(f32)
    x1 = _out_proj(mix_g, mix_r, wts["w_out"], x)
    act_dtype = bf16 if attn_tq % (2 * SUBLANES) == 0 else f32
    qm = _norm_matmul(x1, wts["norm_cross"], wts["w_mem_q"], name="mem_q", out_dtype=act_dtype)
    om = _attention(qm, mem_k, mem_v, l=l, tq=attn_tq, ns=attn_ns)
    x2 = _norm_matmul(om, wts["norm_cross"], wts["w_mem_o"], residual=x1, norm=False, name="mem_o",
                      tn=1024 if om.dtype == bf16 else 512)
    y = _ffn(x2, wts["norm_ffn"], wts["w_gate"], wts["w_up"], wts["w_down"], wts["norm_final"])
    return y.reshape(nseq, l, D_MODEL), conv_new, s_gdn_new, s_ret_new


def _w_in_prep_body(w_ref, o_ref, ba_ref):
    i0 = GDN_CONV_DIM + GDN_V
    i1 = i0 + 2 * GDN_HEADS
    half = RET_DK // 2
    src = lax.broadcasted_iota(jnp.int32, (RET_DK, RET_DK), 0)
    dst = lax.broadcasted_iota(jnp.int32, (RET_DK, RET_DK), 1)
    perm = (src == jnp.where(dst < half, 2 * dst, 2 * (dst - half) + 1)).astype(bf16)
    o_ref[:, :i0] = w_ref[:, :i0].astype(bf16)
    for h in range(2 * RET_HEADS):
        wh = w_ref[:, i1 + h * RET_DK:i1 + (h + 1) * RET_DK].astype(bf16)
        o_ref[:, i0 + h * RET_DK:i0 + (h + 1) * RET_DK] = _dot(wh, perm).astype(bf16)
    rest = i1 + 2 * RET_QK
    o_ref[:, i0 + 2 * RET_QK:] = w_ref[:, rest:].astype(bf16)
    ba = w_ref[:, i0:i1]
    ba_ref[...] = jnp.concatenate([ba, jnp.zeros((ba.shape[0], LANES - 2 * GDN_HEADS), f32)], axis=1).astype(bf16)


def _w_in_prep(w_in, *, tr=256):
    k, n = w_in.shape
    assert k % tr == 0
    return pl.pallas_call(
        _w_in_prep_body,
        grid=(k // tr,),
        in_specs=[pl.BlockSpec((tr, n), lambda i: (i, 0))],
        out_specs=[pl.BlockSpec((tr, MAIN_WIDTH), lambda i: (i, 0)), pl.BlockSpec((tr, LANES), lambda i: (i, 0))],
        out_shape=[jax.ShapeDtypeStruct((k, MAIN_WIDTH), bf16), jax.ShapeDtypeStruct((k, LANES), bf16)],
        compiler_params=_params(("parallel",)),
        name="w_in_prep",
    )(w_in)


def _prep_weights(norm_mix, w_in, conv_w, gdn_a_log, gdn_dt_bias, gdn_norm, ret_norm, w_out,
                  norm_cross, w_mem_q, w_mem_o, norm_ffn, w_gate, w_up, w_down, norm_final):
    w_main, w_ba = _w_in_prep(w_in.astype(f32))
    gparams = jnp.zeros((SUBLANES, LANES), f32)
    gparams = gparams.at[0, GDN_HEADS:2 * GDN_HEADS].set(gdn_a_log.astype(f32))
    gparams = gparams.at[1, GDN_HEADS:2 * GDN_HEADS].set(gdn_dt_bias.astype(f32))
    return {
        "norm_mix": norm_mix, "w_in": w_main, "w_in_ba": w_ba,
        "conv_w_t": conv_w.astype(f32).T, "gparams": gparams,
        "gdn_norm": gdn_norm, "ret_norm": ret_norm, "w_out": w_out,
        "norm_cross": norm_cross, "w_mem_q": w_mem_q, "w_mem_o": w_mem_o,
        "norm_ffn": norm_ffn, "w_gate": w_gate, "w_up": w_up, "w_down": w_down, "norm_final": norm_final,
    }


def kernel(x_prompt, x_sample, mem_prompt, cache_mem_k, cache_mem_v, state_gdn, state_gdn_conv, state_ret,
           norm_mix, w_in, conv_w, gdn_a_log, gdn_dt_bias, gdn_norm, ret_norm, w_out,
           norm_mem_in, norm_cross, w_mem_q, w_mem_k, w_mem_v, w_mem_o,
           norm_ffn, w_gate, w_up, w_down, norm_final):
    depth = w_in.shape[0]
    assert depth == 1
    batch, seq, _ = x_prompt.shape
    dec_batch, dec_seq, _ = x_sample.shape
    width = MEM_HEADS * MEM_HEAD_DIM
    wts = _prep_weights(norm_mix[0], w_in[0], conv_w[0], gdn_a_log[0], gdn_dt_bias[0], gdn_norm[0], ret_norm[0],
                        w_out[0], norm_cross[0], w_mem_q[0], w_mem_o[0], norm_ffn[0], w_gate[0], w_up[0], w_down[0],
                        norm_final)

    mem2 = mem_prompt.reshape(batch * N_MEM, D_MODEL)
    mk = _norm_matmul(mem2, norm_mem_in[0], w_mem_k[0], name="mem_k")
    mv = _norm_matmul(mem2, norm_mem_in[0], w_mem_v[0], name="mem_v")
    mk4 = mk.reshape(batch, N_MEM, MEM_HEADS, MEM_HEAD_DIM)
    mv4 = mv.reshape(batch, N_MEM, MEM_HEADS, MEM_HEAD_DIM)
    y_p, conv_p, sg_p, sr_p = _layer(
        x_prompt, 0.0,
        jnp.zeros((batch, CONV_WIDTH - 1, GDN_CONV_DIM), f32),
        jnp.zeros((batch, GDN_HEADS, GDN_DK, GDN_DV), f32),
        jnp.zeros((batch, RET_HEADS, RET_DK, RET_DV), f32),
        mk.reshape(batch, N_MEM, width), mv.reshape(batch, N_MEM, width), wts,
        pre_nb=1, pre_lb=min(256, seq), gdn_c=min(GDN_CHUNK, seq), gdn_ns=min(2, batch), ret_c=min(RET_CHUNK, seq), ret_ns=1,
        attn_tq=min(512, seq), attn_ns=1)

    y_s, conv_s, sg_s, sr_s = _layer(
        x_sample, float(PAST_LEN), state_gdn_conv[0], state_gdn[0], state_ret[0],
        cache_mem_k[0], cache_mem_v[0], wts,
        pre_nb=min(16, dec_batch), pre_lb=dec_seq, gdn_c=dec_seq, gdn_ns=min(GDN_CHUNK // dec_seq, dec_batch),
        ret_c=dec_seq, ret_ns=1, attn_tq=dec_seq, attn_ns=min(2, dec_batch))

    pdt = x_prompt.dtype
    sdt = x_sample.dtype
    return (y_p.astype(pdt), y_s.astype(sdt),
            sg_p.astype(pdt)[None], conv_p.astype(pdt)[None], sr_p.astype(pdt)[None],
            mk4[None], mv4[None],
            sg_s.astype(sdt)[None], conv_s.astype(sdt)[None], sr_s.astype(sdt)[None])
```

```python
import functools
import math

import jax
import jax.numpy as jnp
from jax import lax
from jax.experimental import pallas as pl
from jax.experimental.pallas import tpu as pltpu

f32 = jnp.float32
bf16 = jnp.bfloat16

D_MODEL = 2048
GDN_HEADS = 8
GDN_DK = 128
GDN_DV = 128
RET_HEADS = 4
RET_DK = 256
RET_DV = 256
CONV_WIDTH = 4
N_MEM = 256
MEM_HEADS = 4
MEM_HEAD_DIM = D_MODEL // MEM_HEADS
D_FF = -(-8 * D_MODEL // (3 * 256)) * 256
EPS = 1e-6
ROPE_BASE = 10000.0
PAST_LEN = 16384

GDN_QK = GDN_HEADS * GDN_DK
GDN_V = GDN_HEADS * GDN_DV
GDN_CONV_DIM = 2 * GDN_QK + GDN_V
RET_QK = RET_HEADS * RET_DK
RET_V = RET_HEADS * RET_DV

LANES = 128
SUBLANES = 8
VMEM_LIMIT = 56 * 1024 * 1024

GDN_CHUNK = 64
RET_CHUNK = 256

COL_Z = GDN_CONV_DIM // GDN_V
COL_RQ = COL_Z + 1
COL_RK = COL_Z + 2
COL_RV = COL_Z + 3
COL_RG = COL_Z + 4
MAIN_WIDTH = GDN_CONV_DIM + GDN_V + 2 * RET_QK + 2 * RET_V


def _params(sem):
    return pltpu.CompilerParams(dimension_semantics=sem, vmem_limit_bytes=VMEM_LIMIT)


def _dot(a, b):
    return jnp.dot(a, b, preferred_element_type=f32)


def _dot_nt(a, b):
    return lax.dot_general(a, b, (((1,), (1,)), ((), ())), preferred_element_type=f32)


def _split2(a):
    hi = a.astype(bf16)
    lo = (a - hi.astype(f32)).astype(bf16)
    return hi, lo


def _dot3(a, b):
    ah, al = _split2(a)
    bh, bl = _split2(b)
    return _dot(ah, bh) + (_dot(ah, bl) + _dot(al, bh))


def _dot_exact_lhs(a_bf16, b):
    b1 = b.astype(bf16)
    r1 = b - b1.astype(f32)
    b2 = r1.astype(bf16)
    b3 = (r1 - b2.astype(f32)).astype(bf16)
    return _dot(a_bf16, b1) + (_dot(a_bf16, b2) + _dot(a_bf16, b3))


def _rms(x, g):
    return x * lax.rsqrt(jnp.mean(x * x, axis=-1, keepdims=True) + EPS) * g


def _nm_body(*refs, norm, residual):
    if residual:
        x_ref, g_ref, w_ref, r_ref, o_ref, xn_ref = refs
    else:
        x_ref, g_ref, w_ref, o_ref, xn_ref = refs

    @pl.when(pl.program_id(1) == 0)
    def _():
        x = x_ref[...].astype(f32)
        if norm:
            x = _rms(x, g_ref[...])
        xn_ref[...] = x.astype(bf16)

    acc = _dot(xn_ref[...], w_ref[...].astype(bf16))
    if residual:
        acc = r_ref[...] + acc
    o_ref[...] = acc.astype(o_ref.dtype)


def _norm_matmul(x, gain, w, *, name, residual=None, norm=True, tm=1024, tn=1024, out_dtype=f32):
    m, k = x.shape
    n = w.shape[1]
    tm = min(tm, m)
    tn = min(tn, n)
    assert m % tm == 0 and n % tn == 0
    in_specs = [
        pl.BlockSpec((tm, k), lambda i, j: (i, 0)),
        pl.BlockSpec((1, k), lambda i, j: (0, 0)),
        pl.BlockSpec((k, tn), lambda i, j: (0, j)),
    ]
    args = [x, gain.reshape(1, k).astype(f32), w]
    if residual is not None:
        in_specs.append(pl.BlockSpec((tm, tn), lambda i, j: (i, j)))
        args.append(residual)
    return pl.pallas_call(
        functools.partial(_nm_body, norm=norm, residual=residual is not None),
        grid=(m // tm, n // tn),
        in_specs=in_specs,
        out_specs=pl.BlockSpec((tm, tn), lambda i, j: (i, j)),
        out_shape=jax.ShapeDtypeStruct((m, n), out_dtype),
        scratch_shapes=[pltpu.VMEM((tm, k), bf16)],
        compiler_params=_params(("parallel", "arbitrary")),
        name=name,
    )(*args)


def _in_proj_body(x_ref, g_ref, w_ref, wba_ref, o_ref, ba_ref, xn_ref):
    @pl.when(pl.program_id(1) == 0)
    def _():
        xn = _rms(x_ref[...], g_ref[...]).astype(bf16)
        xn_ref[...] = xn
        ba_ref[...] = _dot(xn, wba_ref[...])

    o_ref[...] = _dot(xn_ref[...], w_ref[...]).astype(o_ref.dtype)


def _in_proj(x, gain, w_main, w_ba, *, out_dtype, tm=1024, tn=2048):
    m, k = x.shape
    tm = min(tm, m)
    tn = tn * 2 // jnp.dtype(out_dtype).itemsize
    assert m % tm == 0 and MAIN_WIDTH % tn == 0
    return pl.pallas_call(
        _in_proj_body,
        grid=(m // tm, MAIN_WIDTH // tn),
        in_specs=[
            pl.BlockSpec((tm, k), lambda i, j: (i, 0)),
            pl.BlockSpec((1, k), lambda i, j: (0, 0)),
            pl.BlockSpec((k, tn), lambda i, j: (0, j)),
            pl.BlockSpec((k, LANES), lambda i, j: (0, 0)),
        ],
        out_specs=[pl.BlockSpec((tm, tn), lambda i, j: (i, j)), pl.BlockSpec((tm, LANES), lambda i, j: (i, 0))],
        out_shape=[jax.ShapeDtypeStruct((m, MAIN_WIDTH), out_dtype), jax.ShapeDtypeStruct((m, LANES), f32)],
        scratch_shapes=[pltpu.VMEM((tm, k), bf16)],
        compiler_params=_params(("parallel", "arbitrary")),
        name="in_proj",
    )(x, gain.reshape(1, k).astype(f32), w_main, w_ba)


def _gdn_chunk_body(u_ref, ba_ref, hp_ref, cw_ref, gp_ref, z_ref, gn_ref, s0_ref, o_ref, s_ref, carry_ref, *, c, ns):
    @pl.when(pl.program_id(1) == 0)
    def _():
        s_ref[...] = s0_ref[...]
        carry_ref[...] = hp_ref[...]

    r = ns * c
    nfac = int(round(math.log2(c)))
    assert 2 ** nfac == c
    heads = range(GDN_HEADS)
    u = u_ref[...].astype(f32)
    cdim = u.shape[-1]
    ext = jnp.concatenate([carry_ref[...], u], axis=1)
    carry_ref[...] = u[:, c - SUBLANES:, :]
    ext2 = ext.reshape(ns * (SUBLANES + c), cdim)
    cw = cw_ref[...]
    acc = ext2 * cw[CONV_WIDTH - 1:CONV_WIDTH, :]
    for sft in range(1, CONV_WIDTH):
        acc = acc + pltpu.roll(ext2, sft, axis=0) * cw[CONV_WIDTH - 1 - sft:CONV_WIDTH - sft, :]
    act = jax.nn.silu(acc.reshape(ns, SUBLANES + c, cdim)[:, SUBLANES:, :].reshape(r, cdim))
    ba = ba_ref[...].reshape(r, LANES)
    gp = gp_ref[...]
    log_decay = -jnp.exp(gp[0:1, :]) * jax.nn.softplus(ba + gp[1:2, :])
    lane = lax.broadcasted_iota(jnp.int32, ba.shape, 1)
    gates = jnp.where(lane < GDN_HEADS, jax.nn.sigmoid(ba), log_decay)
    row = lax.broadcasted_iota(jnp.int32, (r, r), 0)
    col = lax.broadcasted_iota(jnp.int32, (r, r), 1)
    same = (row // c) == (col // c)
    causal = same & (row >= col)
    strict = same & (row > col)
    eye = (row == col).astype(f32)
    g_cum = _dot_exact_lhs(causal.astype(bf16), gates)
    g_cum_t = g_cum.T
    seq_of_row = lax.broadcasted_iota(jnp.int32, (r, 1), 0) // c
    last = ((row // c) == (col // c)) & ((col % c) == c - 1)
    g_last_rows = _dot_exact_lhs(last.astype(bf16), g_cum)

    q, k, v, beta, gc, eg, decay, qk, p, m = ({} for _ in range(10))
    for h in heads:
        sl = slice(h * GDN_DK, (h + 1) * GDN_DK)
        qh = act[:, sl]
        kh = act[:, GDN_QK + h * GDN_DK:GDN_QK + (h + 1) * GDN_DK]
        q[h] = qh * lax.rsqrt(jnp.sum(qh * qh, axis=-1, keepdims=True) + EPS) * (GDN_DK ** -0.5)
        k[h] = kh * lax.rsqrt(jnp.sum(kh * kh, axis=-1, keepdims=True) + EPS)
        v[h] = act[:, 2 * GDN_QK + h * GDN_DV:2 * GDN_QK + (h + 1) * GDN_DV]
        beta[h] = gates[:, h:h + 1]
        gc[h] = g_cum[:, GDN_HEADS + h:GDN_HEADS + h + 1]
        gr = g_cum_t[GDN_HEADS + h:GDN_HEADS + h + 1, :]
        decay[h] = jnp.exp(jnp.where(causal, gc[h] - gr, -jnp.inf))
        kb = k[h].astype(bf16)
        prod = _dot_nt(jnp.concatenate([q[h], k[h]], axis=0).astype(bf16), kb)
        qk[h] = prod[:r] * decay[h]
        m[h] = jnp.where(strict, -(beta[h] * prod[r:] * decay[h]), 0.0)
        p[h] = eye + m[h]
    for _ in range(nfac - 1):
        for h in heads:
            mb = m[h].astype(bf16)
            m[h] = _dot(mb, mb)
        for h in heads:
            p[h] = p[h] + _dot(p[h].astype(bf16), m[h].astype(bf16))
    u_part, w_part, q_dec = {}, {}, {}
    for h in heads:
        eg[h] = jnp.exp(gc[h])
        rhs = jnp.concatenate([v[h] * beta[h], k[h] * (beta[h] * eg[h])], axis=1)
        sol = _dot(p[h].astype(bf16), rhs.astype(bf16))
        u_part[h] = sol[:, :GDN_DV]
        w_part[h] = sol[:, GDN_DV:]
        q_dec[h] = q[h] * eg[h]
    u, o_state, s_old = {}, {}, {}
    for h in heads:
        us, os_ = [], []
        for j in range(ns):
            rows = slice(j * c, (j + 1) * c)
            s_old[j, h] = s_ref[j, h]
            wq = jnp.concatenate([w_part[h][rows], q_dec[h][rows]], axis=0).astype(bf16)
            wqs = _dot(wq, s_old[j, h].astype(bf16))
            us.append(u_part[h][rows] - wqs[:c])
            os_.append(wqs[c:])
        u[h] = us[0] if ns == 1 else jnp.concatenate(us, axis=0)
        o_state[h] = os_[0] if ns == 1 else jnp.concatenate(os_, axis=0)
    outs, s_new = {}, {}
    for h in heads:
        ub = u[h].astype(bf16)
        outs[h] = o_state[h] + _dot(qk[h].astype(bf16), ub)
        gl = g_last_rows[:, GDN_HEADS + h:GDN_HEADS + h + 1]
        kd = (k[h] * jnp.exp(gl - gc[h])).astype(bf16)
        for j in range(ns):
            uj = ub if ns == 1 else jnp.where(seq_of_row == j, u[h], 0.0).astype(bf16)
            s_new[j, h] = s_old[j, h] * jnp.exp(gl[j * c:j * c + 1, :]) + lax.dot_general(
                kd, uj, (((0,), (0,)), ((), ())), preferred_element_type=f32)
    gn = gn_ref[...]
    for h in heads:
        sl = slice(h * GDN_DV, (h + 1) * GDN_DV)
        y = _rms(outs[h], gn) * jax.nn.silu(z_ref[:, :, sl].astype(f32).reshape(r, GDN_DV))
        o_ref[:, :, sl] = y.reshape(ns, c, GDN_DV).astype(o_ref.dtype)
        for j in range(ns):
            s_ref[j, h] = s_new[j, h]


def _gdn_chunk(proj, ba, hp, conv_w_t, gparams, gdn_norm, s0, *, nseq, c, ns):
    t = proj.shape[0]
    l = t // nseq
    nc = l // c
    assert nseq % ns == 0 and l % c == 0
    tok_spec = lambda width, col: pl.BlockSpec((ns, c, width), lambda b, n: (b, n, col))
    st_spec = pl.BlockSpec((ns, GDN_HEADS, GDN_DK, GDN_DV), lambda b, n: (b, 0, 0, 0))
    proj3 = proj.reshape(nseq, l, MAIN_WIDTH)
    out, s_new = pl.pallas_call(
        functools.partial(_gdn_chunk_body, c=c, ns=ns),
        grid=(nseq // ns, nc),
        in_specs=[
            tok_spec(GDN_CONV_DIM, 0),
            tok_spec(LANES, 0),
            pl.BlockSpec((ns, SUBLANES, GDN_CONV_DIM), lambda b, n: (b, 0, 0)),
            pl.BlockSpec((CONV_WIDTH, GDN_CONV_DIM), lambda b, n: (0, 0)),
            pl.BlockSpec((SUBLANES, LANES), lambda b, n: (0, 0)),
            tok_spec(GDN_V, COL_Z),
            pl.BlockSpec((1, GDN_DV), lambda b, n: (0, 0)),
            st_spec,
        ],
        out_specs=[tok_spec(GDN_V, 0), st_spec],
        out_shape=[
            jax.ShapeDtypeStruct((nseq, l, GDN_V), proj.dtype),
            jax.ShapeDtypeStruct((nseq, GDN_HEADS, GDN_DK, GDN_DV), f32),
        ],
        scratch_shapes=[pltpu.VMEM((ns, SUBLANES, GDN_CONV_DIM), f32)],
        compiler_params=_params(("parallel", "arbitrary")),
        name="gdn_chunk",
    )(proj3, ba.reshape(nseq, l, LANES), hp, conv_w_t, gparams, proj3, gdn_norm.reshape(1, GDN_DV).astype(f32), s0)
    return out.reshape(t, GDN_V), s_new


def _ret_chunk_body(q_ref, k_ref, v_ref, g_ref, cos_ref, sin_ref, rn_ref, s0_ref, o_ref, s_ref, *, c, ns):
    @pl.when(pl.program_id(1) == 0)
    def _():
        s_ref[...] = s0_ref[...]

    half = RET_DK // 2
    cos = cos_ref[...]
    sin = sin_ref[...]
    row = lax.broadcasted_iota(jnp.int32, (c, c), 0)
    col = lax.broadcasted_iota(jnp.int32, (c, c), 1)
    causal = row >= col
    dist = (row - col).astype(f32)
    tpos = lax.broadcasted_iota(jnp.int32, (c, 1), 0).astype(f32)
    src = lax.broadcasted_iota(jnp.int32, (RET_DK, RET_DK), 0)
    dst = lax.broadcasted_iota(jnp.int32, (RET_DK, RET_DK), 1)
    orig = jnp.where(src < half, 2 * src, 2 * (src - half) + 1)
    unperm = (orig == dst).astype(bf16)

    def rot(x):
        xe, xo = x[:, :half], x[:, half:]
        return jnp.concatenate([xe * cos - xo * sin, xo * cos + xe * sin], axis=1)

    heads = range(RET_HEADS)
    lgs = [math.log1p(-(2.0 ** (-5.0 - h))) for h in heads]
    qk, vb, dec = {}, {}, []
    for j in range(ns):
        for h in heads:
            sl = slice(h * RET_DK, (h + 1) * RET_DK)
            q = rot(q_ref[j, :, sl].astype(f32))
            k = rot(k_ref[j, :, sl].astype(f32)) * (RET_DK ** -0.5)
            vb[j, h] = v_ref[j, :, sl].astype(bf16)
            decay = jnp.exp(jnp.where(causal, dist * lgs[h], -jnp.inf))
            qk[j, h] = (_dot_nt(q.astype(bf16), k.astype(bf16)) * decay).astype(bf16)
            dec.append(q * jnp.exp((tpos + 1.0) * lgs[h]))
            dec.append(k * jnp.exp((float(c - 1) - tpos) * lgs[h]))
    dec_orig = _dot(jnp.concatenate(dec, axis=0).astype(bf16), unperm)
    rn = rn_ref[...]
    for j in range(ns):
        outs = []
        for h in heads:
            base = 2 * (j * RET_HEADS + h) * c
            q_dec = dec_orig[base:base + c].astype(bf16)
            k_dec = dec_orig[base + c:base + 2 * c].astype(bf16)
            sl = slice(h * RET_DV, (h + 1) * RET_DV)
            s = s_ref[j, h]
            o = _dot(qk[j, h], vb[j, h]) + _dot(q_dec, s.astype(bf16))
            outs.append(_rms(o, rn[:, sl]) * jax.nn.silu(g_ref[j, :, sl].astype(f32)))
            s_ref[j, h] = s * math.exp(c * lgs[h]) + lax.dot_general(
                k_dec, vb[j, h], (((0,), (0,)), ((), ())), preferred_element_type=f32)
        o_ref[j * c:(j + 1) * c, :] = jnp.concatenate(outs, axis=1).astype(o_ref.dtype)


def _ret_chunk(proj3, cos, sin, ret_norm, s0, *, c, ns):
    nseq, l, _ = proj3.shape
    nc = l // c
    assert nseq % ns == 0 and (ns == 1 or nc == 1)
    st_spec = pl.BlockSpec((ns, RET_HEADS, RET_DK, RET_DV), lambda b, n: (b, 0, 0, 0))
    col_spec = lambda col: pl.BlockSpec((ns, c, RET_QK), lambda b, n: (b, n, col))
    tab_spec = pl.BlockSpec((c, RET_DK // 2), lambda b, n: (n, 0))
    return pl.pallas_call(
        functools.partial(_ret_chunk_body, c=c, ns=ns),
        grid=(nseq // ns, nc),
        in_specs=[col_spec(COL_RQ), col_spec(COL_RK), col_spec(COL_RV), col_spec(COL_RG), tab_spec, tab_spec,
                  pl.BlockSpec((1, RET_V), lambda b, n: (0, 0)), st_spec],
        out_specs=[pl.BlockSpec((ns * c, RET_V), lambda b, n: (b * nc + n, 0)), st_spec],
        out_shape=[
            jax.ShapeDtypeStruct((nseq * l, RET_V), proj3.dtype),
            jax.ShapeDtypeStruct((nseq, RET_HEADS, RET_DK, RET_DV), f32),
        ],
        compiler_params=_params(("parallel", "arbitrary")),
        name="ret_chunk",
    )(proj3, proj3, proj3, proj3, cos, sin, ret_norm.reshape(1, RET_V).astype(f32), s0)


def _out_proj_body(mg_ref, mr_ref, wg_ref, wr_ref, x_ref, o_ref):
    o_ref[...] = x_ref[...] + (_dot(mg_ref[...].astype(bf16), wg_ref[...].astype(bf16))
                               + _dot(mr_ref[...].astype(bf16), wr_ref[...].astype(bf16)))


def _out_proj(mix_g, mix_r, w_out, x, *, tm=1024, tn=1024):
    t = x.shape[0]
    tm = min(tm, t)
    assert GDN_V == RET_V and t % tm == 0 and D_MODEL % tn == 0
    return pl.pallas_call(
        _out_proj_body,
        grid=(t // tm, D_MODEL // tn),
        in_specs=[
            pl.BlockSpec((tm, GDN_V), lambda i, j: (i, 0)),
            pl.BlockSpec((tm, RET_V), lambda i, j: (i, 0)),
            pl.BlockSpec((GDN_V, tn), lambda i, j: (0, j)),
            pl.BlockSpec((RET_V, tn), lambda i, j: (1, j)),
            pl.BlockSpec((tm, tn), lambda i, j: (i, j)),
        ],
        out_specs=pl.BlockSpec((tm, tn), lambda i, j: (i, j)),
        out_shape=jax.ShapeDtypeStruct((t, D_MODEL), f32),
        compiler_params=_params(("parallel", "parallel")),
        name="out_proj",
    )(mix_g, mix_r, w_out, w_out, x)


def _attn_body(q_ref, k_ref, v_ref, o_ref):
    for h in range(MEM_HEADS):
        sl = slice(h * MEM_HEAD_DIM, (h + 1) * MEM_HEAD_DIM)
        qh = q_ref[:, sl].astype(bf16)
        kh = k_ref[0, :, sl].astype(bf16)
        vh = v_ref[0, :, sl].astype(bf16)
        s = _dot_nt(qh, kh) * (MEM_HEAD_DIM ** -0.5)
        e = jnp.exp(s - jnp.max(s, axis=-1, keepdims=True))
        p = e / jnp.sum(e, axis=-1, keepdims=True)
        o_ref[:, sl] = _dot(p.astype(bf16), vh).astype(o_ref.dtype)


def _attn_cache_body(q_ref, k_ref, v_ref, o_ref, *, tq, ns):
    rows = N_MEM * MEM_HEADS
    shape = (MEM_HEADS * tq, rows)
    own_head = (lax.broadcasted_iota(jnp.int32, shape, 0) // tq) == (lax.broadcasted_iota(jnp.int32, shape, 1) % MEM_HEADS)
    for j in range(ns):
        rj = slice(j * tq, (j + 1) * tq)
        k2 = k_ref[j].reshape(rows, MEM_HEAD_DIM).astype(bf16)
        v2 = v_ref[j].reshape(rows, MEM_HEAD_DIM).astype(bf16)
        qs = jnp.concatenate([q_ref[rj, h * MEM_HEAD_DIM:(h + 1) * MEM_HEAD_DIM] for h in range(MEM_HEADS)], axis=0)
        s = _dot_nt(qs.astype(bf16), k2) * (MEM_HEAD_DIM ** -0.5)
        s = jnp.where(own_head, s, -jnp.inf)
        e = jnp.exp(s - jnp.max(s, axis=-1, keepdims=True))
        p = e / jnp.sum(e, axis=-1, keepdims=True)
        o = _dot(p.astype(bf16), v2)
        for h in range(MEM_HEADS):
            o_ref[rj, h * MEM_HEAD_DIM:(h + 1) * MEM_HEAD_DIM] = o[h * tq:(h + 1) * tq].astype(o_ref.dtype)


def _attention(q, mem_k, mem_v, *, l, tq, ns=1):
    t = q.shape[0]
    nq = l // tq
    width = MEM_HEADS * MEM_HEAD_DIM
    if mem_k.ndim == 4:
        assert nq == 1 and (t // l) % ns == 0
        body = functools.partial(_attn_cache_body, tq=tq, ns=ns)
        mem_spec = pl.BlockSpec((ns, N_MEM, MEM_HEADS, MEM_HEAD_DIM), lambda b, i: (b, 0, 0, 0))
    else:
        assert ns == 1
        body = _attn_body
        mem_spec = pl.BlockSpec((1, N_MEM, width), lambda b, i: (b, 0, 0))
    return pl.pallas_call(
        body,
        grid=(t // (l * ns), nq),
        in_specs=[
            pl.BlockSpec((ns * tq, width), lambda b, i: (b * nq + i, 0)),
            mem_spec,
            mem_spec,
        ],
        out_specs=pl.BlockSpec((ns * tq, width), lambda b, i: (b * nq + i, 0)),
        out_shape=jax.ShapeDtypeStruct((t, width), q.dtype),
        compiler_params=_params(("parallel", "arbitrary")),
        name="mem_attn",
    )(q, mem_k, mem_v)


def _ffn_body(x_ref, g_ref, wg_ref, wu_ref, wd_ref, gf_ref, o_ref, xn_ref):
    f = pl.program_id(1)

    @pl.when(f == 0)
    def _():
        x = x_ref[...]
        xn_ref[...] = _rms(x, g_ref[...]).astype(bf16)
        o_ref[...] = x

    xn = xn_ref[...]
    a = _dot(xn, wg_ref[...].astype(bf16))
    u = _dot(xn, wu_ref[...].astype(bf16))
    hidden = (jax.nn.silu(a) * u).astype(bf16)
    o_ref[...] += _dot(hidden, wd_ref[...].astype(bf16))

    @pl.when(f == pl.num_programs(1) - 1)
    def _():
        o_ref[...] = _rms(o_ref[...], gf_ref[...])


def _ffn(x, norm_ffn, w_gate, w_up, w_down, norm_final, *, tm=1024, tf=256):
    t = x.shape[0]
    tm = min(tm, t)
    assert D_FF % tf == 0 and t % tm == 0
    return pl.pallas_call(
        _ffn_body,
        grid=(t // tm, D_FF // tf),
        in_specs=[
            pl.BlockSpec((tm, D_MODEL), lambda i, f: (i, 0)),
            pl.BlockSpec((1, D_MODEL), lambda i, f: (0, 0)),
            pl.BlockSpec((D_MODEL, tf), lambda i, f: (0, f)),
            pl.BlockSpec((D_MODEL, tf), lambda i, f: (0, f)),
            pl.BlockSpec((tf, D_MODEL), lambda i, f: (f, 0)),
            pl.BlockSpec((1, D_MODEL), lambda i, f: (0, 0)),
        ],
        out_specs=pl.BlockSpec((tm, D_MODEL), lambda i, f: (i, 0)),
        out_shape=jax.ShapeDtypeStruct((t, D_MODEL), f32),
        scratch_shapes=[pltpu.VMEM((tm, D_MODEL), bf16)],
        compiler_params=_params(("parallel", "arbitrary")),
        name="ffn",
    )(x, norm_ffn.reshape(1, D_MODEL).astype(f32), w_gate, w_up, w_down, norm_final.reshape(1, D_MODEL).astype(f32))


def _rotary_tables(l, pos0):
    half = RET_DK // 2
    inv = ROPE_BASE ** (-jnp.linspace(0.0, 1.0, half, dtype=f32))
    pos = jnp.arange(l, dtype=f32) + pos0
    ang = pos[:, None] * inv[None, :]
    return jnp.cos(ang), jnp.sin(ang)


def _layer(x3, pos0, conv_hist, s_gdn, s_ret, mem_k, mem_v, wts, *, gdn_c, gdn_ns, ret_c, ret_ns, attn_tq, attn_ns):
    nseq, l, _ = x3.shape
    t = nseq * l
    x = x3.reshape(t, D_MODEL)
    proj_dtype = bf16 if l % (2 * SUBLANES) == 0 else f32
    proj, ba = _in_proj(x, wts["norm_mix"], wts["w_in"], wts["w_in_ba"], out_dtype=proj_dtype)
    proj3 = proj.reshape(nseq, l, MAIN_WIDTH)
    hp = jnp.pad(conv_hist.astype(f32), ((0, 0), (SUBLANES - (CONV_WIDTH - 1), 0), (0, 0)))
    mix_g, s_gdn_new = _gdn_chunk(proj, ba, hp, wts["conv_w_t"], wts["gparams"], wts["gdn_norm"], s_gdn.astype(f32),
                                  nseq=nseq, c=gdn_c, ns=gdn_ns)
    cos, sin = _rotary_tables(l, pos0)
    mix_r, s_ret_new = _ret_chunk(proj3, cos, sin, wts["ret_norm"], s_ret.astype(f32), c=ret_c, ns=ret_ns)
    conv_new = proj3[:, l - (CONV_WIDTH - 1):, :GDN_CONV_DIM].astype(f32)
    x1 = _out_proj(mix_g, mix_r, wts["w_out"], x)
    act_dtype = bf16 if attn_tq % (2 * SUBLANES) == 0 else f32
    qm = _norm_matmul(x1, wts["norm_cross"], wts["w_mem_q"], name="mem_q", out_dtype=act_dtype)
    om = _attention(qm, mem_k, mem_v, l=l, tq=attn_tq, ns=attn_ns)
    x2 = _norm_matmul(om, wts["norm_cross"], wts["w_mem_o"], residual=x1, norm=False, name="mem_o",
                      tn=1024 if om.dtype == bf16 else 512)
    y = _ffn(x2, wts["norm_ffn"], wts["w_gate"], wts["w_up"], wts["w_down"], wts["norm_final"])
    return y.reshape(nseq, l, D_MODEL), conv_new, s_gdn_new, s_ret_new


W_PREP_TILE = 256


def _w_in_prep_body(wt_ref, bat_ref, o_ref, ba_ref):
    j = pl.program_id(0)
    first_ret = (GDN_CONV_DIM + GDN_V) // W_PREP_TILE
    is_ret_qk = (j >= first_ret) & (j < first_ret + 2 * RET_QK // W_PREP_TILE)

    @pl.when(is_ret_qk)
    def _():
        half = RET_DK // 2
        src = lax.broadcasted_iota(jnp.int32, (RET_DK, RET_DK), 0)
        dst = lax.broadcasted_iota(jnp.int32, (RET_DK, RET_DK), 1)
        perm = (src == jnp.where(dst < half, 2 * dst, 2 * (dst - half) + 1)).astype(bf16)
        o_ref[...] = _dot(wt_ref[...].T.astype(bf16), perm).astype(bf16)

    @pl.when(jnp.logical_not(is_ret_qk))
    def _():
        o_ref[...] = wt_ref[...].T.astype(bf16)

    @pl.when(j == 0)
    def _():
        ba = bat_ref[...].T
        ba_ref[...] = jnp.concatenate([ba, jnp.zeros((ba.shape[0], LANES - 2 * GDN_HEADS), f32)], axis=1).astype(bf16)


def _w_in_prep(w_in):
    k, n = w_in.shape
    i0 = GDN_CONV_DIM + GDN_V
    n_gate = 2 * GDN_HEADS
    assert RET_DK == W_PREP_TILE and i0 % W_PREP_TILE == 0 and n == MAIN_WIDTH + n_gate and i0 % n_gate == 0
    wt = jnp.swapaxes(w_in, 0, 1)
    src_row = lambda j: (pl.multiple_of(j * W_PREP_TILE + n_gate * (j * W_PREP_TILE >= i0).astype(jnp.int32), SUBLANES), 0)
    return pl.pallas_call(
        _w_in_prep_body,
        grid=(MAIN_WIDTH // W_PREP_TILE,),
        in_specs=[pl.BlockSpec((pl.Element(W_PREP_TILE), pl.Element(k)), src_row),
                  pl.BlockSpec((n_gate, k), lambda j: (i0 // n_gate, 0))],
        out_specs=[pl.BlockSpec((k, W_PREP_TILE), lambda j: (0, j)), pl.BlockSpec((k, LANES), lambda j: (0, 0))],
        out_shape=[jax.ShapeDtypeStruct((k, MAIN_WIDTH), bf16), jax.ShapeDtypeStruct((k, LANES), bf16)],
        compiler_params=_params(("arbitrary",)),
        name="w_in_prep",
    )(wt, wt)


def _prep_weights(norm_mix, w_in, conv_w, gdn_a_log, gdn_dt_bias, gdn_norm, ret_norm, w_out,
                  norm_cross, w_mem_q, w_mem_o, norm_ffn, w_gate, w_up, w_down, norm_final):
    w_main, w_ba = _w_in_prep(w_in.astype(f32))
    gparams = jnp.zeros((SUBLANES, LANES), f32)
    gparams = gparams.at[0, GDN_HEADS:2 * GDN_HEADS].set(gdn_a_log.astype(f32))
    gparams = gparams.at[1, GDN_HEADS:2 * GDN_HEADS].set(gdn_dt_bias.astype(f32))
    return {
        "norm_mix": norm_mix, "w_in": w_main, "w_in_ba": w_ba,
        "conv_w_t": conv_w.astype(f32).T, "gparams": gparams,
        "gdn_norm": gdn_norm, "ret_norm": ret_norm, "w_out": w_out,
        "norm_cross": norm_cross, "w_mem_q": w_mem_q, "w_mem_o": w_mem_o,
        "norm_ffn": norm_ffn, "w_gate": w_gate, "w_up": w_up, "w_down": w_down, "norm_final": norm_final,
    }


def kernel(x_prompt, x_sample, mem_prompt, cache_mem_k, cache_mem_v, state_gdn, state_gdn_conv, state_ret,
           norm_mix, w_in, conv_w, gdn_a_log, gdn_dt_bias, gdn_norm, ret_norm, w_out,
           norm_mem_in, norm_cross, w_mem_q, w_mem_k, w_mem_v, w_mem_o,
           norm_ffn, w_gate, w_up, w_down, norm_final):
    depth = w_in.shape[0]
    assert depth == 1
    batch, seq, _ = x_prompt.shape
    dec_batch, dec_seq, _ = x_sample.shape
    width = MEM_HEADS * MEM_HEAD_DIM
    wts = _prep_weights(norm_mix[0], w_in[0], conv_w[0], gdn_a_log[0], gdn_dt_bias[0], gdn_norm[0], ret_norm[0],
                        w_out[0], norm_cross[0], w_mem_q[0], w_mem_o[0], norm_ffn[0], w_gate[0], w_up[0], w_down[0],
                        norm_final)

    mem2 = mem_prompt.reshape(batch * N_MEM, D_MODEL)
    mk = _norm_matmul(mem2, norm_mem_in[0], w_mem_k[0], name="mem_k")
    mv = _norm_matmul(mem2, norm_mem_in[0], w_mem_v[0], name="mem_v")
    mk4 = mk.reshape(batch, N_MEM, MEM_HEADS, MEM_HEAD_DIM)
    mv4 = mv.reshape(batch, N_MEM, MEM_HEADS, MEM_HEAD_DIM)
    y_p, conv_p, sg_p, sr_p = _layer(
        x_prompt, 0.0,
        jnp.zeros((batch, CONV_WIDTH - 1, GDN_CONV_DIM), f32),
        jnp.zeros((batch, GDN_HEADS, GDN_DK, GDN_DV), f32),
        jnp.zeros((batch, RET_HEADS, RET_DK, RET_DV), f32),
        mk.reshape(batch, N_MEM, width), mv.reshape(batch, N_MEM, width), wts,
        gdn_c=min(GDN_CHUNK, seq), gdn_ns=min(2, batch), ret_c=min(RET_CHUNK, seq), ret_ns=1,
        attn_tq=min(512, seq), attn_ns=1)

    y_s, conv_s, sg_s, sr_s = _layer(
        x_sample, float(PAST_LEN), state_gdn_conv[0], state_gdn[0], state_ret[0],
        cache_mem_k[0], cache_mem_v[0], wts,
        gdn_c=dec_seq, gdn_ns=min(GDN_CHUNK // dec_seq, dec_batch),
        ret_c=dec_seq, ret_ns=1, attn_tq=dec_seq, attn_ns=min(2, dec_batch))

    pdt = x_prompt.dtype
    sdt = x_sample.dtype
    return (y_p.astype(pdt), y_s.astype(sdt),
            sg_p.astype(pdt)[None], conv_p.astype(pdt)[None], sr_p.astype(pdt)[None],
            mk4[None], mv4[None],
            sg_s.astype(sdt)[None], conv_s.astype(sdt)[None], sr_s.astype(sdt)[None])
```

```python
import functools
import math

import jax
import jax.numpy as jnp
from jax import lax
from jax.experimental import pallas as pl
from jax.experimental.pallas import tpu as pltpu

f32 = jnp.float32
bf16 = jnp.bfloat16

D_MODEL = 2048
GDN_HEADS = 8
GDN_DK = 128
GDN_DV = 128
RET_HEADS = 4
RET_DK = 256
RET_DV = 256
CONV_WIDTH = 4
N_MEM = 256
MEM_HEADS = 4
MEM_HEAD_DIM = D_MODEL // MEM_HEADS
D_FF = -(-8 * D_MODEL // (3 * 256)) * 256
EPS = 1e-6
ROPE_BASE = 10000.0
PAST_LEN = 16384

GDN_QK = GDN_HEADS * GDN_DK
GDN_V = GDN_HEADS * GDN_DV
GDN_CONV_DIM = 2 * GDN_QK + GDN_V
RET_QK = RET_HEADS * RET_DK
RET_V = RET_HEADS * RET_DV

LANES = 128
SUBLANES = 8
VMEM_LIMIT = 56 * 1024 * 1024

GDN_CHUNK = 64
RET_CHUNK = 256

COL_Z = GDN_CONV_DIM // GDN_V
COL_RQ = COL_Z + 1
COL_RK = COL_Z + 2
COL_RV = COL_Z + 3
COL_RG = COL_Z + 4
MAIN_WIDTH = GDN_CONV_DIM + GDN_V + 2 * RET_QK + 2 * RET_V


def _params(sem):
    return pltpu.CompilerParams(dimension_semantics=sem, vmem_limit_bytes=VMEM_LIMIT)


def _dot(a, b):
    return jnp.dot(a, b, preferred_element_type=f32)


def _dot_nt(a, b):
    return lax.dot_general(a, b, (((1,), (1,)), ((), ())), preferred_element_type=f32)


def _split2(a):
    hi = a.astype(bf16)
    lo = (a - hi.astype(f32)).astype(bf16)
    return hi, lo


def _dot3(a, b):
    ah, al = _split2(a)
    bh, bl = _split2(b)
    return _dot(ah, bh) + (_dot(ah, bl) + _dot(al, bh))


def _dot_exact_lhs(a_bf16, b):
    b1 = b.astype(bf16)
    r1 = b - b1.astype(f32)
    b2 = r1.astype(bf16)
    b3 = (r1 - b2.astype(f32)).astype(bf16)
    return _dot(a_bf16, b1) + (_dot(a_bf16, b2) + _dot(a_bf16, b3))


def _rms(x, g):
    return x * lax.rsqrt(jnp.mean(x * x, axis=-1, keepdims=True) + EPS) * g


def _nm_body(*refs, norm, residual):
    if residual:
        x_ref, g_ref, w_ref, r_ref, o_ref, xn_ref = refs
    else:
        x_ref, g_ref, w_ref, o_ref, xn_ref = refs

    @pl.when(pl.program_id(1) == 0)
    def _():
        x = x_ref[...].astype(f32)
        if norm:
            x = _rms(x, g_ref[...])
        xn_ref[...] = x.astype(bf16)

    acc = _dot(xn_ref[...], w_ref[...].astype(bf16))
    if residual:
        acc = r_ref[...] + acc
    o_ref[...] = acc.astype(o_ref.dtype)


def _norm_matmul(x, gain, w, *, name, residual=None, norm=True, tm=1024, tn=1024, out_dtype=f32):
    m, k = x.shape
    n = w.shape[1]
    tm = min(tm, m)
    tn = min(tn, n)
    assert m % tm == 0 and n % tn == 0
    in_specs = [
        pl.BlockSpec((tm, k), lambda i, j: (i, 0)),
        pl.BlockSpec((1, k), lambda i, j: (0, 0)),
        pl.BlockSpec((k, tn), lambda i, j: (0, j)),
    ]
    args = [x, gain.reshape(1, k).astype(f32), w]
    if residual is not None:
        in_specs.append(pl.BlockSpec((tm, tn), lambda i, j: (i, j)))
        args.append(residual)
    return pl.pallas_call(
        functools.partial(_nm_body, norm=norm, residual=residual is not None),
        grid=(m // tm, n // tn),
        in_specs=in_specs,
        out_specs=pl.BlockSpec((tm, tn), lambda i, j: (i, j)),
        out_shape=jax.ShapeDtypeStruct((m, n), out_dtype),
        scratch_shapes=[pltpu.VMEM((tm, k), bf16)],
        compiler_params=_params(("parallel", "arbitrary")),
        name=name,
    )(*args)


def _in_proj_body(x_ref, g_ref, w_ref, wba_ref, o_ref, ba_ref, xn_ref):
    @pl.when(pl.program_id(1) == 0)
    def _():
        xn = _rms(x_ref[...], g_ref[...]).astype(bf16)
        xn_ref[...] = xn
        ba_ref[...] = _dot(xn, wba_ref[...])

    o_ref[...] = _dot(xn_ref[...], w_ref[...]).astype(o_ref.dtype)


def _in_proj(x, gain, w_main, w_ba, *, out_dtype, tm=1024, tn=2048):
    m, k = x.shape
    tm = min(tm, m)
    tn = tn * 2 // jnp.dtype(out_dtype).itemsize
    assert m % tm == 0 and MAIN_WIDTH % tn == 0
    return pl.pallas_call(
        _in_proj_body,
        grid=(m // tm, MAIN_WIDTH // tn),
        in_specs=[
            pl.BlockSpec((tm, k), lambda i, j: (i, 0)),
            pl.BlockSpec((1, k), lambda i, j: (0, 0)),
            pl.BlockSpec((k, tn), lambda i, j: (0, j)),
            pl.BlockSpec((k, LANES), lambda i, j: (0, 0)),
        ],
        out_specs=[pl.BlockSpec((tm, tn), lambda i, j: (i, j)), pl.BlockSpec((tm, LANES), lambda i, j: (i, 0))],
        out_shape=[jax.ShapeDtypeStruct((m, MAIN_WIDTH), out_dtype), jax.ShapeDtypeStruct((m, LANES), f32)],
        scratch_shapes=[pltpu.VMEM((tm, k), bf16)],
        compiler_params=_params(("parallel", "arbitrary")),
        name="in_proj",
    )(x, gain.reshape(1, k).astype(f32), w_main, w_ba)


def _gdn_chunk_body(u_ref, ba_ref, hp_ref, cw_ref, gp_ref, z_ref, gn_ref, s0_ref, o_ref, s_ref, carry_ref, *, c, ns):
    @pl.when(pl.program_id(1) == 0)
    def _():
        s_ref[...] = s0_ref[...]
        carry_ref[...] = hp_ref[...]

    r = ns * c
    nfac = int(round(math.log2(c)))
    assert 2 ** nfac == c
    heads = range(GDN_HEADS)
    u = u_ref[...].astype(f32)
    cdim = u.shape[-1]
    ext = jnp.concatenate([carry_ref[...], u], axis=1)
    carry_ref[...] = u[:, c - SUBLANES:, :]
    ext2 = ext.reshape(ns * (SUBLANES + c), cdim)
    cw = cw_ref[...]
    acc = ext2 * cw[CONV_WIDTH - 1:CONV_WIDTH, :]
    for sft in range(1, CONV_WIDTH):
        acc = acc + pltpu.roll(ext2, sft, axis=0) * cw[CONV_WIDTH - 1 - sft:CONV_WIDTH - sft, :]
    act = jax.nn.silu(acc.reshape(ns, SUBLANES + c, cdim)[:, SUBLANES:, :].reshape(r, cdim))
    ba = ba_ref[...].reshape(r, LANES)
    gp = gp_ref[...]
    log_decay = -jnp.exp(gp[0:1, :]) * jax.nn.softplus(ba + gp[1:2, :])
    lane = lax.broadcasted_iota(jnp.int32, ba.shape, 1)
    gates = jnp.where(lane < GDN_HEADS, jax.nn.sigmoid(ba), log_decay)
    row = lax.broadcasted_iota(jnp.int32, (r, r), 0)
    col = lax.broadcasted_iota(jnp.int32, (r, r), 1)
    same = (row // c) == (col // c)
    causal = same & (row >= col)
    strict = same & (row > col)
    g_cum = _dot_exact_lhs(causal.astype(bf16), gates)
    g_cum_t = g_cum.T
    seq_of_row = lax.broadcasted_iota(jnp.int32, (r, 1), 0) // c
    last = ((row // c) == (col // c)) & ((col % c) == c - 1)
    g_last_rows = _dot_exact_lhs(last.astype(bf16), g_cum)

    q, k, v, beta, gc, eg, decay, qk, m = ({} for _ in range(9))
    for h in heads:
        sl = slice(h * GDN_DK, (h + 1) * GDN_DK)
        qh = act[:, sl]
        kh = act[:, GDN_QK + h * GDN_DK:GDN_QK + (h + 1) * GDN_DK]
        q[h] = qh * lax.rsqrt(jnp.sum(qh * qh, axis=-1, keepdims=True) + EPS) * (GDN_DK ** -0.5)
        k[h] = kh * lax.rsqrt(jnp.sum(kh * kh, axis=-1, keepdims=True) + EPS)
        v[h] = act[:, 2 * GDN_QK + h * GDN_DV:2 * GDN_QK + (h + 1) * GDN_DV]
        beta[h] = gates[:, h:h + 1]
        gc[h] = g_cum[:, GDN_HEADS + h:GDN_HEADS + h + 1]
        gr = g_cum_t[GDN_HEADS + h:GDN_HEADS + h + 1, :]
        decay[h] = jnp.exp(jnp.where(causal, gc[h] - gr, -jnp.inf))
        kb = k[h].astype(bf16)
        prod = _dot_nt(jnp.concatenate([q[h], k[h]], axis=0).astype(bf16), kb)
        qk[h] = prod[:r] * decay[h]
        m[h] = jnp.where(strict, -(beta[h] * prod[r:] * decay[h]), 0.0)
    p = {h: jnp.where(row == col, 1.0, m[h]) for h in heads}
    for _ in range(nfac - 1):
        for h in heads:
            mb = m[h].astype(bf16)
            m[h] = _dot(mb, mb)
        for h in heads:
            p[h] = p[h] + _dot(p[h].astype(bf16), m[h].astype(bf16))
    u_part, w_part, q_dec = {}, {}, {}
    for h in heads:
        eg[h] = jnp.exp(gc[h])
        rhs = jnp.concatenate([v[h] * beta[h], k[h] * (beta[h] * eg[h])], axis=1)
        sol = _dot(p[h].astype(bf16), rhs.astype(bf16))
        u_part[h] = sol[:, :GDN_DV]
        w_part[h] = sol[:, GDN_DV:]
        q_dec[h] = q[h] * eg[h]
    u, o_state, s_old = {}, {}, {}
    for h in heads:
        us, os_ = [], []
        for j in range(ns):
            rows = slice(j * c, (j + 1) * c)
            s_old[j, h] = s_ref[j, h]
            wq = jnp.concatenate([w_part[h][rows], q_dec[h][rows]], axis=0).astype(bf16)
            wqs = _dot(wq, s_old[j, h].astype(bf16))
            us.append(u_part[h][rows] - wqs[:c])
            os_.append(wqs[c:])
        u[h] = us[0] if ns == 1 else jnp.concatenate(us, axis=0)
        o_state[h] = os_[0] if ns == 1 else jnp.concatenate(os_, axis=0)
    outs, s_new = {}, {}
    for h in heads:
        ub = u[h].astype(bf16)
        outs[h] = o_state[h] + _dot(qk[h].astype(bf16), ub)
        gl = g_last_rows[:, GDN_HEADS + h:GDN_HEADS + h + 1]
        kd = (k[h] * jnp.exp(gl - gc[h])).astype(bf16)
        for j in range(ns):
            uj = ub if ns == 1 else jnp.where(seq_of_row == j, u[h], 0.0).astype(bf16)
            s_new[j, h] = s_old[j, h] * jnp.exp(gl[j * c:j * c + 1, :]) + lax.dot_general(
                kd, uj, (((0,), (0,)), ((), ())), preferred_element_type=f32)
    gn = gn_ref[...]
    for h in heads:
        sl = slice(h * GDN_DV, (h + 1) * GDN_DV)
        y = _rms(outs[h], gn) * jax.nn.silu(z_ref[:, :, sl].astype(f32).reshape(r, GDN_DV))
        o_ref[:, :, sl] = y.reshape(ns, c, GDN_DV).astype(o_ref.dtype)
        for j in range(ns):
            s_ref[j, h] = s_new[j, h]


def _gdn_chunk(proj, ba, hp, conv_w_t, gparams, gdn_norm, s0, *, nseq, c, ns):
    t = proj.shape[0]
    l = t // nseq
    nc = l // c
    assert nseq % ns == 0 and l % c == 0
    tok_spec = lambda width, col: pl.BlockSpec((ns, c, width), lambda b, n: (b, n, col))
    st_spec = pl.BlockSpec((ns, GDN_HEADS, GDN_DK, GDN_DV), lambda b, n: (b, 0, 0, 0))
    proj3 = proj.reshape(nseq, l, MAIN_WIDTH)
    out, s_new = pl.pallas_call(
        functools.partial(_gdn_chunk_body, c=c, ns=ns),
        grid=(nseq // ns, nc),
        in_specs=[
            tok_spec(GDN_CONV_DIM, 0),
            tok_spec(LANES, 0),
            pl.BlockSpec((ns, SUBLANES, GDN_CONV_DIM), lambda b, n: (b, 0, 0)),
            pl.BlockSpec((CONV_WIDTH, GDN_CONV_DIM), lambda b, n: (0, 0)),
            pl.BlockSpec((SUBLANES, LANES), lambda b, n: (0, 0)),
            tok_spec(GDN_V, COL_Z),
            pl.BlockSpec((1, GDN_DV), lambda b, n: (0, 0)),
            st_spec,
        ],
        out_specs=[tok_spec(GDN_V, 0), st_spec],
        out_shape=[
            jax.ShapeDtypeStruct((nseq, l, GDN_V), proj.dtype),
            jax.ShapeDtypeStruct((nseq, GDN_HEADS, GDN_DK, GDN_DV), f32),
        ],
        scratch_shapes=[pltpu.VMEM((ns, SUBLANES, GDN_CONV_DIM), f32)],
        compiler_params=_params(("parallel", "arbitrary")),
        name="gdn_chunk",
    )(proj3, ba.reshape(nseq, l, LANES), hp, conv_w_t, gparams, proj3, gdn_norm.reshape(1, GDN_DV).astype(f32), s0)
    return out.reshape(t, GDN_V), s_new


def _ret_chunk_body(q_ref, k_ref, v_ref, g_ref, cos_ref, sin_ref, rn_ref, s0_ref, o_ref, s_ref, *, c, ns):
    @pl.when(pl.program_id(1) == 0)
    def _():
        s_ref[...] = s0_ref[...]

    half = RET_DK // 2
    cos = cos_ref[...]
    sin = sin_ref[...]
    row = lax.broadcasted_iota(jnp.int32, (c, c), 0)
    col = lax.broadcasted_iota(jnp.int32, (c, c), 1)
    causal = row >= col
    dist = (row - col).astype(f32)
    tpos = lax.broadcasted_iota(jnp.int32, (c, 1), 0).astype(f32)
    src = lax.broadcasted_iota(jnp.int32, (RET_DK, RET_DK), 0)
    dst = lax.broadcasted_iota(jnp.int32, (RET_DK, RET_DK), 1)
    orig = jnp.where(src < half, 2 * src, 2 * (src - half) + 1)
    unperm = (orig == dst).astype(bf16)

    def rot(x):
        xe, xo = x[:, :half], x[:, half:]
        return jnp.concatenate([xe * cos - xo * sin, xo * cos + xe * sin], axis=1)

    heads = range(RET_HEADS)
    lgs = [math.log1p(-(2.0 ** (-5.0 - h))) for h in heads]
    qk, vb, dec = {}, {}, []
    for j in range(ns):
        for h in heads:
            sl = slice(h * RET_DK, (h + 1) * RET_DK)
            q = rot(q_ref[j, :, sl].astype(f32))
            k = rot(k_ref[j, :, sl].astype(f32)) * (RET_DK ** -0.5)
            vb[j, h] = v_ref[j, :, sl].astype(bf16)
            decay = jnp.exp(jnp.where(causal, dist * lgs[h], -jnp.inf))
            qk[j, h] = (_dot_nt(q.astype(bf16), k.astype(bf16)) * decay).astype(bf16)
            dec.append(q * jnp.exp((tpos + 1.0) * lgs[h]))
            dec.append(k * jnp.exp((float(c - 1) - tpos) * lgs[h]))
    dec_orig = _dot(jnp.concatenate(dec, axis=0).astype(bf16), unperm)
    rn = rn_ref[...]
    for j in range(ns):
        outs = []
        for h in heads:
            base = 2 * (j * RET_HEADS + h) * c
            q_dec = dec_orig[base:base + c].astype(bf16)
            k_dec = dec_orig[base + c:base + 2 * c].astype(bf16)
            sl = slice(h * RET_DV, (h + 1) * RET_DV)
            s = s_ref[j, h]
            o = _dot(qk[j, h], vb[j, h]) + _dot(q_dec, s.astype(bf16))
            outs.append(_rms(o, rn[:, sl]) * jax.nn.silu(g_ref[j, :, sl].astype(f32)))
            s_ref[j, h] = s * math.exp(c * lgs[h]) + lax.dot_general(
                k_dec, vb[j, h], (((0,), (0,)), ((), ())), preferred_element_type=f32)
        o_ref[j * c:(j + 1) * c, :] = jnp.concatenate(outs, axis=1).astype(o_ref.dtype)


def _ret_chunk(proj3, cos, sin, ret_norm, s0, *, c, ns):
    nseq, l, _ = proj3.shape
    nc = l // c
    assert nseq % ns == 0 and (ns == 1 or nc == 1)
    st_spec = pl.BlockSpec((ns, RET_HEADS, RET_DK, RET_DV), lambda b, n: (b, 0, 0, 0))
    col_spec = lambda col: pl.BlockSpec((ns, c, RET_QK), lambda b, n: (b, n, col))
    tab_spec = pl.BlockSpec((c, RET_DK // 2), lambda b, n: (n, 0))
    return pl.pallas_call(
        functools.partial(_ret_chunk_body, c=c, ns=ns),
        grid=(nseq // ns, nc),
        in_specs=[col_spec(COL_RQ), col_spec(COL_RK), col_spec(COL_RV), col_spec(COL_RG), tab_spec, tab_spec,
                  pl.BlockSpec((1, RET_V), lambda b, n: (0, 0)), st_spec],
        out_specs=[pl.BlockSpec((ns * c, RET_V), lambda b, n: (b * nc + n, 0)), st_spec],
        out_shape=[
            jax.ShapeDtypeStruct((nseq * l, RET_V), proj3.dtype),
            jax.ShapeDtypeStruct((nseq, RET_HEADS, RET_DK, RET_DV), f32),
        ],
        compiler_params=_params(("parallel", "arbitrary")),
        name="ret_chunk",
    )(proj3, proj3, proj3, proj3, cos, sin, ret_norm.reshape(1, RET_V).astype(f32), s0)


def _out_proj_body(mg_ref, mr_ref, wg_ref, wr_ref, x_ref, o_ref):
    o_ref[...] = x_ref[...] + (_dot(mg_ref[...].astype(bf16), wg_ref[...].astype(bf16))
                               + _dot(mr_ref[...].astype(bf16), wr_ref[...].astype(bf16)))


def _out_proj(mix_g, mix_r, w_out, x, *, tm=2048, tn=512):
    t = x.shape[0]
    tm = min(tm, t)
    assert GDN_V == RET_V and t % tm == 0 and D_MODEL % tn == 0
    return pl.pallas_call(
        _out_proj_body,
        grid=(t // tm, D_MODEL // tn),
        in_specs=[
            pl.BlockSpec((tm, GDN_V), lambda i, j: (i, 0)),
            pl.BlockSpec((tm, RET_V), lambda i, j: (i, 0)),
            pl.BlockSpec((GDN_V, tn), lambda i, j: (0, j)),
            pl.BlockSpec((RET_V, tn), lambda i, j: (1, j)),
            pl.BlockSpec((tm, tn), lambda i, j: (i, j)),
        ],
        out_specs=pl.BlockSpec((tm, tn), lambda i, j: (i, j)),
        out_shape=jax.ShapeDtypeStruct((t, D_MODEL), f32),
        compiler_params=_params(("parallel", "parallel")),
        name="out_proj",
    )(mix_g, mix_r, w_out, w_out, x)


def _attn_body(q_ref, k_ref, v_ref, o_ref):
    for h in range(MEM_HEADS):
        sl = slice(h * MEM_HEAD_DIM, (h + 1) * MEM_HEAD_DIM)
        qh = q_ref[:, sl].astype(bf16)
        kh = k_ref[0, :, sl].astype(bf16)
        vh = v_ref[0, :, sl].astype(bf16)
        s = _dot_nt(qh, kh) * (MEM_HEAD_DIM ** -0.5)
        e = jnp.exp(s - jnp.max(s, axis=-1, keepdims=True))
        p = e / jnp.sum(e, axis=-1, keepdims=True)
        o_ref[:, sl] = _dot(p.astype(bf16), vh).astype(o_ref.dtype)


def _attn_cache_body(q_ref, k_ref, v_ref, o_ref, *, tq, ns):
    rows = N_MEM * MEM_HEADS
    shape = (MEM_HEADS * tq, rows)
    own_head = (lax.broadcasted_iota(jnp.int32, shape, 0) // tq) == (lax.broadcasted_iota(jnp.int32, shape, 1) % MEM_HEADS)
    for j in range(ns):
        rj = slice(j * tq, (j + 1) * tq)
        k2 = k_ref[j].reshape(rows, MEM_HEAD_DIM).astype(bf16)
        v2 = v_ref[j].reshape(rows, MEM_HEAD_DIM).astype(bf16)
        qs = jnp.concatenate([q_ref[rj, h * MEM_HEAD_DIM:(h + 1) * MEM_HEAD_DIM] for h in range(MEM_HEADS)], axis=0)
        s = _dot_nt(qs.astype(bf16), k2) * (MEM_HEAD_DIM ** -0.5)
        s = jnp.where(own_head, s, -jnp.inf)
        e = jnp.exp(s - jnp.max(s, axis=-1, keepdims=True))
        p = e / jnp.sum(e, axis=-1, keepdims=True)
        o = _dot(p.astype(bf16), v2)
        for h in range(MEM_HEADS):
            o_ref[rj, h * MEM_HEAD_DIM:(h + 1) * MEM_HEAD_DIM] = o[h * tq:(h + 1) * tq].astype(o_ref.dtype)


def _attention(q, mem_k, mem_v, *, l, tq, ns=1):
    t = q.shape[0]
    nq = l // tq
    width = MEM_HEADS * MEM_HEAD_DIM
    if mem_k.ndim == 4:
        assert nq == 1 and (t // l) % ns == 0
        body = functools.partial(_attn_cache_body, tq=tq, ns=ns)
        mem_spec = pl.BlockSpec((ns, N_MEM, MEM_HEADS, MEM_HEAD_DIM), lambda b, i: (b, 0, 0, 0))
    else:
        assert ns == 1
        body = _attn_body
        mem_spec = pl.BlockSpec((1, N_MEM, width), lambda b, i: (b, 0, 0))
    return pl.pallas_call(
        body,
        grid=(t // (l * ns), nq),
        in_specs=[
            pl.BlockSpec((ns * tq, width), lambda b, i: (b * nq + i, 0)),
            mem_spec,
            mem_spec,
        ],
        out_specs=pl.BlockSpec((ns * tq, width), lambda b, i: (b * nq + i, 0)),
        out_shape=jax.ShapeDtypeStruct((t, width), q.dtype),
        compiler_params=_params(("parallel", "arbitrary")),
        name="mem_attn",
    )(q, mem_k, mem_v)


def _ffn_body(x_ref, g_ref, wg_ref, wu_ref, wd_ref, gf_ref, o_ref, xn_ref):
    f = pl.program_id(1)

    @pl.when(f == 0)
    def _():
        x = x_ref[...]
        xn_ref[...] = _rms(x, g_ref[...]).astype(bf16)
        o_ref[...] = x

    xn = xn_ref[...]
    a = _dot(xn, wg_ref[...].astype(bf16))
    u = _dot(xn, wu_ref[...].astype(bf16))
    hidden = (jax.nn.silu(a) * u).astype(bf16)
    o_ref[...] += _dot(hidden, wd_ref[...].astype(bf16))

    @pl.when(f == pl.num_programs(1) - 1)
    def _():
        o_ref[...] = _rms(o_ref[...], gf_ref[...])


def _ffn(x, norm_ffn, w_gate, w_up, w_down, norm_final, *, tm=1024, tf=256):
    t = x.shape[0]
    tm = min(tm, t)
    assert D_FF % tf == 0 and t % tm == 0
    return pl.pallas_call(
        _ffn_body,
        grid=(t // tm, D_FF // tf),
        in_specs=[
            pl.BlockSpec((tm, D_MODEL), lambda i, f: (i, 0)),
            pl.BlockSpec((1, D_MODEL), lambda i, f: (0, 0)),
            pl.BlockSpec((D_MODEL, tf), lambda i, f: (0, f)),
            pl.BlockSpec((D_MODEL, tf), lambda i, f: (0, f)),
            pl.BlockSpec((tf, D_MODEL), lambda i, f: (f, 0)),
            pl.BlockSpec((1, D_MODEL), lambda i, f: (0, 0)),
        ],
        out_specs=pl.BlockSpec((tm, D_MODEL), lambda i, f: (i, 0)),
        out_shape=jax.ShapeDtypeStruct((t, D_MODEL), f32),
        scratch_shapes=[pltpu.VMEM((tm, D_MODEL), bf16)],
        compiler_params=_params(("parallel", "arbitrary")),
        name="ffn",
    )(x, norm_ffn.reshape(1, D_MODEL).astype(f32), w_gate, w_up, w_down, norm_final.reshape(1, D_MODEL).astype(f32))


def _rotary_tables(l, pos0):
    half = RET_DK // 2
    inv = ROPE_BASE ** (-jnp.linspace(0.0, 1.0, half, dtype=f32))
    pos = jnp.arange(l, dtype=f32) + pos0
    ang = pos[:, None] * inv[None, :]
    return jnp.cos(ang), jnp.sin(ang)


def _layer(x3, pos0, conv_hist, s_gdn, s_ret, mem_k, mem_v, wts, *, gdn_c, gdn_ns, ret_c, ret_ns, attn_tq, attn_ns):
    nseq, l, _ = x3.shape
    t = nseq * l
    x = x3.reshape(t, D_MODEL)
    proj_dtype = bf16 if l % (2 * SUBLANES) == 0 else f32
    proj, ba = _in_proj(x, wts["norm_mix"], wts["w_in"], wts["w_in_ba"], out_dtype=proj_dtype)
    proj3 = proj.reshape(nseq, l, MAIN_WIDTH)
    hp = jnp.pad(conv_hist.astype(f32), ((0, 0), (SUBLANES - (CONV_WIDTH - 1), 0), (0, 0)))
    mix_g, s_gdn_new = _gdn_chunk(proj, ba, hp, wts["conv_w_t"], wts["gparams"], wts["gdn_norm"], s_gdn.astype(f32),
                                  nseq=nseq, c=gdn_c, ns=gdn_ns)
    cos, sin = _rotary_tables(l, pos0)
    mix_r, s_ret_new = _ret_chunk(proj3, cos, sin, wts["ret_norm"], s_ret.astype(f32), c=ret_c, ns=ret_ns)
    conv_new = proj3[:, l - (CONV_WIDTH - 1):, :GDN_CONV_DIM].astype(f32)
    x1 = _out_proj(mix_g, mix_r, wts["w_out"], x)
    act_dtype = bf16 if attn_tq % (2 * SUBLANES) == 0 else f32
    qm = _norm_matmul(x1, wts["norm_cross"], wts["w_mem_q"], name="mem_q", out_dtype=act_dtype)
    om = _attention(qm, mem_k, mem_v, l=l, tq=attn_tq, ns=attn_ns)
    x2 = _norm_matmul(om, wts["norm_cross"], wts["w_mem_o"], residual=x1, norm=False, name="mem_o",
                      tm=2048 if om.dtype == bf16 else 1024, tn=512)
    y = _ffn(x2, wts["norm_ffn"], wts["w_gate"], wts["w_up"], wts["w_down"], wts["norm_final"])
    return y.reshape(nseq, l, D_MODEL), conv_new, s_gdn_new, s_ret_new


W_PREP_TILE = 256


def _w_in_prep_body(wt_ref, bat_ref, o_ref, ba_ref):
    j = pl.program_id(0)
    first_ret = (GDN_CONV_DIM + GDN_V) // W_PREP_TILE
    is_ret_qk = (j >= first_ret) & (j < first_ret + 2 * RET_QK // W_PREP_TILE)

    @pl.when(is_ret_qk)
    def _():
        half = RET_DK // 2
        src = lax.broadcasted_iota(jnp.int32, (RET_DK, RET_DK), 0)
        dst = lax.broadcasted_iota(jnp.int32, (RET_DK, RET_DK), 1)
        perm = (src == jnp.where(dst < half, 2 * dst, 2 * (dst - half) + 1)).astype(bf16)
        o_ref[...] = _dot(wt_ref[...].T.astype(bf16), perm).astype(bf16)

    @pl.when(jnp.logical_not(is_ret_qk))
    def _():
        o_ref[...] = wt_ref[...].T.astype(bf16)

    @pl.when(j == 0)
    def _():
        ba = bat_ref[...].T
        ba_ref[...] = jnp.concatenate([ba, jnp.zeros((ba.shape[0], LANES - 2 * GDN_HEADS), f32)], axis=1).astype(bf16)


def _w_in_prep(w_in):
    k, n = w_in.shape
    i0 = GDN_CONV_DIM + GDN_V
    n_gate = 2 * GDN_HEADS
    assert RET_DK == W_PREP_TILE and i0 % W_PREP_TILE == 0 and n == MAIN_WIDTH + n_gate and i0 % n_gate == 0
    wt = jnp.swapaxes(w_in, 0, 1)
    src_row = lambda j: (pl.multiple_of(j * W_PREP_TILE + n_gate * (j * W_PREP_TILE >= i0).astype(jnp.int32), SUBLANES), 0)
    return pl.pallas_call(
        _w_in_prep_body,
        grid=(MAIN_WIDTH // W_PREP_TILE,),
        in_specs=[pl.BlockSpec((pl.Element(W_PREP_TILE), pl.Element(k)), src_row),
                  pl.BlockSpec((n_gate, k), lambda j: (i0 // n_gate, 0))],
        out_specs=[pl.BlockSpec((k, W_PREP_TILE), lambda j: (0, j)), pl.BlockSpec((k, LANES), lambda j: (0, 0))],
        out_shape=[jax.ShapeDtypeStruct((k, MAIN_WIDTH), bf16), jax.ShapeDtypeStruct((k, LANES), bf16)],
        compiler_params=_params(("arbitrary",)),
        name="w_in_prep",
    )(wt, wt)


def _prep_weights(norm_mix, w_in, conv_w, gdn_a_log, gdn_dt_bias, gdn_norm, ret_norm, w_out,
                  norm_cross, w_mem_q, w_mem_o, norm_ffn, w_gate, w_up, w_down, norm_final):
    w_main, w_ba = _w_in_prep(w_in.astype(f32))
    gparams = jnp.zeros((SUBLANES, LANES), f32)
    gparams = gparams.at[0, GDN_HEADS:2 * GDN_HEADS].set(gdn_a_log.astype(f32))
    gparams = gparams.at[1, GDN_HEADS:2 * GDN_HEADS].set(gdn_dt_bias.astype(f32))
    return {
        "norm_mix": norm_mix, "w_in": w_main, "w_in_ba": w_ba,
        "conv_w_t": conv_w.astype(f32).T, "gparams": gparams,
        "gdn_norm": gdn_norm, "ret_norm": ret_norm, "w_out": w_out,
        "norm_cross": norm_cross, "w_mem_q": w_mem_q, "w_mem_o": w_mem_o,
        "norm_ffn": norm_ffn, "w_gate": w_gate, "w_up": w_up, "w_down": w_down, "norm_final": norm_final,
    }


def kernel(x_prompt, x_sample, mem_prompt, cache_mem_k, cache_mem_v, state_gdn, state_gdn_conv, state_ret,
           norm_mix, w_in, conv_w, gdn_a_log, gdn_dt_bias, gdn_norm, ret_norm, w_out,
           norm_mem_in, norm_cross, w_mem_q, w_mem_k, w_mem_v, w_mem_o,
           norm_ffn, w_gate, w_up, w_down, norm_final):
    depth = w_in.shape[0]
    assert depth == 1
    batch, seq, _ = x_prompt.shape
    dec_batch, dec_seq, _ = x_sample.shape
    width = MEM_HEADS * MEM_HEAD_DIM
    wts = _prep_weights(norm_mix[0], w_in[0], conv_w[0], gdn_a_log[0], gdn_dt_bias[0], gdn_norm[0], ret_norm[0],
                        w_out[0], norm_cross[0], w_mem_q[0], w_mem_o[0], norm_ffn[0], w_gate[0], w_up[0], w_down[0],
                        norm_final)

    mem2 = mem_prompt.reshape(batch * N_MEM, D_MODEL)
    mk = _norm_matmul(mem2, norm_mem_in[0], w_mem_k[0], name="mem_k")
    mv = _norm_matmul(mem2, norm_mem_in[0], w_mem_v[0], name="mem_v")
    mk4 = mk.reshape(batch, N_MEM, MEM_HEADS, MEM_HEAD_DIM)
    mv4 = mv.reshape(batch, N_MEM, MEM_HEADS, MEM_HEAD_DIM)
    y_p, conv_p, sg_p, sr_p = _layer(
        x_prompt, 0.0,
        jnp.zeros((batch, CONV_WIDTH - 1, GDN_CONV_DIM), f32),
        jnp.zeros((batch, GDN_HEADS, GDN_DK, GDN_DV), f32),
        jnp.zeros((batch, RET_HEADS, RET_DK, RET_DV), f32),
        mk.reshape(batch, N_MEM, width), mv.reshape(batch, N_MEM, width), wts,
        gdn_c=min(GDN_CHUNK, seq), gdn_ns=min(2, batch), ret_c=min(RET_CHUNK, seq), ret_ns=1,
        attn_tq=min(512, seq), attn_ns=1)

    y_s, conv_s, sg_s, sr_s = _layer(
        x_sample, float(PAST_LEN), state_gdn_conv[0], state_gdn[0], state_ret[0],
        cache_mem_k[0], cache_mem_v[0], wts,
        gdn_c=dec_seq, gdn_ns=min(2 * GDN_CHUNK // dec_seq, dec_batch),
        ret_c=dec_seq, ret_ns=min(8, dec_batch), attn_tq=dec_seq, attn_ns=min(4, dec_batch))

    pdt = x_prompt.dtype
    sdt = x_sample.dtype
    return (y_p.astype(pdt), y_s.astype(sdt),
            sg_p.astype(pdt)[None], conv_p.astype(pdt)[None], sr_p.astype(pdt)[None],
            mk4[None], mv4[None],
            sg_s.astype(sdt)[None], conv_s.astype(sdt)[None], sr_s.astype(sdt)[None])
```

```python
import functools
import math

import jax
import jax.numpy as jnp
from jax import lax
from jax.experimental import pallas as pl
from jax.experimental.pallas import tpu as pltpu

f32 = jnp.float32
bf16 = jnp.bfloat16

D_MODEL = 2048
GDN_HEADS = 8
GDN_DK = 128
GDN_DV = 128
RET_HEADS = 4
RET_DK = 256
RET_DV = 256
CONV_WIDTH = 4
N_MEM = 256
MEM_HEADS = 4
MEM_HEAD_DIM = D_MODEL // MEM_HEADS
D_FF = -(-8 * D_MODEL // (3 * 256)) * 256
EPS = 1e-6
ROPE_BASE = 10000.0
PAST_LEN = 16384

GDN_QK = GDN_HEADS * GDN_DK
GDN_V = GDN_HEADS * GDN_DV
GDN_CONV_DIM = 2 * GDN_QK + GDN_V
RET_QK = RET_HEADS * RET_DK
RET_V = RET_HEADS * RET_DV

LANES = 128
SUBLANES = 8
VMEM_LIMIT = 56 * 1024 * 1024

GDN_CHUNK = 64
RET_CHUNK = 256

COL_Z = GDN_CONV_DIM // GDN_V
COL_RQ = COL_Z + 1
COL_RK = COL_Z + 2
COL_RV = COL_Z + 3
COL_RG = COL_Z + 4
MAIN_WIDTH = GDN_CONV_DIM + GDN_V + 2 * RET_QK + 2 * RET_V


def _params(sem):
    return pltpu.CompilerParams(dimension_semantics=sem, vmem_limit_bytes=VMEM_LIMIT)


def _dot(a, b):
    return jnp.dot(a, b, preferred_element_type=f32)


def _dot_nt(a, b):
    return lax.dot_general(a, b, (((1,), (1,)), ((), ())), preferred_element_type=f32)


def _split2(a):
    hi = a.astype(bf16)
    lo = (a - hi.astype(f32)).astype(bf16)
    return hi, lo


def _dot3(a, b):
    ah, al = _split2(a)
    bh, bl = _split2(b)
    return _dot(ah, bh) + (_dot(ah, bl) + _dot(al, bh))


def _dot_exact_lhs(a_bf16, b):
    b1 = b.astype(bf16)
    r1 = b - b1.astype(f32)
    b2 = r1.astype(bf16)
    b3 = (r1 - b2.astype(f32)).astype(bf16)
    return _dot(a_bf16, b1) + (_dot(a_bf16, b2) + _dot(a_bf16, b3))


def _rms(x, g):
    return x * lax.rsqrt(jnp.mean(x * x, axis=-1, keepdims=True) + EPS) * g


def _nm_body(*refs, norm, residual):
    if residual:
        x_ref, g_ref, w_ref, r_ref, o_ref, xn_ref = refs
    else:
        x_ref, g_ref, w_ref, o_ref, xn_ref = refs

    @pl.when(pl.program_id(1) == 0)
    def _():
        x = x_ref[...].astype(f32)
        if norm:
            x = _rms(x, g_ref[...])
        xn_ref[...] = x.astype(bf16)

    acc = _dot(xn_ref[...], w_ref[...].astype(bf16))
    if residual:
        acc = r_ref[...] + acc
    o_ref[...] = acc.astype(o_ref.dtype)


def _norm_matmul(x, gain, w, *, name, residual=None, norm=True, tm=1024, tn=1024, out_dtype=f32):
    m, k = x.shape
    n = w.shape[1]
    tm = min(tm, m)
    tn = min(tn, n)
    assert m % tm == 0 and n % tn == 0
    in_specs = [
        pl.BlockSpec((tm, k), lambda i, j: (i, 0)),
        pl.BlockSpec((1, k), lambda i, j: (0, 0)),
        pl.BlockSpec((k, tn), lambda i, j: (0, j)),
    ]
    args = [x, gain.reshape(1, k).astype(f32), w]
    if residual is not None:
        in_specs.append(pl.BlockSpec((tm, tn), lambda i, j: (i, j)))
        args.append(residual)
    return pl.pallas_call(
        functools.partial(_nm_body, norm=norm, residual=residual is not None),
        grid=(m // tm, n // tn),
        in_specs=in_specs,
        out_specs=pl.BlockSpec((tm, tn), lambda i, j: (i, j)),
        out_shape=jax.ShapeDtypeStruct((m, n), out_dtype),
        scratch_shapes=[pltpu.VMEM((tm, k), bf16)],
        compiler_params=_params(("parallel", "arbitrary")),
        name=name,
    )(*args)


def _in_proj_body(x_ref, g_ref, w_ref, wba_ref, o_ref, ba_ref, xn_ref):
    @pl.when(pl.program_id(1) == 0)
    def _():
        xn = _rms(x_ref[...], g_ref[...]).astype(bf16)
        xn_ref[...] = xn
        ba_ref[...] = _dot(xn, wba_ref[...])

    o_ref[...] = _dot(xn_ref[...], w_ref[...]).astype(o_ref.dtype)


def _in_proj(x, gain, w_main, w_ba, *, out_dtype, tm=1024, tn=2048):
    m, k = x.shape
    tm = min(tm, m)
    tn = tn * 2 // jnp.dtype(out_dtype).itemsize
    assert m % tm == 0 and MAIN_WIDTH % tn == 0
    return pl.pallas_call(
        _in_proj_body,
        grid=(m // tm, MAIN_WIDTH // tn),
        in_specs=[
            pl.BlockSpec((tm, k), lambda i, j: (i, 0)),
            pl.BlockSpec((1, k), lambda i, j: (0, 0)),
            pl.BlockSpec((k, tn), lambda i, j: (0, j)),
            pl.BlockSpec((k, LANES), lambda i, j: (0, 0)),
        ],
        out_specs=[pl.BlockSpec((tm, tn), lambda i, j: (i, j)), pl.BlockSpec((tm, LANES), lambda i, j: (i, 0))],
        out_shape=[jax.ShapeDtypeStruct((m, MAIN_WIDTH), out_dtype), jax.ShapeDtypeStruct((m, LANES), f32)],
        scratch_shapes=[pltpu.VMEM((tm, k), bf16)],
        compiler_params=_params(("parallel", "arbitrary")),
        name="in_proj",
    )(x, gain.reshape(1, k).astype(f32), w_main, w_ba)


def _gdn_chunk_body(u_ref, ba_ref, hp_ref, cw_ref, gp_ref, z_ref, gn_ref, s0_ref, o_ref, s_ref, carry_ref, *, c, ns):
    @pl.when(pl.program_id(1) == 0)
    def _():
        s_ref[...] = s0_ref[...]
        carry_ref[...] = hp_ref[...]

    r = ns * c
    nfac = int(round(math.log2(c)))
    assert 2 ** nfac == c
    heads = range(GDN_HEADS)
    u = u_ref[...].astype(f32)
    cdim = u.shape[-1]
    ext = jnp.concatenate([carry_ref[...], u], axis=1)
    carry_ref[...] = u[:, c - SUBLANES:, :]
    ext2 = ext.reshape(ns * (SUBLANES + c), cdim)
    cw = cw_ref[...]
    acc = ext2 * cw[CONV_WIDTH - 1:CONV_WIDTH, :]
    for sft in range(1, CONV_WIDTH):
        acc = acc + pltpu.roll(ext2, sft, axis=0) * cw[CONV_WIDTH - 1 - sft:CONV_WIDTH - sft, :]
    act = jax.nn.silu(acc.reshape(ns, SUBLANES + c, cdim)[:, SUBLANES:, :].reshape(r, cdim))
    ba = ba_ref[...].reshape(r, LANES)
    gp = gp_ref[...]
    log_decay = -jnp.exp(gp[0:1, :]) * jax.nn.softplus(ba + gp[1:2, :])
    lane = lax.broadcasted_iota(jnp.int32, ba.shape, 1)
    gates = jnp.where(lane < GDN_HEADS, jax.nn.sigmoid(ba), log_decay)
    row = lax.broadcasted_iota(jnp.int32, (r, r), 0)
    col = lax.broadcasted_iota(jnp.int32, (r, r), 1)
    same = (row // c) == (col // c)
    causal = same & (row >= col)
    strict = same & (row > col)
    g_cum = _dot_exact_lhs(causal.astype(bf16), gates)
    g_cum_t = g_cum.T
    seq_of_row = lax.broadcasted_iota(jnp.int32, (r, 1), 0) // c
    last = ((row // c) == (col // c)) & ((col % c) == c - 1)
    g_last_rows = _dot_exact_lhs(last.astype(bf16), g_cum)

    q, k, v, beta, gc, eg, decay, qk, m = ({} for _ in range(9))
    for h in heads:
        sl = slice(h * GDN_DK, (h + 1) * GDN_DK)
        qh = act[:, sl]
        kh = act[:, GDN_QK + h * GDN_DK:GDN_QK + (h + 1) * GDN_DK]
        q[h] = qh * lax.rsqrt(jnp.sum(qh * qh, axis=-1, keepdims=True) + EPS) * (GDN_DK ** -0.5)
        k[h] = kh * lax.rsqrt(jnp.sum(kh * kh, axis=-1, keepdims=True) + EPS)
        v[h] = act[:, 2 * GDN_QK + h * GDN_DV:2 * GDN_QK + (h + 1) * GDN_DV]
        beta[h] = gates[:, h:h + 1]
        gc[h] = g_cum[:, GDN_HEADS + h:GDN_HEADS + h + 1]
        gr = g_cum_t[GDN_HEADS + h:GDN_HEADS + h + 1, :]
        decay[h] = jnp.exp(jnp.where(causal, gc[h] - gr, -jnp.inf))
        kb = k[h].astype(bf16)
        prod = _dot_nt(jnp.concatenate([q[h], k[h]], axis=0).astype(bf16), kb)
        qk[h] = prod[:r] * decay[h]
        m[h] = jnp.where(strict, -(beta[h] * prod[r:] * decay[h]), 0.0)
    p = {h: jnp.where(row == col, 1.0, m[h]) for h in heads}
    for _ in range(nfac - 1):
        for h in heads:
            mb = m[h].astype(bf16)
            m[h] = _dot(mb, mb)
        for h in heads:
            p[h] = p[h] + _dot(p[h].astype(bf16), m[h].astype(bf16))
    u_part, w_part, q_dec = {}, {}, {}
    for h in heads:
        eg[h] = jnp.exp(gc[h])
        rhs = jnp.concatenate([v[h] * beta[h], k[h] * (beta[h] * eg[h])], axis=1)
        sol = _dot(p[h].astype(bf16), rhs.astype(bf16))
        u_part[h] = sol[:, :GDN_DV]
        w_part[h] = sol[:, GDN_DV:]
        q_dec[h] = q[h] * eg[h]
    u, o_state, s_old = {}, {}, {}
    for h in heads:
        us, os_ = [], []
        for j in range(ns):
            rows = slice(j * c, (j + 1) * c)
            s_old[j, h] = s_ref[j, h]
            wq = jnp.concatenate([w_part[h][rows], q_dec[h][rows]], axis=0).astype(bf16)
            wqs = _dot(wq, s_old[j, h].astype(bf16))
            us.append(u_part[h][rows] - wqs[:c])
            os_.append(wqs[c:])
        u[h] = us[0] if ns == 1 else jnp.concatenate(us, axis=0)
        o_state[h] = os_[0] if ns == 1 else jnp.concatenate(os_, axis=0)
    outs, s_new = {}, {}
    for h in heads:
        ub = u[h].astype(bf16)
        outs[h] = o_state[h] + _dot(qk[h].astype(bf16), ub)
        gl = g_last_rows[:, GDN_HEADS + h:GDN_HEADS + h + 1]
        kd = (k[h] * jnp.exp(gl - gc[h])).astype(bf16)
        for j in range(ns):
            uj = ub if ns == 1 else jnp.where(seq_of_row == j, u[h], 0.0).astype(bf16)
            s_new[j, h] = s_old[j, h] * jnp.exp(gl[j * c:j * c + 1, :]) + lax.dot_general(
                kd, uj, (((0,), (0,)), ((), ())), preferred_element_type=f32)
    gn = gn_ref[...]
    for h in heads:
        sl = slice(h * GDN_DV, (h + 1) * GDN_DV)
        y = _rms(outs[h], gn) * jax.nn.silu(z_ref[:, :, sl].astype(f32).reshape(r, GDN_DV))
        o_ref[:, :, sl] = y.reshape(ns, c, GDN_DV).astype(o_ref.dtype)
        for j in range(ns):
            s_ref[j, h] = s_new[j, h]


def _gdn_chunk(proj, ba, hp, conv_w_t, gparams, gdn_norm, s0, *, nseq, c, ns):
    t = proj.shape[0]
    l = t // nseq
    nc = l // c
    assert nseq % ns == 0 and l % c == 0
    tok_spec = lambda width, col: pl.BlockSpec((ns, c, width), lambda b, n: (b, n, col))
    st_spec = pl.BlockSpec((ns, GDN_HEADS, GDN_DK, GDN_DV), lambda b, n: (b, 0, 0, 0))
    proj3 = proj.reshape(nseq, l, MAIN_WIDTH)
    out, s_new = pl.pallas_call(
        functools.partial(_gdn_chunk_body, c=c, ns=ns),
        grid=(nseq // ns, nc),
        in_specs=[
            tok_spec(GDN_CONV_DIM, 0),
            tok_spec(LANES, 0),
            pl.BlockSpec((ns, SUBLANES, GDN_CONV_DIM), lambda b, n: (b, 0, 0)),
            pl.BlockSpec((CONV_WIDTH, GDN_CONV_DIM), lambda b, n: (0, 0)),
            pl.BlockSpec((SUBLANES, LANES), lambda b, n: (0, 0)),
            tok_spec(GDN_V, COL_Z),
            pl.BlockSpec((1, GDN_DV), lambda b, n: (0, 0)),
            st_spec,
        ],
        out_specs=[tok_spec(GDN_V, 0), st_spec],
        out_shape=[
            jax.ShapeDtypeStruct((nseq, l, GDN_V), proj.dtype),
            jax.ShapeDtypeStruct((nseq, GDN_HEADS, GDN_DK, GDN_DV), f32),
        ],
        scratch_shapes=[pltpu.VMEM((ns, SUBLANES, GDN_CONV_DIM), f32)],
        compiler_params=_params(("parallel", "arbitrary")),
        name="gdn_chunk",
    )(proj3, ba.reshape(nseq, l, LANES), hp, conv_w_t, gparams, proj3, gdn_norm.reshape(1, GDN_DV).astype(f32), s0)
    return out.reshape(t, GDN_V), s_new


def _ret_chunk_body(q_ref, k_ref, v_ref, g_ref, cos_ref, sin_ref, rn_ref, s0_ref, o_ref, s_ref, *, c, ns):
    @pl.when(pl.program_id(1) == 0)
    def _():
        s_ref[...] = s0_ref[...]

    half = RET_DK // 2
    cos = cos_ref[...]
    sin = sin_ref[...]
    row = lax.broadcasted_iota(jnp.int32, (c, c), 0)
    col = lax.broadcasted_iota(jnp.int32, (c, c), 1)
    causal = row >= col
    dist = (row - col).astype(f32)
    tpos = lax.broadcasted_iota(jnp.int32, (c, 1), 0).astype(f32)
    src = lax.broadcasted_iota(jnp.int32, (RET_DK, RET_DK), 0)
    dst = lax.broadcasted_iota(jnp.int32, (RET_DK, RET_DK), 1)
    orig = jnp.where(src < half, 2 * src, 2 * (src - half) + 1)
    unperm = (orig == dst).astype(bf16)

    def rot(x):
        xe, xo = x[:, :half], x[:, half:]
        return jnp.concatenate([xe * cos - xo * sin, xo * cos + xe * sin], axis=1)

    heads = range(RET_HEADS)
    lgs = [math.log1p(-(2.0 ** (-5.0 - h))) for h in heads]
    qk, vb, dec = {}, {}, []
    for j in range(ns):
        for h in heads:
            sl = slice(h * RET_DK, (h + 1) * RET_DK)
            q = rot(q_ref[j, :, sl].astype(f32))
            k = rot(k_ref[j, :, sl].astype(f32)) * (RET_DK ** -0.5)
            vb[j, h] = v_ref[j, :, sl].astype(bf16)
            decay = jnp.exp(jnp.where(causal, dist * lgs[h], -jnp.inf))
            qk[j, h] = (_dot_nt(q.astype(bf16), k.astype(bf16)) * decay).astype(bf16)
            dec.append(q * jnp.exp((tpos + 1.0) * lgs[h]))
            dec.append(k * jnp.exp((float(c - 1) - tpos) * lgs[h]))
    dec_orig = _dot(jnp.concatenate(dec, axis=0).astype(bf16), unperm)
    rn = rn_ref[...]
    for j in range(ns):
        outs = []
        for h in heads:
            base = 2 * (j * RET_HEADS + h) * c
            q_dec = dec_orig[base:base + c].astype(bf16)
            k_dec = dec_orig[base + c:base + 2 * c].astype(bf16)
            sl = slice(h * RET_DV, (h + 1) * RET_DV)
            s = s_ref[j, h]
            o = _dot(qk[j, h], vb[j, h]) + _dot(q_dec, s.astype(bf16))
            outs.append(_rms(o, rn[:, sl]) * jax.nn.silu(g_ref[j, :, sl].astype(f32)))
            s_ref[j, h] = s * math.exp(c * lgs[h]) + lax.dot_general(
                k_dec, vb[j, h], (((0,), (0,)), ((), ())), preferred_element_type=f32)
        o_ref[j * c:(j + 1) * c, :] = jnp.concatenate(outs, axis=1).astype(o_ref.dtype)


def _ret_chunk(proj3, cos, sin, ret_norm, s0, *, c, ns):
    nseq, l, _ = proj3.shape
    nc = l // c
    assert nseq % ns == 0 and (ns == 1 or nc == 1)
    st_spec = pl.BlockSpec((ns, RET_HEADS, RET_DK, RET_DV), lambda b, n: (b, 0, 0, 0))
    col_spec = lambda col: pl.BlockSpec((ns, c, RET_QK), lambda b, n: (b, n, col))
    tab_spec = pl.BlockSpec((c, RET_DK // 2), lambda b, n: (n, 0))
    return pl.pallas_call(
        functools.partial(_ret_chunk_body, c=c, ns=ns),
        grid=(nseq // ns, nc),
        in_specs=[col_spec(COL_RQ), col_spec(COL_RK), col_spec(COL_RV), col_spec(COL_RG), tab_spec, tab_spec,
                  pl.BlockSpec((1, RET_V), lambda b, n: (0, 0)), st_spec],
        out_specs=[pl.BlockSpec((ns * c, RET_V), lambda b, n: (b * nc + n, 0)), st_spec],
        out_shape=[
            jax.ShapeDtypeStruct((nseq * l, RET_V), proj3.dtype),
            jax.ShapeDtypeStruct((nseq, RET_HEADS, RET_DK, RET_DV), f32),
        ],
        compiler_params=_params(("parallel", "arbitrary")),
        name="ret_chunk",
    )(proj3, proj3, proj3, proj3, cos, sin, ret_norm.reshape(1, RET_V).astype(f32), s0)


def _post_mixer_body(mg_ref, mr_ref, x_ref, wo_ref, g_ref, wq_ref, x1_ref, q_ref):
    wo = wo_ref[...]
    x1 = x_ref[...] + (_dot(mg_ref[...].astype(bf16), wo[:GDN_V]) + _dot(mr_ref[...].astype(bf16), wo[GDN_V:]))
    x1_ref[...] = x1
    q_ref[...] = _dot(_rms(x1, g_ref[...]).astype(bf16), wq_ref[...]).astype(q_ref.dtype)


def _post_mixer(mix_g, mix_r, w_out, x, norm_cross, w_q, *, q_dtype, tm=512):
    t = x.shape[0]
    tm = min(tm, t)
    width = w_q.shape[1]
    assert t % tm == 0
    resident = lambda shape: pl.BlockSpec(shape, lambda i: (0, 0), pipeline_mode=pl.Buffered(1))
    row = lambda w: pl.BlockSpec((tm, w), lambda i: (i, 0))
    return pl.pallas_call(
        _post_mixer_body,
        grid=(t // tm,),
        in_specs=[row(GDN_V), row(RET_V), row(D_MODEL), resident((GDN_V + RET_V, D_MODEL)),
                  pl.BlockSpec((1, D_MODEL), lambda i: (0, 0)), resident((D_MODEL, width))],
        out_specs=[row(D_MODEL), row(width)],
        out_shape=[jax.ShapeDtypeStruct((t, D_MODEL), f32), jax.ShapeDtypeStruct((t, width), q_dtype)],
        compiler_params=_params(("parallel",)),
        name="post_mixer",
    )(mix_g, mix_r, x, w_out, norm_cross.reshape(1, D_MODEL).astype(f32), w_q)


def _attn_body(q_ref, k_ref, v_ref, o_ref):
    for h in range(MEM_HEADS):
        sl = slice(h * MEM_HEAD_DIM, (h + 1) * MEM_HEAD_DIM)
        qh = q_ref[:, sl].astype(bf16)
        kh = k_ref[0, :, sl].astype(bf16)
        vh = v_ref[0, :, sl].astype(bf16)
        s = _dot_nt(qh, kh) * (MEM_HEAD_DIM ** -0.5)
        e = jnp.exp(s - jnp.max(s, axis=-1, keepdims=True))
        p = e / jnp.sum(e, axis=-1, keepdims=True)
        o_ref[:, sl] = _dot(p.astype(bf16), vh).astype(o_ref.dtype)


def _attn_out_body(q_ref, k_ref, v_ref, wo_ref, x_ref, o_ref, om_ref):
    for h in range(MEM_HEADS):
        sl = slice(h * MEM_HEAD_DIM, (h + 1) * MEM_HEAD_DIM)
        qh = q_ref[:, sl].astype(bf16)
        kh = k_ref[0, :, sl].astype(bf16)
        vh = v_ref[0, :, sl].astype(bf16)
        s = _dot_nt(qh, kh) * (MEM_HEAD_DIM ** -0.5)
        e = jnp.exp(s - jnp.max(s, axis=-1, keepdims=True))
        p = e / jnp.sum(e, axis=-1, keepdims=True)
        om_ref[:, sl] = _dot(p.astype(bf16), vh).astype(bf16)
    o_ref[...] = x_ref[...] + _dot(om_ref[...], wo_ref[...].astype(bf16))


def _attention_out(q, mem_k, mem_v, w_o, x, *, l, tq):
    t = q.shape[0]
    nq = l // tq
    width = MEM_HEADS * MEM_HEAD_DIM
    row = lambda w: pl.BlockSpec((tq, w), lambda b, i: (b * nq + i, 0))
    mem_spec = pl.BlockSpec((1, N_MEM, width), lambda b, i: (b, 0, 0))
    return pl.pallas_call(
        _attn_out_body,
        grid=(t // l, nq),
        in_specs=[row(width), mem_spec, mem_spec,
                  pl.BlockSpec((width, D_MODEL), lambda b, i: (0, 0), pipeline_mode=pl.Buffered(1)), row(D_MODEL)],
        out_specs=row(D_MODEL),
        out_shape=jax.ShapeDtypeStruct((t, D_MODEL), f32),
        scratch_shapes=[pltpu.VMEM((tq, width), bf16)],
        compiler_params=_params(("parallel", "arbitrary")),
        name="mem_attn_out",
    )(q, mem_k, mem_v, w_o, x)


def _attn_cache_body(q_ref, k_ref, v_ref, o_ref, *, tq, ns):
    rows = N_MEM * MEM_HEADS
    shape = (MEM_HEADS * tq, rows)
    own_head = (lax.broadcasted_iota(jnp.int32, shape, 0) // tq) == (lax.broadcasted_iota(jnp.int32, shape, 1) % MEM_HEADS)
    for j in range(ns):
        rj = slice(j * tq, (j + 1) * tq)
        k2 = k_ref[j].reshape(rows, MEM_HEAD_DIM).astype(bf16)
        v2 = v_ref[j].reshape(rows, MEM_HEAD_DIM).astype(bf16)
        qs = jnp.concatenate([q_ref[rj, h * MEM_HEAD_DIM:(h + 1) * MEM_HEAD_DIM] for h in range(MEM_HEADS)], axis=0)
        s = _dot_nt(qs.astype(bf16), k2) * (MEM_HEAD_DIM ** -0.5)
        s = jnp.where(own_head, s, -jnp.inf)
        e = jnp.exp(s - jnp.max(s, axis=-1, keepdims=True))
        p = e / jnp.sum(e, axis=-1, keepdims=True)
        o = _dot(p.astype(bf16), v2)
        for h in range(MEM_HEADS):
            o_ref[rj, h * MEM_HEAD_DIM:(h + 1) * MEM_HEAD_DIM] = o[h * tq:(h + 1) * tq].astype(o_ref.dtype)


def _attention(q, mem_k, mem_v, *, l, tq, ns=1):
    t = q.shape[0]
    nq = l // tq
    width = MEM_HEADS * MEM_HEAD_DIM
    if mem_k.ndim == 4:
        assert nq == 1 and (t // l) % ns == 0
        body = functools.partial(_attn_cache_body, tq=tq, ns=ns)
        mem_spec = pl.BlockSpec((ns, N_MEM, MEM_HEADS, MEM_HEAD_DIM), lambda b, i: (b, 0, 0, 0))
    else:
        assert ns == 1
        body = _attn_body
        mem_spec = pl.BlockSpec((1, N_MEM, width), lambda b, i: (b, 0, 0))
    return pl.pallas_call(
        body,
        grid=(t // (l * ns), nq),
        in_specs=[
            pl.BlockSpec((ns * tq, width), lambda b, i: (b * nq + i, 0)),
            mem_spec,
            mem_spec,
        ],
        out_specs=pl.BlockSpec((ns * tq, width), lambda b, i: (b * nq + i, 0)),
        out_shape=jax.ShapeDtypeStruct((t, width), q.dtype),
        compiler_params=_params(("parallel", "arbitrary")),
        name="mem_attn",
    )(q, mem_k, mem_v)


def _ffn_body(x_ref, g_ref, wg_ref, wu_ref, wd_ref, gf_ref, o_ref, xn_ref):
    f = pl.program_id(1)

    @pl.when(f == 0)
    def _():
        x = x_ref[...]
        xn_ref[...] = _rms(x, g_ref[...]).astype(bf16)
        o_ref[...] = x

    xn = xn_ref[...]
    a = _dot(xn, wg_ref[...].astype(bf16))
    u = _dot(xn, wu_ref[...].astype(bf16))
    hidden = (jax.nn.silu(a) * u).astype(bf16)
    o_ref[...] += _dot(hidden, wd_ref[...].astype(bf16))

    @pl.when(f == pl.num_programs(1) - 1)
    def _():
        o_ref[...] = _rms(o_ref[...], gf_ref[...])


def _ffn(x, norm_ffn, w_gate, w_up, w_down, norm_final, *, tm=1024, tf=256):
    t = x.shape[0]
    tm = min(tm, t)
    assert D_FF % tf == 0 and t % tm == 0
    return pl.pallas_call(
        _ffn_body,
        grid=(t // tm, D_FF // tf),
        in_specs=[
            pl.BlockSpec((tm, D_MODEL), lambda i, f: (i, 0)),
            pl.BlockSpec((1, D_MODEL), lambda i, f: (0, 0)),
            pl.BlockSpec((D_MODEL, tf), lambda i, f: (0, f)),
            pl.BlockSpec((D_MODEL, tf), lambda i, f: (0, f)),
            pl.BlockSpec((tf, D_MODEL), lambda i, f: (f, 0)),
            pl.BlockSpec((1, D_MODEL), lambda i, f: (0, 0)),
        ],
        out_specs=pl.BlockSpec((tm, D_MODEL), lambda i, f: (i, 0)),
        out_shape=jax.ShapeDtypeStruct((t, D_MODEL), f32),
        scratch_shapes=[pltpu.VMEM((tm, D_MODEL), bf16)],
        compiler_params=_params(("parallel", "arbitrary")),
        name="ffn",
    )(x, norm_ffn.reshape(1, D_MODEL).astype(f32), w_gate, w_up, w_down, norm_final.reshape(1, D_MODEL).astype(f32))


def _rotary_tables(l, pos0):
    half = RET_DK // 2
    inv = ROPE_BASE ** (-jnp.linspace(0.0, 1.0, half, dtype=f32))
    pos = jnp.arange(l, dtype=f32) + pos0
    ang = pos[:, None] * inv[None, :]
    return jnp.cos(ang), jnp.sin(ang)


def _layer(x3, pos0, conv_hist, s_gdn, s_ret, mem_k, mem_v, wts, *, gdn_c, gdn_ns, ret_c, ret_ns, attn_tq, attn_ns):
    nseq, l, _ = x3.shape
    t = nseq * l
    x = x3.reshape(t, D_MODEL)
    proj_dtype = bf16 if l % (2 * SUBLANES) == 0 else f32
    proj, ba = _in_proj(x, wts["norm_mix"], wts["w_in"], wts["w_in_ba"], out_dtype=proj_dtype)
    proj3 = proj.reshape(nseq, l, MAIN_WIDTH)
    hp = jnp.pad(conv_hist.astype(f32), ((0, 0), (SUBLANES - (CONV_WIDTH - 1), 0), (0, 0)))
    mix_g, s_gdn_new = _gdn_chunk(proj, ba, hp, wts["conv_w_t"], wts["gparams"], wts["gdn_norm"], s_gdn.astype(f32),
                                  nseq=nseq, c=gdn_c, ns=gdn_ns)
    cos, sin = _rotary_tables(l, pos0)
    mix_r, s_ret_new = _ret_chunk(proj3, cos, sin, wts["ret_norm"], s_ret.astype(f32), c=ret_c, ns=ret_ns)
    conv_new = proj3[:, l - (CONV_WIDTH - 1):, :GDN_CONV_DIM].astype(f32)
    act_dtype = bf16 if attn_tq % (2 * SUBLANES) == 0 else f32
    x1, qm = _post_mixer(mix_g, mix_r, wts["w_out"], x, wts["norm_cross"], wts["w_mem_q"], q_dtype=act_dtype)
    if mem_k.ndim == 3:
        x2 = _attention_out(qm, mem_k, mem_v, wts["w_mem_o"], x1, l=l, tq=attn_tq)
    else:
        om = _attention(qm, mem_k, mem_v, l=l, tq=attn_tq, ns=attn_ns)
        x2 = _norm_matmul(om, wts["norm_cross"], wts["w_mem_o"], residual=x1, norm=False, name="mem_o", tn=512)
    y = _ffn(x2, wts["norm_ffn"], wts["w_gate"], wts["w_up"], wts["w_down"], wts["norm_final"])
    return y.reshape(nseq, l, D_MODEL), conv_new, s_gdn_new, s_ret_new


W_PREP_TILE = 256


def _w_in_prep_body(wt_ref, bat_ref, o_ref, ba_ref):
    j = pl.program_id(0)
    first_ret = (GDN_CONV_DIM + GDN_V) // W_PREP_TILE
    is_ret_qk = (j >= first_ret) & (j < first_ret + 2 * RET_QK // W_PREP_TILE)

    @pl.when(is_ret_qk)
    def _():
        half = RET_DK // 2
        src = lax.broadcasted_iota(jnp.int32, (RET_DK, RET_DK), 0)
        dst = lax.broadcasted_iota(jnp.int32, (RET_DK, RET_DK), 1)
        perm = (src == jnp.where(dst < half, 2 * dst, 2 * (dst - half) + 1)).astype(bf16)
        o_ref[...] = _dot(wt_ref[...].T.astype(bf16), perm).astype(bf16)

    @pl.when(jnp.logical_not(is_ret_qk))
    def _():
        o_ref[...] = wt_ref[...].T.astype(bf16)

    @pl.when(j == 0)
    def _():
        ba = bat_ref[...].T
        ba_ref[...] = jnp.concatenate([ba, jnp.zeros((ba.shape[0], LANES - 2 * GDN_HEADS), f32)], axis=1).astype(bf16)


def _w_in_prep(w_in):
    k, n = w_in.shape
    i0 = GDN_CONV_DIM + GDN_V
    n_gate = 2 * GDN_HEADS
    assert RET_DK == W_PREP_TILE and i0 % W_PREP_TILE == 0 and n == MAIN_WIDTH + n_gate and i0 % n_gate == 0
    wt = jnp.swapaxes(w_in, 0, 1)
    src_row = lambda j: (pl.multiple_of(j * W_PREP_TILE + n_gate * (j * W_PREP_TILE >= i0).astype(jnp.int32), SUBLANES), 0)
    return pl.pallas_call(
        _w_in_prep_body,
        grid=(MAIN_WIDTH // W_PREP_TILE,),
        in_specs=[pl.BlockSpec((pl.Element(W_PREP_TILE), pl.Element(k)), src_row),
                  pl.BlockSpec((n_gate, k), lambda j: (i0 // n_gate, 0))],
        out_specs=[pl.BlockSpec((k, W_PREP_TILE), lambda j: (0, j)), pl.BlockSpec((k, LANES), lambda j: (0, 0))],
        out_shape=[jax.ShapeDtypeStruct((k, MAIN_WIDTH), bf16), jax.ShapeDtypeStruct((k, LANES), bf16)],
        compiler_params=_params(("arbitrary",)),
        name="w_in_prep",
    )(wt, wt)


def _prep_weights(norm_mix, w_in, conv_w, gdn_a_log, gdn_dt_bias, gdn_norm, ret_norm, w_out,
                  norm_cross, w_mem_q, w_mem_o, norm_ffn, w_gate, w_up, w_down, norm_final):
    w_main, w_ba = _w_in_prep(w_in.astype(f32))
    gparams = jnp.zeros((SUBLANES, LANES), f32)
    gparams = gparams.at[0, GDN_HEADS:2 * GDN_HEADS].set(gdn_a_log.astype(f32))
    gparams = gparams.at[1, GDN_HEADS:2 * GDN_HEADS].set(gdn_dt_bias.astype(f32))
    return {
        "norm_mix": norm_mix, "w_in": w_main, "w_in_ba": w_ba,
        "conv_w_t": conv_w.astype(f32).T, "gparams": gparams,
        "gdn_norm": gdn_norm, "ret_norm": ret_norm, "w_out": w_out.astype(bf16),
        "norm_cross": norm_cross, "w_mem_q": w_mem_q.astype(bf16), "w_mem_o": w_mem_o.astype(bf16),
        "norm_ffn": norm_ffn, "w_gate": w_gate, "w_up": w_up, "w_down": w_down, "norm_final": norm_final,
    }


def kernel(x_prompt, x_sample, mem_prompt, cache_mem_k, cache_mem_v, state_gdn, state_gdn_conv, state_ret,
           norm_mix, w_in, conv_w, gdn_a_log, gdn_dt_bias, gdn_norm, ret_norm, w_out,
           norm_mem_in, norm_cross, w_mem_q, w_mem_k, w_mem_v, w_mem_o,
           norm_ffn, w_gate, w_up, w_down, norm_final):
    depth = w_in.shape[0]
    assert depth == 1
    batch, seq, _ = x_prompt.shape
    dec_batch, dec_seq, _ = x_sample.shape
    width = MEM_HEADS * MEM_HEAD_DIM
    wts = _prep_weights(norm_mix[0], w_in[0], conv_w[0], gdn_a_log[0], gdn_dt_bias[0], gdn_norm[0], ret_norm[0],
                        w_out[0], norm_cross[0], w_mem_q[0], w_mem_o[0], norm_ffn[0], w_gate[0], w_up[0], w_down[0],
                        norm_final)

    mem2 = mem_prompt.reshape(batch * N_MEM, D_MODEL)
    mk = _norm_matmul(mem2, norm_mem_in[0], w_mem_k[0], name="mem_k")
    mv = _norm_matmul(mem2, norm_mem_in[0], w_mem_v[0], name="mem_v")
    mk4 = mk.reshape(batch, N_MEM, MEM_HEADS, MEM_HEAD_DIM)
    mv4 = mv.reshape(batch, N_MEM, MEM_HEADS, MEM_HEAD_DIM)
    y_p, conv_p, sg_p, sr_p = _layer(
        x_prompt, 0.0,
        jnp.zeros((batch, CONV_WIDTH - 1, GDN_CONV_DIM), f32),
        jnp.zeros((batch, GDN_HEADS, GDN_DK, GDN_DV), f32),
        jnp.zeros((batch, RET_HEADS, RET_DK, RET_DV), f32),
        mk.reshape(batch, N_MEM, width), mv.reshape(batch, N_MEM, width), wts,
        gdn_c=min(GDN_CHUNK, seq), gdn_ns=min(2, batch), ret_c=min(RET_CHUNK, seq), ret_ns=1,
        attn_tq=min(512, seq), attn_ns=1)

    y_s, conv_s, sg_s, sr_s = _layer(
        x_sample, float(PAST_LEN), state_gdn_conv[0], state_gdn[0], state_ret[0],
        cache_mem_k[0], cache_mem_v[0], wts,
        gdn_c=dec_seq, gdn_ns=min(2 * GDN_CHUNK // dec_seq, dec_batch),
        ret_c=dec_seq, ret_ns=min(8, dec_batch), attn_tq=dec_seq, attn_ns=min(4, dec_batch))

    pdt = x_prompt.dtype
    sdt = x_sample.dtype
    return (y_p.astype(pdt), y_s.astype(sdt),
            sg_p.astype(pdt)[None], conv_p.astype(pdt)[None], sr_p.astype(pdt)[None],
            mk4[None], mv4[None],
            sg_s.astype(sdt)[None], conv_s.astype(sdt)[None], sr_s.astype(sdt)[None])
```

```python
import functools
import math

import jax
import jax.numpy as jnp
from jax import lax
from jax.experimental import pallas as pl
from jax.experimental.pallas import tpu as pltpu

f32 = jnp.float32
bf16 = jnp.bfloat16

D_MODEL = 2048
GDN_HEADS = 8
GDN_DK = 128
GDN_DV = 128
RET_HEADS = 4
RET_DK = 256
RET_DV = 256
CONV_WIDTH = 4
N_MEM = 256
MEM_HEADS = 4
MEM_HEAD_DIM = D_MODEL // MEM_HEADS
D_FF = -(-8 * D_MODEL // (3 * 256)) * 256
EPS = 1e-6
ROPE_BASE = 10000.0
PAST_LEN = 16384

GDN_QK = GDN_HEADS * GDN_DK
GDN_V = GDN_HEADS * GDN_DV
GDN_CONV_DIM = 2 * GDN_QK + GDN_V
RET_QK = RET_HEADS * RET_DK
RET_V = RET_HEADS * RET_DV

LANES = 128
SUBLANES = 8
VMEM_LIMIT = 56 * 1024 * 1024

GDN_CHUNK = 64
RET_CHUNK = 256

COL_Z = GDN_CONV_DIM // GDN_V
COL_RQ = COL_Z + 1
COL_RK = COL_Z + 2
COL_RV = COL_Z + 3
COL_RG = COL_Z + 4
MAIN_WIDTH = GDN_CONV_DIM + GDN_V + 2 * RET_QK + 2 * RET_V


def _params(sem):
    return pltpu.CompilerParams(dimension_semantics=sem, vmem_limit_bytes=VMEM_LIMIT)


def _dot(a, b):
    return jnp.dot(a, b, preferred_element_type=f32)


def _dot_nt(a, b):
    return lax.dot_general(a, b, (((1,), (1,)), ((), ())), preferred_element_type=f32)


def _dot_exact_lhs(a_bf16, b):
    b1 = b.astype(bf16)
    r1 = b - b1.astype(f32)
    b2 = r1.astype(bf16)
    b3 = (r1 - b2.astype(f32)).astype(bf16)
    return _dot(a_bf16, b1) + (_dot(a_bf16, b2) + _dot(a_bf16, b3))


def _rms(x, g):
    return x * lax.rsqrt(jnp.mean(x * x, axis=-1, keepdims=True) + EPS) * g


def _nm_body(*refs, norm, residual):
    if residual:
        x_ref, g_ref, w_ref, r_ref, o_ref, xn_ref = refs
    else:
        x_ref, g_ref, w_ref, o_ref, xn_ref = refs

    @pl.when(pl.program_id(1) == 0)
    def _():
        x = x_ref[...].astype(f32)
        if norm:
            x = _rms(x, g_ref[...])
        xn_ref[...] = x.astype(bf16)

    acc = _dot(xn_ref[...], w_ref[...].astype(bf16))
    if residual:
        acc = r_ref[...] + acc
    o_ref[...] = acc.astype(o_ref.dtype)


def _norm_matmul(x, gain, w, *, name, residual=None, norm=True, tm=1024, tn=1024, out_dtype=f32):
    m, k = x.shape
    n = w.shape[1]
    tm = min(tm, m)
    tn = min(tn, n)
    assert m % tm == 0 and n % tn == 0
    in_specs = [
        pl.BlockSpec((tm, k), lambda i, j: (i, 0)),
        pl.BlockSpec((1, k), lambda i, j: (0, 0)),
        pl.BlockSpec((k, tn), lambda i, j: (0, j)),
    ]
    args = [x, gain.reshape(1, k).astype(f32), w]
    if residual is not None:
        in_specs.append(pl.BlockSpec((tm, tn), lambda i, j: (i, j)))
        args.append(residual)
    return pl.pallas_call(
        functools.partial(_nm_body, norm=norm, residual=residual is not None),
        grid=(m // tm, n // tn),
        in_specs=in_specs,
        out_specs=pl.BlockSpec((tm, tn), lambda i, j: (i, j)),
        out_shape=jax.ShapeDtypeStruct((m, n), out_dtype),
        scratch_shapes=[pltpu.VMEM((tm, k), bf16)],
        compiler_params=_params(("parallel", "arbitrary")),
        name=name,
    )(*args)


def _in_proj_body(x_ref, g_ref, w_ref, wba_ref, o_ref, ba_ref, xn_ref):
    @pl.when(pl.program_id(1) == 0)
    def _():
        xn = _rms(x_ref[...], g_ref[...]).astype(bf16)
        xn_ref[...] = xn
        ba_ref[...] = _dot(xn, wba_ref[...])

    o_ref[...] = _dot(xn_ref[...], w_ref[...]).astype(o_ref.dtype)


def _in_proj(x, gain, w_main, w_ba, *, out_dtype, tm=1024, tn=2048):
    m, k = x.shape
    tm = min(tm, m)
    tn = tn * 2 // jnp.dtype(out_dtype).itemsize
    assert m % tm == 0 and MAIN_WIDTH % tn == 0
    return pl.pallas_call(
        _in_proj_body,
        grid=(m // tm, MAIN_WIDTH // tn),
        in_specs=[
            pl.BlockSpec((tm, k), lambda i, j: (i, 0)),
            pl.BlockSpec((1, k), lambda i, j: (0, 0)),
            pl.BlockSpec((k, tn), lambda i, j: (0, j)),
            pl.BlockSpec((k, LANES), lambda i, j: (0, 0)),
        ],
        out_specs=[pl.BlockSpec((tm, tn), lambda i, j: (i, j)), pl.BlockSpec((tm, LANES), lambda i, j: (i, 0))],
        out_shape=[jax.ShapeDtypeStruct((m, MAIN_WIDTH), out_dtype), jax.ShapeDtypeStruct((m, LANES), f32)],
        scratch_shapes=[pltpu.VMEM((tm, k), bf16)],
        compiler_params=_params(("parallel", "arbitrary")),
        name="in_proj",
    )(x, gain.reshape(1, k).astype(f32), w_main, w_ba)


def _gdn_chunk_body(u_ref, ba_ref, hp_ref, cw_ref, gp_ref, z_ref, gn_ref, s0_ref, o_ref, s_ref, carry_ref, *, c, ns):
    @pl.when(pl.program_id(1) == 0)
    def _():
        s_ref[...] = s0_ref[...]
        carry_ref[...] = hp_ref[...]

    r = ns * c
    nfac = int(round(math.log2(c)))
    assert 2 ** nfac == c
    heads = range(GDN_HEADS)
    u = u_ref[...].astype(f32)
    cdim = u.shape[-1]
    ext = jnp.concatenate([carry_ref[...], u], axis=1)
    carry_ref[...] = u[:, c - SUBLANES:, :]
    ext2 = ext.reshape(ns * (SUBLANES + c), cdim)
    cw = cw_ref[...]
    acc = ext2 * cw[CONV_WIDTH - 1:CONV_WIDTH, :]
    for sft in range(1, CONV_WIDTH):
        acc = acc + pltpu.roll(ext2, sft, axis=0) * cw[CONV_WIDTH - 1 - sft:CONV_WIDTH - sft, :]
    act = jax.nn.silu(acc.reshape(ns, SUBLANES + c, cdim)[:, SUBLANES:, :].reshape(r, cdim))
    ba = ba_ref[...].reshape(r, LANES)
    gp = gp_ref[...]
    log_decay = -jnp.exp(gp[0:1, :]) * jax.nn.softplus(ba + gp[1:2, :])
    lane = lax.broadcasted_iota(jnp.int32, ba.shape, 1)
    gates = jnp.where(lane < GDN_HEADS, jax.nn.sigmoid(ba), log_decay)
    row = lax.broadcasted_iota(jnp.int32, (r, r), 0)
    col = lax.broadcasted_iota(jnp.int32, (r, r), 1)
    same = (row // c) == (col // c)
    causal = same & (row >= col)
    strict = same & (row > col)
    g_cum = _dot_exact_lhs(causal.astype(bf16), gates)
    g_cum_t = g_cum.T
    seq_of_row = lax.broadcasted_iota(jnp.int32, (r, 1), 0) // c
    last = ((row // c) == (col // c)) & ((col % c) == c - 1)
    g_last_rows = _dot_exact_lhs(last.astype(bf16), g_cum)

    q, k, v, beta, gc, eg, decay, qk, m = ({} for _ in range(9))
    for h in heads:
        sl = slice(h * GDN_DK, (h + 1) * GDN_DK)
        qh = act[:, sl]
        kh = act[:, GDN_QK + h * GDN_DK:GDN_QK + (h + 1) * GDN_DK]
        q[h] = qh * lax.rsqrt(jnp.sum(qh * qh, axis=-1, keepdims=True) + EPS) * (GDN_DK ** -0.5)
        k[h] = kh * lax.rsqrt(jnp.sum(kh * kh, axis=-1, keepdims=True) + EPS)
        v[h] = act[:, 2 * GDN_QK + h * GDN_DV:2 * GDN_QK + (h + 1) * GDN_DV]
        beta[h] = gates[:, h:h + 1]
        gc[h] = g_cum[:, GDN_HEADS + h:GDN_HEADS + h + 1]
        gr = g_cum_t[GDN_HEADS + h:GDN_HEADS + h + 1, :]
        decay[h] = jnp.exp(jnp.where(causal, gc[h] - gr, -jnp.inf))
        kb = k[h].astype(bf16)
        prod = _dot_nt(jnp.concatenate([q[h], k[h]], axis=0).astype(bf16), kb)
        qk[h] = prod[:r] * decay[h]
        m[h] = jnp.where(strict, -(beta[h] * prod[r:] * decay[h]), 0.0)
    p = {h: jnp.where(row == col, 1.0, m[h]) for h in heads}
    for _ in range(nfac - 1):
        for h in heads:
            mb = m[h].astype(bf16)
            m[h] = _dot(mb, mb)
        for h in heads:
            p[h] = p[h] + _dot(p[h].astype(bf16), m[h].astype(bf16))
    u_part, w_part, q_dec = {}, {}, {}
    for h in heads:
        eg[h] = jnp.exp(gc[h])
        rhs = jnp.concatenate([v[h] * beta[h], k[h] * (beta[h] * eg[h])], axis=1)
        sol = _dot(p[h].astype(bf16), rhs.astype(bf16))
        u_part[h] = sol[:, :GDN_DV]
        w_part[h] = sol[:, GDN_DV:]
        q_dec[h] = q[h] * eg[h]
    u, o_state, s_old = {}, {}, {}
    for h in heads:
        us, os_ = [], []
        for j in range(ns):
            rows = slice(j * c, (j + 1) * c)
            s_old[j, h] = s_ref[j, h]
            wq = jnp.concatenate([w_part[h][rows], q_dec[h][rows]], axis=0).astype(bf16)
            wqs = _dot(wq, s_old[j, h].astype(bf16))
            us.append(u_part[h][rows] - wqs[:c])
            os_.append(wqs[c:])
        u[h] = us[0] if ns == 1 else jnp.concatenate(us, axis=0)
        o_state[h] = os_[0] if ns == 1 else jnp.concatenate(os_, axis=0)
    outs, s_new = {}, {}
    for h in heads:
        ub = u[h].astype(bf16)
        outs[h] = o_state[h] + _dot(qk[h].astype(bf16), ub)
        gl = g_last_rows[:, GDN_HEADS + h:GDN_HEADS + h + 1]
        kd = (k[h] * jnp.exp(gl - gc[h])).astype(bf16)
        for j in range(ns):
            uj = ub if ns == 1 else jnp.where(seq_of_row == j, u[h], 0.0).astype(bf16)
            s_new[j, h] = s_old[j, h] * jnp.exp(gl[j * c:j * c + 1, :]) + lax.dot_general(
                kd, uj, (((0,), (0,)), ((), ())), preferred_element_type=f32)
    gn = gn_ref[...]
    for h in heads:
        sl = slice(h * GDN_DV, (h + 1) * GDN_DV)
        y = _rms(outs[h], gn) * jax.nn.silu(z_ref[:, :, sl].astype(f32).reshape(r, GDN_DV))
        o_ref[:, :, sl] = y.reshape(ns, c, GDN_DV).astype(o_ref.dtype)
        for j in range(ns):
            s_ref[j, h] = s_new[j, h]


def _gdn_chunk(proj, ba, hp, conv_w_t, gparams, gdn_norm, s0, *, nseq, c, ns):
    t = proj.shape[0]
    l = t // nseq
    nc = l // c
    assert nseq % ns == 0 and l % c == 0
    tok_spec = lambda width, col: pl.BlockSpec((ns, c, width), lambda b, n: (b, n, col))
    st_spec = pl.BlockSpec((ns, GDN_HEADS, GDN_DK, GDN_DV), lambda b, n: (b, 0, 0, 0))
    proj3 = proj.reshape(nseq, l, MAIN_WIDTH)
    out, s_new = pl.pallas_call(
        functools.partial(_gdn_chunk_body, c=c, ns=ns),
        grid=(nseq // ns, nc),
        in_specs=[
            tok_spec(GDN_CONV_DIM, 0),
            tok_spec(LANES, 0),
            pl.BlockSpec((ns, SUBLANES, GDN_CONV_DIM), lambda b, n: (b, 0, 0)),
            pl.BlockSpec((CONV_WIDTH, GDN_CONV_DIM), lambda b, n: (0, 0)),
            pl.BlockSpec((SUBLANES, LANES), lambda b, n: (0, 0)),
            tok_spec(GDN_V, COL_Z),
            pl.BlockSpec((1, GDN_DV), lambda b, n: (0, 0)),
            st_spec,
        ],
        out_specs=[tok_spec(GDN_V, 0), st_spec],
        out_shape=[
            jax.ShapeDtypeStruct((nseq, l, GDN_V), proj.dtype),
            jax.ShapeDtypeStruct((nseq, GDN_HEADS, GDN_DK, GDN_DV), f32),
        ],
        scratch_shapes=[pltpu.VMEM((ns, SUBLANES, GDN_CONV_DIM), f32)],
        compiler_params=_params(("parallel", "arbitrary")),
        name="gdn_chunk",
    )(proj3, ba.reshape(nseq, l, LANES), hp, conv_w_t, gparams, proj3, gdn_norm.reshape(1, GDN_DV).astype(f32), s0)
    return out.reshape(t, GDN_V), s_new


def _ret_chunk_body(q_ref, k_ref, v_ref, g_ref, cos_ref, sin_ref, rn_ref, s0_ref, o_ref, s_ref, *, c, ns):
    @pl.when(pl.program_id(1) == 0)
    def _():
        s_ref[...] = s0_ref[...]

    half = RET_DK // 2
    cos = cos_ref[...]
    sin = sin_ref[...]
    row = lax.broadcasted_iota(jnp.int32, (c, c), 0)
    col = lax.broadcasted_iota(jnp.int32, (c, c), 1)
    causal = row >= col
    dist = (row - col).astype(f32)
    tpos = lax.broadcasted_iota(jnp.int32, (c, 1), 0).astype(f32)
    src = lax.broadcasted_iota(jnp.int32, (RET_DK, RET_DK), 0)
    dst = lax.broadcasted_iota(jnp.int32, (RET_DK, RET_DK), 1)
    orig = jnp.where(src < half, 2 * src, 2 * (src - half) + 1)
    unperm = (orig == dst).astype(bf16)

    def rot(x):
        xe, xo = x[:, :half], x[:, half:]
        return jnp.concatenate([xe * cos - xo * sin, xo * cos + xe * sin], axis=1)

    heads = range(RET_HEADS)
    lgs = [math.log1p(-(2.0 ** (-5.0 - h))) for h in heads]
    qk, vb, dec = {}, {}, []
    for j in range(ns):
        for h in heads:
            sl = slice(h * RET_DK, (h + 1) * RET_DK)
            q = rot(q_ref[j, :, sl].astype(f32))
            k = rot(k_ref[j, :, sl].astype(f32)) * (RET_DK ** -0.5)
            vb[j, h] = v_ref[j, :, sl].astype(bf16)
            decay = jnp.exp(jnp.where(causal, dist * lgs[h], -jnp.inf))
            qk[j, h] = (_dot_nt(q.astype(bf16), k.astype(bf16)) * decay).astype(bf16)
            dec.append(q * jnp.exp((tpos + 1.0) * lgs[h]))
            dec.append(k * jnp.exp((float(c - 1) - tpos) * lgs[h]))
    dec_orig = _dot(jnp.concatenate(dec, axis=0).astype(bf16), unperm)
    rn = rn_ref[...]
    for j in range(ns):
        outs = []
        for h in heads:
            base = 2 * (j * RET_HEADS + h) * c
            q_dec = dec_orig[base:base + c].astype(bf16)
            k_dec = dec_orig[base + c:base + 2 * c].astype(bf16)
            sl = slice(h * RET_DV, (h + 1) * RET_DV)
            s = s_ref[j, h]
            o = _dot(qk[j, h], vb[j, h]) + _dot(q_dec, s.astype(bf16))
            outs.append(_rms(o, rn[:, sl]) * jax.nn.silu(g_ref[j, :, sl].astype(f32)))
            s_ref[j, h] = s * math.exp(c * lgs[h]) + lax.dot_general(
                k_dec, vb[j, h], (((0,), (0,)), ((), ())), preferred_element_type=f32)
        o_ref[j] = jnp.concatenate(outs, axis=1).astype(o_ref.dtype)


def _ret_chunk(proj3, cos, sin, ret_norm, s0, *, c, ns):
    nseq, l, _ = proj3.shape
    nc = l // c
    assert nseq % ns == 0 and l % c == 0
    st_spec = pl.BlockSpec((ns, RET_HEADS, RET_DK, RET_DV), lambda b, n: (b, 0, 0, 0))
    col_spec = lambda col: pl.BlockSpec((ns, c, RET_QK), lambda b, n: (b, n, col))
    tab_spec = pl.BlockSpec((c, RET_DK // 2), lambda b, n: (n, 0))
    out, s_new = pl.pallas_call(
        functools.partial(_ret_chunk_body, c=c, ns=ns),
        grid=(nseq // ns, nc),
        in_specs=[col_spec(COL_RQ), col_spec(COL_RK), col_spec(COL_RV), col_spec(COL_RG), tab_spec, tab_spec,
                  pl.BlockSpec((1, RET_V), lambda b, n: (0, 0)), st_spec],
        out_specs=[col_spec(0), st_spec],
        out_shape=[
            jax.ShapeDtypeStruct((nseq, l, RET_V), proj3.dtype),
            jax.ShapeDtypeStruct((nseq, RET_HEADS, RET_DK, RET_DV), f32),
        ],
        compiler_params=_params(("parallel", "arbitrary")),
        name="ret_chunk",
    )(proj3, proj3, proj3, proj3, cos, sin, ret_norm.reshape(1, RET_V).astype(f32), s0)
    return out.reshape(nseq * l, RET_V), s_new


def _post_mixer_body(mg_ref, mr_ref, x_ref, wo_ref, g_ref, wq_ref, x1_ref, q_ref):
    wo = wo_ref[...]
    x1 = x_ref[...] + (_dot(mg_ref[...].astype(bf16), wo[:GDN_V]) + _dot(mr_ref[...].astype(bf16), wo[GDN_V:]))
    x1_ref[...] = x1
    q_ref[...] = _dot(_rms(x1, g_ref[...]).astype(bf16), wq_ref[...]).astype(q_ref.dtype)


def _post_mixer(mix_g, mix_r, w_out, x, norm_cross, w_q, *, q_dtype, tm=512):
    t = x.shape[0]
    tm = min(tm, t)
    width = w_q.shape[1]
    assert t % tm == 0
    resident = lambda shape: pl.BlockSpec(shape, lambda i: (0, 0), pipeline_mode=pl.Buffered(1))
    row = lambda w: pl.BlockSpec((tm, w), lambda i: (i, 0))
    return pl.pallas_call(
        _post_mixer_body,
        grid=(t // tm,),
        in_specs=[row(GDN_V), row(RET_V), row(D_MODEL), resident((GDN_V + RET_V, D_MODEL)),
                  pl.BlockSpec((1, D_MODEL), lambda i: (0, 0)), resident((D_MODEL, width))],
        out_specs=[row(D_MODEL), row(width)],
        out_shape=[jax.ShapeDtypeStruct((t, D_MODEL), f32), jax.ShapeDtypeStruct((t, width), q_dtype)],
        compiler_params=_params(("parallel",)),
        name="post_mixer",
    )(mix_g, mix_r, x, w_out, norm_cross.reshape(1, D_MODEL).astype(f32), w_q)


def _attn_body(q_ref, k_ref, v_ref, o_ref):
    for h in range(MEM_HEADS):
        sl = slice(h * MEM_HEAD_DIM, (h + 1) * MEM_HEAD_DIM)
        qh = q_ref[:, sl].astype(bf16)
        kh = k_ref[0, :, sl].astype(bf16)
        vh = v_ref[0, :, sl].astype(bf16)
        s = _dot_nt(qh, kh) * (MEM_HEAD_DIM ** -0.5)
        e = jnp.exp(s - jnp.max(s, axis=-1, keepdims=True))
        p = e / jnp.sum(e, axis=-1, keepdims=True)
        o_ref[:, sl] = _dot(p.astype(bf16), vh).astype(o_ref.dtype)


def _attn_out_body(q_ref, k_ref, v_ref, wo_ref, x_ref, o_ref, om_ref):
    for h in range(MEM_HEADS):
        sl = slice(h * MEM_HEAD_DIM, (h + 1) * MEM_HEAD_DIM)
        qh = q_ref[:, sl].astype(bf16)
        kh = k_ref[0, :, sl].astype(bf16)
        vh = v_ref[0, :, sl].astype(bf16)
        s = _dot_nt(qh, kh) * (MEM_HEAD_DIM ** -0.5)
        e = jnp.exp(s - jnp.max(s, axis=-1, keepdims=True))
        p = e / jnp.sum(e, axis=-1, keepdims=True)
        om_ref[:, sl] = _dot(p.astype(bf16), vh).astype(bf16)
    o_ref[...] = x_ref[...] + _dot(om_ref[...], wo_ref[...].astype(bf16))


def _attention_out(q, mem_k, mem_v, w_o, x, *, l, tq):
    t = q.shape[0]
    nq = l // tq
    width = MEM_HEADS * MEM_HEAD_DIM
    row = lambda w: pl.BlockSpec((tq, w), lambda b, i: (b * nq + i, 0))
    mem_spec = pl.BlockSpec((1, N_MEM, width), lambda b, i: (b, 0, 0))
    return pl.pallas_call(
        _attn_out_body,
        grid=(t // l, nq),
        in_specs=[row(width), mem_spec, mem_spec,
                  pl.BlockSpec((width, D_MODEL), lambda b, i: (0, 0), pipeline_mode=pl.Buffered(1)), row(D_MODEL)],
        out_specs=row(D_MODEL),
        out_shape=jax.ShapeDtypeStruct((t, D_MODEL), f32),
        scratch_shapes=[pltpu.VMEM((tq, width), bf16)],
        compiler_params=_params(("parallel", "arbitrary")),
        name="mem_attn_out",
    )(q, mem_k, mem_v, w_o, x)


def _attn_cache_body(q_ref, k_ref, v_ref, o_ref, *, tq, ns):
    rows = N_MEM * MEM_HEADS
    shape = (MEM_HEADS * tq, rows)
    own_head = (lax.broadcasted_iota(jnp.int32, shape, 0) // tq) == (lax.broadcasted_iota(jnp.int32, shape, 1) % MEM_HEADS)
    for j in range(ns):
        rj = slice(j * tq, (j + 1) * tq)
        k2 = k_ref[j].reshape(rows, MEM_HEAD_DIM).astype(bf16)
        v2 = v_ref[j].reshape(rows, MEM_HEAD_DIM).astype(bf16)
        qs = jnp.concatenate([q_ref[rj, h * MEM_HEAD_DIM:(h + 1) * MEM_HEAD_DIM] for h in range(MEM_HEADS)], axis=0)
        s = _dot_nt(qs.astype(bf16), k2) * (MEM_HEAD_DIM ** -0.5)
        s = jnp.where(own_head, s, -jnp.inf)
        e = jnp.exp(s - jnp.max(s, axis=-1, keepdims=True))
        p = e / jnp.sum(e, axis=-1, keepdims=True)
        o = _dot(p.astype(bf16), v2)
        for h in range(MEM_HEADS):
            o_ref[rj, h * MEM_HEAD_DIM:(h + 1) * MEM_HEAD_DIM] = o[h * tq:(h + 1) * tq].astype(o_ref.dtype)


def _attention(q, mem_k, mem_v, *, l, tq, ns=1):
    t = q.shape[0]
    nq = l // tq
    width = MEM_HEADS * MEM_HEAD_DIM
    if mem_k.ndim == 4:
        assert nq == 1 and (t // l) % ns == 0
        body = functools.partial(_attn_cache_body, tq=tq, ns=ns)
        mem_spec = pl.BlockSpec((ns, N_MEM, MEM_HEADS, MEM_HEAD_DIM), lambda b, i: (b, 0, 0, 0))
    else:
        assert ns == 1
        body = _attn_body
        mem_spec = pl.BlockSpec((1, N_MEM, width), lambda b, i: (b, 0, 0))
    return pl.pallas_call(
        body,
        grid=(t // (l * ns), nq),
        in_specs=[
            pl.BlockSpec((ns * tq, width), lambda b, i: (b * nq + i, 0)),
            mem_spec,
            mem_spec,
        ],
        out_specs=pl.BlockSpec((ns * tq, width), lambda b, i: (b * nq + i, 0)),
        out_shape=jax.ShapeDtypeStruct((t, width), q.dtype),
        compiler_params=_params(("parallel", "arbitrary")),
        name="mem_attn",
    )(q, mem_k, mem_v)


def _ffn_body(x_ref, g_ref, wg_ref, wu_ref, wd_ref, gf_ref, o_ref, xn_ref):
    f = pl.program_id(1)

    @pl.when(f == 0)
    def _():
        x = x_ref[...]
        xn_ref[...] = _rms(x, g_ref[...]).astype(bf16)
        o_ref[...] = x

    xn = xn_ref[...]
    a = _dot(xn, wg_ref[...].astype(bf16))
    u = _dot(xn, wu_ref[...].astype(bf16))
    hidden = (jax.nn.silu(a) * u).astype(bf16)
    o_ref[...] += _dot(hidden, wd_ref[...].astype(bf16))

    @pl.when(f == pl.num_programs(1) - 1)
    def _():
        o_ref[...] = _rms(o_ref[...], gf_ref[...])


def _ffn(x, norm_ffn, w_gate, w_up, w_down, norm_final, *, tm=1024, tf=256):
    t = x.shape[0]
    tm = min(tm, t)
    assert D_FF % tf == 0 and t % tm == 0
    return pl.pallas_call(
        _ffn_body,
        grid=(t // tm, D_FF // tf),
        in_specs=[
            pl.BlockSpec((tm, D_MODEL), lambda i, f: (i, 0)),
            pl.BlockSpec((1, D_MODEL), lambda i, f: (0, 0)),
            pl.BlockSpec((D_MODEL, tf), lambda i, f: (0, f)),
            pl.BlockSpec((D_MODEL, tf), lambda i, f: (0, f)),
            pl.BlockSpec((tf, D_MODEL), lambda i, f: (f, 0)),
            pl.BlockSpec((1, D_MODEL), lambda i, f: (0, 0)),
        ],
        out_specs=pl.BlockSpec((tm, D_MODEL), lambda i, f: (i, 0)),
        out_shape=jax.ShapeDtypeStruct((t, D_MODEL), f32),
        scratch_shapes=[pltpu.VMEM((tm, D_MODEL), bf16)],
        compiler_params=_params(("parallel", "arbitrary")),
        name="ffn",
    )(x, norm_ffn.reshape(1, D_MODEL).astype(f32), w_gate, w_up, w_down, norm_final.reshape(1, D_MODEL).astype(f32))


def _rotary_tables(l, pos0):
    half = RET_DK // 2
    inv = ROPE_BASE ** (-jnp.linspace(0.0, 1.0, half, dtype=f32))
    pos = jnp.arange(l, dtype=f32) + pos0
    ang = pos[:, None] * inv[None, :]
    return jnp.cos(ang), jnp.sin(ang)


def _layer(x3, pos0, conv_hist, s_gdn, s_ret, mem_k, mem_v, wts, *, gdn_c, gdn_ns, ret_c, ret_ns, attn_tq, attn_ns):
    nseq, l, _ = x3.shape
    t = nseq * l
    x = x3.reshape(t, D_MODEL)
    proj_dtype = bf16 if l % (2 * SUBLANES) == 0 else f32
    proj, ba = _in_proj(x, wts["norm_mix"], wts["w_in"], wts["w_in_ba"], out_dtype=proj_dtype)
    proj3 = proj.reshape(nseq, l, MAIN_WIDTH)
    hp = jnp.pad(conv_hist.astype(f32), ((0, 0), (SUBLANES - (CONV_WIDTH - 1), 0), (0, 0)))
    mix_g, s_gdn_new = _gdn_chunk(proj, ba, hp, wts["conv_w_t"], wts["gparams"], wts["gdn_norm"], s_gdn.astype(f32),
                                  nseq=nseq, c=gdn_c, ns=gdn_ns)
    cos, sin = _rotary_tables(l, pos0)
    mix_r, s_ret_new = _ret_chunk(proj3, cos, sin, wts["ret_norm"], s_ret.astype(f32), c=ret_c, ns=ret_ns)
    conv_new = proj3[:, l - (CONV_WIDTH - 1):, :GDN_CONV_DIM].astype(f32)
    act_dtype = bf16 if attn_tq % (2 * SUBLANES) == 0 else f32
    x1, qm = _post_mixer(mix_g, mix_r, wts["w_out"], x, wts["norm_cross"], wts["w_mem_q"], q_dtype=act_dtype)
    if mem_k.ndim == 3:
        x2 = _attention_out(qm, mem_k, mem_v, wts["w_mem_o"], x1, l=l, tq=attn_tq)
    else:
        om = _attention(qm, mem_k, mem_v, l=l, tq=attn_tq, ns=attn_ns)
        x2 = _norm_matmul(om, wts["norm_cross"], wts["w_mem_o"], residual=x1, norm=False, name="mem_o", tn=512)
    y = _ffn(x2, wts["norm_ffn"], wts["w_gate"], wts["w_up"], wts["w_down"], wts["norm_final"])
    return y.reshape(nseq, l, D_MODEL), conv_new, s_gdn_new, s_ret_new


W_PREP_TILE = 256


def _w_in_prep_body(wt_ref, bat_ref, o_ref, ba_ref):
    j = pl.program_id(0)
    first_ret = (GDN_CONV_DIM + GDN_V) // W_PREP_TILE
    is_ret_qk = (j >= first_ret) & (j < first_ret + 2 * RET_QK // W_PREP_TILE)

    @pl.when(is_ret_qk)
    def _():
        half = RET_DK // 2
        src = lax.broadcasted_iota(jnp.int32, (RET_DK, RET_DK), 0)
        dst = lax.broadcasted_iota(jnp.int32, (RET_DK, RET_DK), 1)
        perm = (src == jnp.where(dst < half, 2 * dst, 2 * (dst - half) + 1)).astype(bf16)
        o_ref[...] = _dot(wt_ref[...].T.astype(bf16), perm).astype(bf16)

    @pl.when(jnp.logical_not(is_ret_qk))
    def _():
        o_ref[...] = wt_ref[...].T.astype(bf16)

    @pl.when(j == 0)
    def _():
        ba = bat_ref[...].T
        ba_ref[...] = jnp.concatenate([ba, jnp.zeros((ba.shape[0], LANES - 2 * GDN_HEADS), f32)], axis=1).astype(bf16)


def _w_in_prep(w_in):
    k, n = w_in.shape
    i0 = GDN_CONV_DIM + GDN_V
    n_gate = 2 * GDN_HEADS
    assert RET_DK == W_PREP_TILE and i0 % W_PREP_TILE == 0 and n == MAIN_WIDTH + n_gate and i0 % n_gate == 0
    wt = jnp.swapaxes(w_in, 0, 1)
    src_row = lambda j: (pl.multiple_of(j * W_PREP_TILE + n_gate * (j * W_PREP_TILE >= i0).astype(jnp.int32), SUBLANES), 0)
    return pl.pallas_call(
        _w_in_prep_body,
        grid=(MAIN_WIDTH // W_PREP_TILE,),
        in_specs=[pl.BlockSpec((pl.Element(W_PREP_TILE), pl.Element(k)), src_row),
                  pl.BlockSpec((n_gate, k), lambda j: (i0 // n_gate, 0))],
        out_specs=[pl.BlockSpec((k, W_PREP_TILE), lambda j: (0, j)), pl.BlockSpec((k, LANES), lambda j: (0, 0))],
        out_shape=[jax.ShapeDtypeStruct((k, MAIN_WIDTH), bf16), jax.ShapeDtypeStruct((k, LANES), bf16)],
        compiler_params=_params(("arbitrary",)),
        name="w_in_prep",
    )(wt, wt)


def _prep_weights(norm_mix, w_in, conv_w, gdn_a_log, gdn_dt_bias, gdn_norm, ret_norm, w_out,
                  norm_cross, w_mem_q, w_mem_o, norm_ffn, w_gate, w_up, w_down, norm_final):
    w_main, w_ba = _w_in_prep(w_in.astype(f32))
    gparams = jnp.zeros((SUBLANES, LANES), f32)
    gparams = gparams.at[0, GDN_HEADS:2 * GDN_HEADS].set(gdn_a_log.astype(f32))
    gparams = gparams.at[1, GDN_HEADS:2 * GDN_HEADS].set(gdn_dt_bias.astype(f32))
    return {
        "norm_mix": norm_mix, "w_in": w_main, "w_in_ba": w_ba,
        "conv_w_t": conv_w.astype(f32).T, "gparams": gparams,
        "gdn_norm": gdn_norm, "ret_norm": ret_norm, "w_out": w_out.astype(bf16),
        "norm_cross": norm_cross, "w_mem_q": w_mem_q.astype(bf16), "w_mem_o": w_mem_o.astype(bf16),
        "norm_ffn": norm_ffn, "w_gate": w_gate, "w_up": w_up, "w_down": w_down, "norm_final": norm_final,
    }


def kernel(x_prompt, x_sample, mem_prompt, cache_mem_k, cache_mem_v, state_gdn, state_gdn_conv, state_ret,
           norm_mix, w_in, conv_w, gdn_a_log, gdn_dt_bias, gdn_norm, ret_norm, w_out,
           norm_mem_in, norm_cross, w_mem_q, w_mem_k, w_mem_v, w_mem_o,
           norm_ffn, w_gate, w_up, w_down, norm_final):
    depth = w_in.shape[0]
    assert depth == 1
    batch, seq, _ = x_prompt.shape
    dec_batch, dec_seq, _ = x_sample.shape
    width = MEM_HEADS * MEM_HEAD_DIM
    wts = _prep_weights(norm_mix[0], w_in[0], conv_w[0], gdn_a_log[0], gdn_dt_bias[0], gdn_norm[0], ret_norm[0],
                        w_out[0], norm_cross[0], w_mem_q[0], w_mem_o[0], norm_ffn[0], w_gate[0], w_up[0], w_down[0],
                        norm_final)

    mem2 = mem_prompt.reshape(batch * N_MEM, D_MODEL)
    mk = _norm_matmul(mem2, norm_mem_in[0], w_mem_k[0], name="mem_k")
    mv = _norm_matmul(mem2, norm_mem_in[0], w_mem_v[0], name="mem_v")
    mk4 = mk.reshape(batch, N_MEM, MEM_HEADS, MEM_HEAD_DIM)
    mv4 = mv.reshape(batch, N_MEM, MEM_HEADS, MEM_HEAD_DIM)
    y_p, conv_p, sg_p, sr_p = _layer(
        x_prompt, 0.0,
        jnp.zeros((batch, CONV_WIDTH - 1, GDN_CONV_DIM), f32),
        jnp.zeros((batch, GDN_HEADS, GDN_DK, GDN_DV), f32),
        jnp.zeros((batch, RET_HEADS, RET_DK, RET_DV), f32),
        mk.reshape(batch, N_MEM, width), mv.reshape(batch, N_MEM, width), wts,
        gdn_c=min(GDN_CHUNK, seq), gdn_ns=min(2, batch), ret_c=min(RET_CHUNK, seq), ret_ns=min(2, batch),
        attn_tq=min(512, seq), attn_ns=1)

    y_s, conv_s, sg_s, sr_s = _layer(
        x_sample, float(PAST_LEN), state_gdn_conv[0], state_gdn[0], state_ret[0],
        cache_mem_k[0], cache_mem_v[0], wts,
        gdn_c=dec_seq, gdn_ns=min(2 * GDN_CHUNK // dec_seq, dec_batch),
        ret_c=dec_seq, ret_ns=min(8, dec_batch), attn_tq=dec_seq, attn_ns=min(4, dec_batch))

    pdt = x_prompt.dtype
    sdt = x_sample.dtype
    return (y_p.astype(pdt), y_s.astype(sdt),
            sg_p.astype(pdt)[None], conv_p.astype(pdt)[None], sr_p.astype(pdt)[None],
            mk4[None], mv4[None],
            sg_s.astype(sdt)[None], conv_s.astype(sdt)[None], sr_s.astype(sdt)[None])
```

```python
import functools
import math

import jax
import jax.numpy as jnp
from jax import lax
from jax.experimental import pallas as pl
from jax.experimental.pallas import tpu as pltpu

f32 = jnp.float32
bf16 = jnp.bfloat16

D_MODEL = 2048
GDN_HEADS = 8
GDN_DK = 128
GDN_DV = 128
RET_HEADS = 4
RET_DK = 256
RET_DV = 256
CONV_WIDTH = 4
N_MEM = 256
MEM_HEADS = 4
MEM_HEAD_DIM = D_MODEL // MEM_HEADS
D_FF = -(-8 * D_MODEL // (3 * 256)) * 256
EPS = 1e-6
ROPE_BASE = 10000.0
PAST_LEN = 16384

GDN_QK = GDN_HEADS * GDN_DK
GDN_V = GDN_HEADS * GDN_DV
GDN_CONV_DIM = 2 * GDN_QK + GDN_V
RET_QK = RET_HEADS * RET_DK
RET_V = RET_HEADS * RET_DV

LANES = 128
SUBLANES = 8
VMEM_LIMIT = 56 * 1024 * 1024

GDN_CHUNK = 64
RET_CHUNK = 256

COL_Z = GDN_CONV_DIM // GDN_V
COL_RQ = COL_Z + 1
COL_RK = COL_Z + 2
COL_RV = COL_Z + 3
COL_RG = COL_Z + 4
MAIN_WIDTH = GDN_CONV_DIM + GDN_V + 2 * RET_QK + 2 * RET_V


def _params(sem):
    return pltpu.CompilerParams(dimension_semantics=sem, vmem_limit_bytes=VMEM_LIMIT)


def _dot(a, b):
    return jnp.dot(a, b, preferred_element_type=f32)


def _dot_nt(a, b):
    return lax.dot_general(a, b, (((1,), (1,)), ((), ())), preferred_element_type=f32)


def _dot_exact_lhs(a_bf16, b):
    b1 = b.astype(bf16)
    r1 = b - b1.astype(f32)
    b2 = r1.astype(bf16)
    b3 = (r1 - b2.astype(f32)).astype(bf16)
    return _dot(a_bf16, b1) + (_dot(a_bf16, b2) + _dot(a_bf16, b3))


def _rms(x, g):
    return x * lax.rsqrt(jnp.mean(x * x, axis=-1, keepdims=True) + EPS) * g


def _nm_body(*refs, norm, residual):
    if residual:
        x_ref, g_ref, w_ref, r_ref, o_ref, xn_ref = refs
    else:
        x_ref, g_ref, w_ref, o_ref, xn_ref = refs

    @pl.when(pl.program_id(1) == 0)
    def _():
        x = x_ref[...].astype(f32)
        if norm:
            x = _rms(x, g_ref[...])
        xn_ref[...] = x.astype(bf16)

    acc = _dot(xn_ref[...], w_ref[...].astype(bf16))
    if residual:
        acc = r_ref[...] + acc
    o_ref[...] = acc.astype(o_ref.dtype)


def _norm_matmul(x, gain, w, *, name, residual=None, norm=True, tm=1024, tn=1024, out_dtype=f32):
    m, k = x.shape
    n = w.shape[1]
    tm = min(tm, m)
    tn = min(tn, n)
    assert m % tm == 0 and n % tn == 0
    in_specs = [
        pl.BlockSpec((tm, k), lambda i, j: (i, 0)),
        pl.BlockSpec((1, k), lambda i, j: (0, 0)),
        pl.BlockSpec((k, tn), lambda i, j: (0, j)),
    ]
    args = [x, gain.reshape(1, k).astype(f32), w]
    if residual is not None:
        in_specs.append(pl.BlockSpec((tm, tn), lambda i, j: (i, j)))
        args.append(residual)
    return pl.pallas_call(
        functools.partial(_nm_body, norm=norm, residual=residual is not None),
        grid=(m // tm, n // tn),
        in_specs=in_specs,
        out_specs=pl.BlockSpec((tm, tn), lambda i, j: (i, j)),
        out_shape=jax.ShapeDtypeStruct((m, n), out_dtype),
        scratch_shapes=[pltpu.VMEM((tm, k), bf16)],
        compiler_params=_params(("parallel", "arbitrary")),
        name=name,
    )(*args)


def _memory_kv_body(x_ref, g_ref, wk_ref, wv_ref, k_ref, v_ref, xn_ref):
    @pl.when(pl.program_id(1) == 0)
    def _():
        xn_ref[...] = _rms(x_ref[...], g_ref[...]).astype(bf16)

    xn = xn_ref[...]
    k_ref[...] = _dot(xn, wk_ref[...].astype(bf16))
    v_ref[...] = _dot(xn, wv_ref[...].astype(bf16))


def _memory_kv(x, gain, w_k, w_v, *, tm=1024, tn=512):
    m, k = x.shape
    n = w_k.shape[1]
    tm = min(tm, m)
    assert m % tm == 0 and n % tn == 0 and w_v.shape == w_k.shape
    w_spec = pl.BlockSpec((k, tn), lambda i, j: (0, j))
    o_spec = pl.BlockSpec((tm, tn), lambda i, j: (i, j))
    return pl.pallas_call(
        _memory_kv_body,
        grid=(m // tm, n // tn),
        in_specs=[pl.BlockSpec((tm, k), lambda i, j: (i, 0)), pl.BlockSpec((1, k), lambda i, j: (0, 0)), w_spec, w_spec],
        out_specs=[o_spec, o_spec],
        out_shape=[jax.ShapeDtypeStruct((m, n), f32), jax.ShapeDtypeStruct((m, n), f32)],
        scratch_shapes=[pltpu.VMEM((tm, k), bf16)],
        compiler_params=_params(("parallel", "arbitrary")),
        name="memory_kv",
    )(x, gain.reshape(1, k).astype(f32), w_k, w_v)


def _in_proj_body(x_ref, g_ref, w_ref, wba_ref, o_ref, ba_ref, xn_ref):
    @pl.when(pl.program_id(1) == 0)
    def _():
        xn = _rms(x_ref[...], g_ref[...]).astype(bf16)
        xn_ref[...] = xn
        ba_ref[...] = _dot(xn, wba_ref[...])

    o_ref[...] = _dot(xn_ref[...], w_ref[...]).astype(o_ref.dtype)


def _in_proj(x, gain, w_main, w_ba, *, out_dtype, tm=1024, tn=2048):
    m, k = x.shape
    tm = min(tm, m)
    tn = tn * 2 // jnp.dtype(out_dtype).itemsize
    assert m % tm == 0 and MAIN_WIDTH % tn == 0
    return pl.pallas_call(
        _in_proj_body,
        grid=(m // tm, MAIN_WIDTH // tn),
        in_specs=[
            pl.BlockSpec((tm, k), lambda i, j: (i, 0)),
            pl.BlockSpec((1, k), lambda i, j: (0, 0)),
            pl.BlockSpec((k, tn), lambda i, j: (0, j)),
            pl.BlockSpec((k, LANES), lambda i, j: (0, 0)),
        ],
        out_specs=[pl.BlockSpec((tm, tn), lambda i, j: (i, j)), pl.BlockSpec((tm, LANES), lambda i, j: (i, 0))],
        out_shape=[jax.ShapeDtypeStruct((m, MAIN_WIDTH), out_dtype), jax.ShapeDtypeStruct((m, LANES), f32)],
        scratch_shapes=[pltpu.VMEM((tm, k), bf16)],
        compiler_params=_params(("parallel", "arbitrary")),
        name="in_proj",
    )(x, gain.reshape(1, k).astype(f32), w_main, w_ba)


def _gdn_chunk_body(u_ref, ba_ref, hp_ref, cw_ref, gp_ref, z_ref, gn_ref, s0_ref, o_ref, s_ref, carry_ref, *, c, ns):
    @pl.when(pl.program_id(1) == 0)
    def _():
        s_ref[...] = s0_ref[...]
        carry_ref[...] = hp_ref[...]

    r = ns * c
    nfac = int(round(math.log2(c)))
    assert 2 ** nfac == c
    heads = range(GDN_HEADS)
    u = u_ref[...].astype(f32)
    cdim = u.shape[-1]
    ext = jnp.concatenate([carry_ref[...], u], axis=1)
    carry_ref[...] = u[:, c - SUBLANES:, :]
    ext2 = ext.reshape(ns * (SUBLANES + c), cdim)
    cw = cw_ref[...]
    acc = ext2 * cw[CONV_WIDTH - 1:CONV_WIDTH, :]
    for sft in range(1, CONV_WIDTH):
        acc = acc + pltpu.roll(ext2, sft, axis=0) * cw[CONV_WIDTH - 1 - sft:CONV_WIDTH - sft, :]
    act = jax.nn.silu(acc.reshape(ns, SUBLANES + c, cdim)[:, SUBLANES:, :].reshape(r, cdim))
    ba = ba_ref[...].reshape(r, LANES)
    gp = gp_ref[...]
    log_decay = -jnp.exp(gp[0:1, :]) * jax.nn.softplus(ba + gp[1:2, :])
    lane = lax.broadcasted_iota(jnp.int32, ba.shape, 1)
    gates = jnp.where(lane < GDN_HEADS, jax.nn.sigmoid(ba), log_decay)
    row = lax.broadcasted_iota(jnp.int32, (r, r), 0)
    col = lax.broadcasted_iota(jnp.int32, (r, r), 1)
    same = (row // c) == (col // c)
    causal = same & (row >= col)
    strict = same & (row > col)
    g_cum = _dot_exact_lhs(causal.astype(bf16), gates)
    g_cum_t = g_cum.T
    seq_of_row = lax.broadcasted_iota(jnp.int32, (r, 1), 0) // c
    last = ((row // c) == (col // c)) & ((col % c) == c - 1)
    g_last_rows = _dot_exact_lhs(last.astype(bf16), g_cum)

    q, k, v, beta, gc, eg, decay, qk, m = ({} for _ in range(9))
    for h in heads:
        sl = slice(h * GDN_DK, (h + 1) * GDN_DK)
        qh = act[:, sl]
        kh = act[:, GDN_QK + h * GDN_DK:GDN_QK + (h + 1) * GDN_DK]
        q[h] = qh * lax.rsqrt(jnp.sum(qh * qh, axis=-1, keepdims=True) + EPS) * (GDN_DK ** -0.5)
        k[h] = kh * lax.rsqrt(jnp.sum(kh * kh, axis=-1, keepdims=True) + EPS)
        v[h] = act[:, 2 * GDN_QK + h * GDN_DV:2 * GDN_QK + (h + 1) * GDN_DV]
        beta[h] = gates[:, h:h + 1]
        gc[h] = g_cum[:, GDN_HEADS + h:GDN_HEADS + h + 1]
        gr = g_cum_t[GDN_HEADS + h:GDN_HEADS + h + 1, :]
        decay[h] = jnp.exp(jnp.where(causal, gc[h] - gr, -jnp.inf))
        kb = k[h].astype(bf16)
        prod = _dot_nt(jnp.concatenate([q[h], k[h]], axis=0).astype(bf16), kb)
        qk[h] = prod[:r] * decay[h]
        m[h] = jnp.where(strict, -(beta[h] * prod[r:] * decay[h]), 0.0)
    p = {h: jnp.where(row == col, 1.0, m[h]) for h in heads}
    for _ in range(nfac - 1):
        for h in heads:
            mb = m[h].astype(bf16)
            m[h] = _dot(mb, mb)
        for h in heads:
            p[h] = p[h] + _dot(p[h].astype(bf16), m[h].astype(bf16))
    u_part, w_part, q_dec = {}, {}, {}
    for h in heads:
        eg[h] = jnp.exp(gc[h])
        rhs = jnp.concatenate([v[h] * beta[h], k[h] * (beta[h] * eg[h])], axis=1)
        sol = _dot(p[h].astype(bf16), rhs.astype(bf16))
        u_part[h] = sol[:, :GDN_DV]
        w_part[h] = sol[:, GDN_DV:]
        q_dec[h] = q[h] * eg[h]
    u, o_state, s_old = {}, {}, {}
    for h in heads:
        us, os_ = [], []
        for j in range(ns):
            rows = slice(j * c, (j + 1) * c)
            s_old[j, h] = s_ref[j, h]
            wq = jnp.concatenate([w_part[h][rows], q_dec[h][rows]], axis=0).astype(bf16)
            wqs = _dot(wq, s_old[j, h].astype(bf16))
            us.append(u_part[h][rows] - wqs[:c])
            os_.append(wqs[c:])
        u[h] = us[0] if ns == 1 else jnp.concatenate(us, axis=0)
        o_state[h] = os_[0] if ns == 1 else jnp.concatenate(os_, axis=0)
    outs, s_new = {}, {}
    for h in heads:
        ub = u[h].astype(bf16)
        outs[h] = o_state[h] + _dot(qk[h].astype(bf16), ub)
        gl = g_last_rows[:, GDN_HEADS + h:GDN_HEADS + h + 1]
        kd = (k[h] * jnp.exp(gl - gc[h])).astype(bf16)
        for j in range(ns):
            uj = ub if ns == 1 else jnp.where(seq_of_row == j, u[h], 0.0).astype(bf16)
            s_new[j, h] = s_old[j, h] * jnp.exp(gl[j * c:j * c + 1, :]) + lax.dot_general(
                kd, uj, (((0,), (0,)), ((), ())), preferred_element_type=f32)
    gn = gn_ref[...]
    for h in heads:
        sl = slice(h * GDN_DV, (h + 1) * GDN_DV)
        y = _rms(outs[h], gn) * jax.nn.silu(z_ref[:, :, sl].astype(f32).reshape(r, GDN_DV))
        o_ref[:, :, sl] = y.reshape(ns, c, GDN_DV).astype(o_ref.dtype)
        for j in range(ns):
            s_ref[j, h] = s_new[j, h]


def _gdn_chunk(proj, ba, hp, conv_w_t, gparams, gdn_norm, s0, *, nseq, c, ns):
    t = proj.shape[0]
    l = t // nseq
    nc = l // c
    assert nseq % ns == 0 and l % c == 0
    tok_spec = lambda width, col: pl.BlockSpec((ns, c, width), lambda b, n: (b, n, col))
    st_spec = pl.BlockSpec((ns, GDN_HEADS, GDN_DK, GDN_DV), lambda b, n: (b, 0, 0, 0))
    proj3 = proj.reshape(nseq, l, MAIN_WIDTH)
    out, s_new = pl.pallas_call(
        functools.partial(_gdn_chunk_body, c=c, ns=ns),
        grid=(nseq // ns, nc),
        in_specs=[
            tok_spec(GDN_CONV_DIM, 0),
            tok_spec(LANES, 0),
            pl.BlockSpec((ns, SUBLANES, GDN_CONV_DIM), lambda b, n: (b, 0, 0)),
            pl.BlockSpec((CONV_WIDTH, GDN_CONV_DIM), lambda b, n: (0, 0)),
            pl.BlockSpec((SUBLANES, LANES), lambda b, n: (0, 0)),
            tok_spec(GDN_V, COL_Z),
            pl.BlockSpec((1, GDN_DV), lambda b, n: (0, 0)),
            st_spec,
        ],
        out_specs=[tok_spec(GDN_V, 0), st_spec],
        out_shape=[
            jax.ShapeDtypeStruct((nseq, l, GDN_V), proj.dtype),
            jax.ShapeDtypeStruct((nseq, GDN_HEADS, GDN_DK, GDN_DV), f32),
        ],
        scratch_shapes=[pltpu.VMEM((ns, SUBLANES, GDN_CONV_DIM), f32)],
        compiler_params=_params(("parallel", "arbitrary")),
        name="gdn_chunk",
    )(proj3, ba.reshape(nseq, l, LANES), hp, conv_w_t, gparams, proj3, gdn_norm.reshape(1, GDN_DV).astype(f32), s0)
    return out.reshape(t, GDN_V), s_new


def _ret_chunk_body(q_ref, k_ref, v_ref, g_ref, cos_ref, sin_ref, rn_ref, s0_ref, o_ref, s_ref, *, c, ns):
    @pl.when(pl.program_id(1) == 0)
    def _():
        s_ref[...] = s0_ref[...]

    half = RET_DK // 2
    cos = cos_ref[...]
    sin = sin_ref[...]
    row = lax.broadcasted_iota(jnp.int32, (c, c), 0)
    col = lax.broadcasted_iota(jnp.int32, (c, c), 1)
    causal = row >= col
    dist = (row - col).astype(f32)
    tpos = lax.broadcasted_iota(jnp.int32, (c, 1), 0).astype(f32)
    src = lax.broadcasted_iota(jnp.int32, (RET_DK, RET_DK), 0)
    dst = lax.broadcasted_iota(jnp.int32, (RET_DK, RET_DK), 1)
    orig = jnp.where(src < half, 2 * src, 2 * (src - half) + 1)
    unperm = (orig == dst).astype(bf16)

    def rot(x):
        xe, xo = x[:, :half], x[:, half:]
        return jnp.concatenate([xe * cos - xo * sin, xo * cos + xe * sin], axis=1)

    heads = range(RET_HEADS)
    lgs = [math.log1p(-(2.0 ** (-5.0 - h))) for h in heads]
    qk, vb, dec = {}, {}, []
    decay = [jnp.exp(jnp.where(causal, dist * lgs[h], -jnp.inf)) for h in heads]
    q_scale = [jnp.exp((tpos + 1.0) * lgs[h]) for h in heads]
    k_scale = [jnp.exp((float(c - 1) - tpos) * lgs[h]) for h in heads]
    for j in range(ns):
        for h in heads:
            sl = slice(h * RET_DK, (h + 1) * RET_DK)
            q = rot(q_ref[j, :, sl].astype(f32))
            k = rot(k_ref[j, :, sl].astype(f32)) * (RET_DK ** -0.5)
            vb[j, h] = v_ref[j, :, sl].astype(bf16)
            qk[j, h] = (_dot_nt(q.astype(bf16), k.astype(bf16)) * decay[h]).astype(bf16)
            dec.append(q * q_scale[h])
            dec.append(k * k_scale[h])
    dec_orig = _dot(jnp.concatenate(dec, axis=0).astype(bf16), unperm)
    rn = rn_ref[...]
    for j in range(ns):
        outs = []
        for h in heads:
            base = 2 * (j * RET_HEADS + h) * c
            q_dec = dec_orig[base:base + c].astype(bf16)
            k_dec = dec_orig[base + c:base + 2 * c].astype(bf16)
            sl = slice(h * RET_DV, (h + 1) * RET_DV)
            s = s_ref[j, h]
            o = _dot(qk[j, h], vb[j, h]) + _dot(q_dec, s.astype(bf16))
            outs.append(_rms(o, rn[:, sl]) * jax.nn.silu(g_ref[j, :, sl].astype(f32)))
            s_ref[j, h] = s * math.exp(c * lgs[h]) + lax.dot_general(
                k_dec, vb[j, h], (((0,), (0,)), ((), ())), preferred_element_type=f32)
        o_ref[j] = jnp.concatenate(outs, axis=1).astype(o_ref.dtype)


def _ret_chunk(proj3, cos, sin, ret_norm, s0, *, c, ns):
    nseq, l, _ = proj3.shape
    nc = l // c
    assert nseq % ns == 0 and l % c == 0
    st_spec = pl.BlockSpec((ns, RET_HEADS, RET_DK, RET_DV), lambda b, n: (b, 0, 0, 0))
    col_spec = lambda col: pl.BlockSpec((ns, c, RET_QK), lambda b, n: (b, n, col))
    tab_spec = pl.BlockSpec((c, RET_DK // 2), lambda b, n: (n, 0))
    out, s_new = pl.pallas_call(
        functools.partial(_ret_chunk_body, c=c, ns=ns),
        grid=(nseq // ns, nc),
        in_specs=[col_spec(COL_RQ), col_spec(COL_RK), col_spec(COL_RV), col_spec(COL_RG), tab_spec, tab_spec,
                  pl.BlockSpec((1, RET_V), lambda b, n: (0, 0)), st_spec],
        out_specs=[col_spec(0), st_spec],
        out_shape=[
            jax.ShapeDtypeStruct((nseq, l, RET_V), proj3.dtype),
            jax.ShapeDtypeStruct((nseq, RET_HEADS, RET_DK, RET_DV), f32),
        ],
        compiler_params=_params(("parallel", "arbitrary")),
        name="ret_chunk",
    )(proj3, proj3, proj3, proj3, cos, sin, ret_norm.reshape(1, RET_V).astype(f32), s0)
    return out.reshape(nseq * l, RET_V), s_new


def _post_mixer_body(mg_ref, mr_ref, x_ref, wo_ref, g_ref, wq_ref, x1_ref, q_ref):
    wo = wo_ref[...]
    x1 = x_ref[...] + (_dot(mg_ref[...].astype(bf16), wo[:GDN_V]) + _dot(mr_ref[...].astype(bf16), wo[GDN_V:]))
    x1_ref[...] = x1
    q_ref[...] = _dot(_rms(x1, g_ref[...]).astype(bf16), wq_ref[...]).astype(q_ref.dtype)


def _post_mixer(mix_g, mix_r, w_out, x, norm_cross, w_q, *, q_dtype, tm=512):
    t = x.shape[0]
    tm = min(tm, t)
    width = w_q.shape[1]
    assert t % tm == 0
    resident = lambda shape: pl.BlockSpec(shape, lambda i: (0, 0), pipeline_mode=pl.Buffered(1))
    row = lambda w: pl.BlockSpec((tm, w), lambda i: (i, 0))
    return pl.pallas_call(
        _post_mixer_body,
        grid=(t // tm,),
        in_specs=[row(GDN_V), row(RET_V), row(D_MODEL), resident((GDN_V + RET_V, D_MODEL)),
                  pl.BlockSpec((1, D_MODEL), lambda i: (0, 0)), resident((D_MODEL, width))],
        out_specs=[row(D_MODEL), row(width)],
        out_shape=[jax.ShapeDtypeStruct((t, D_MODEL), f32), jax.ShapeDtypeStruct((t, width), q_dtype)],
        compiler_params=_params(("parallel",)),
        name="post_mixer",
    )(mix_g, mix_r, x, w_out, norm_cross.reshape(1, D_MODEL).astype(f32), w_q)


def _attn_body(q_ref, k_ref, v_ref, o_ref):
    for h in range(MEM_HEADS):
        sl = slice(h * MEM_HEAD_DIM, (h + 1) * MEM_HEAD_DIM)
        qh = q_ref[:, sl].astype(bf16)
        kh = k_ref[0, :, sl].astype(bf16)
        vh = v_ref[0, :, sl].astype(bf16)
        s = _dot_nt(qh, kh) * (MEM_HEAD_DIM ** -0.5)
        e = jnp.exp(s - jnp.max(s, axis=-1, keepdims=True))
        p = e / jnp.sum(e, axis=-1, keepdims=True)
        o_ref[:, sl] = _dot(p.astype(bf16), vh).astype(o_ref.dtype)


def _attn_out_body(q_ref, k_ref, v_ref, wo_ref, x_ref, o_ref, om_ref):
    for h in range(MEM_HEADS):
        sl = slice(h * MEM_HEAD_DIM, (h + 1) * MEM_HEAD_DIM)
        qh = q_ref[:, sl].astype(bf16)
        kh = k_ref[0, :, sl].astype(bf16)
        vh = v_ref[0, :, sl].astype(bf16)
        s = _dot_nt(qh, kh) * (MEM_HEAD_DIM ** -0.5)
        e = jnp.exp(s - jnp.max(s, axis=-1, keepdims=True))
        p = e / jnp.sum(e, axis=-1, keepdims=True)
        om_ref[:, sl] = _dot(p.astype(bf16), vh).astype(bf16)
    o_ref[...] = x_ref[...] + _dot(om_ref[...], wo_ref[...].astype(bf16))


def _attention_out(q, mem_k, mem_v, w_o, x, *, l, tq):
    t = q.shape[0]
    nq = l // tq
    width = MEM_HEADS * MEM_HEAD_DIM
    row = lambda w: pl.BlockSpec((tq, w), lambda b, i: (b * nq + i, 0))
    mem_spec = pl.BlockSpec((1, N_MEM, width), lambda b, i: (b, 0, 0))
    return pl.pallas_call(
        _attn_out_body,
        grid=(t // l, nq),
        in_specs=[row(width), mem_spec, mem_spec,
                  pl.BlockSpec((width, D_MODEL), lambda b, i: (0, 0), pipeline_mode=pl.Buffered(1)), row(D_MODEL)],
        out_specs=row(D_MODEL),
        out_shape=jax.ShapeDtypeStruct((t, D_MODEL), f32),
        scratch_shapes=[pltpu.VMEM((tq, width), bf16)],
        compiler_params=_params(("parallel", "arbitrary")),
        name="mem_attn_out",
    )(q, mem_k, mem_v, w_o, x)


def _attn_cache_body(q_ref, k_ref, v_ref, o_ref, *, tq, ns):
    rows = N_MEM * MEM_HEADS
    shape = (MEM_HEADS * tq, rows)
    own_head = (lax.broadcasted_iota(jnp.int32, shape, 0) // tq) == (lax.broadcasted_iota(jnp.int32, shape, 1) % MEM_HEADS)
    for j in range(ns):
        rj = slice(j * tq, (j + 1) * tq)
        k2 = k_ref[j].reshape(rows, MEM_HEAD_DIM).astype(bf16)
        v2 = v_ref[j].reshape(rows, MEM_HEAD_DIM).astype(bf16)
        qs = jnp.concatenate([q_ref[rj, h * MEM_HEAD_DIM:(h + 1) * MEM_HEAD_DIM] for h in range(MEM_HEADS)], axis=0)
        s = _dot_nt(qs.astype(bf16), k2) * (MEM_HEAD_DIM ** -0.5)
        s = jnp.where(own_head, s, -jnp.inf)
        e = jnp.exp(s - jnp.max(s, axis=-1, keepdims=True))
        p = e / jnp.sum(e, axis=-1, keepdims=True)
        o = _dot(p.astype(bf16), v2)
        for h in range(MEM_HEADS):
            o_ref[rj, h * MEM_HEAD_DIM:(h + 1) * MEM_HEAD_DIM] = o[h * tq:(h + 1) * tq].astype(o_ref.dtype)


def _attention(q, mem_k, mem_v, *, l, tq, ns=1):
    t = q.shape[0]
    nq = l // tq
    width = MEM_HEADS * MEM_HEAD_DIM
    if mem_k.ndim == 4:
        assert nq == 1 and (t // l) % ns == 0
        body = functools.partial(_attn_cache_body, tq=tq, ns=ns)
        mem_spec = pl.BlockSpec((ns, N_MEM, MEM_HEADS, MEM_HEAD_DIM), lambda b, i: (b, 0, 0, 0))
    else:
        assert ns == 1
        body = _attn_body
        mem_spec = pl.BlockSpec((1, N_MEM, width), lambda b, i: (b, 0, 0))
    return pl.pallas_call(
        body,
        grid=(t // (l * ns), nq),
        in_specs=[
            pl.BlockSpec((ns * tq, width), lambda b, i: (b * nq + i, 0)),
            mem_spec,
            mem_spec,
        ],
        out_specs=pl.BlockSpec((ns * tq, width), lambda b, i: (b * nq + i, 0)),
        out_shape=jax.ShapeDtypeStruct((t, width), q.dtype),
        compiler_params=_params(("parallel", "arbitrary")),
        name="mem_attn",
    )(q, mem_k, mem_v)


def _ffn_body(x_ref, g_ref, wg_ref, wu_ref, wd_ref, gf_ref, o_ref, xn_ref):
    f = pl.program_id(1)

    @pl.when(f == 0)
    def _():
        x = x_ref[...]
        xn_ref[...] = _rms(x, g_ref[...]).astype(bf16)
        o_ref[...] = x

    xn = xn_ref[...]
    a = _dot(xn, wg_ref[...].astype(bf16))
    u = _dot(xn, wu_ref[...].astype(bf16))
    hidden = (jax.nn.silu(a) * u).astype(bf16)
    o_ref[...] += _dot(hidden, wd_ref[...].astype(bf16))

    @pl.when(f == pl.num_programs(1) - 1)
    def _():
        o_ref[...] = _rms(o_ref[...], gf_ref[...])


def _ffn(x, norm_ffn, w_gate, w_up, w_down, norm_final, *, tm=1024, tf=256):
    t = x.shape[0]
    tm = min(tm, t)
    assert D_FF % tf == 0 and t % tm == 0
    return pl.pallas_call(
        _ffn_body,
        grid=(t // tm, D_FF // tf),
        in_specs=[
            pl.BlockSpec((tm, D_MODEL), lambda i, f: (i, 0)),
            pl.BlockSpec((1, D_MODEL), lambda i, f: (0, 0)),
            pl.BlockSpec((D_MODEL, tf), lambda i, f: (0, f)),
            pl.BlockSpec((D_MODEL, tf), lambda i, f: (0, f)),
            pl.BlockSpec((tf, D_MODEL), lambda i, f: (f, 0)),
            pl.BlockSpec((1, D_MODEL), lambda i, f: (0, 0)),
        ],
        out_specs=pl.BlockSpec((tm, D_MODEL), lambda i, f: (i, 0)),
        out_shape=jax.ShapeDtypeStruct((t, D_MODEL), f32),
        scratch_shapes=[pltpu.VMEM((tm, D_MODEL), bf16)],
        compiler_params=_params(("parallel", "arbitrary")),
        name="ffn",
    )(x, norm_ffn.reshape(1, D_MODEL).astype(f32), w_gate, w_up, w_down, norm_final.reshape(1, D_MODEL).astype(f32))


def _rotary_tables(l, pos0):
    half = RET_DK // 2
    inv = ROPE_BASE ** (-jnp.linspace(0.0, 1.0, half, dtype=f32))
    pos = jnp.arange(l, dtype=f32) + pos0
    ang = pos[:, None] * inv[None, :]
    return jnp.cos(ang), jnp.sin(ang)


def _layer(x3, pos0, conv_hist, s_gdn, s_ret, mem_k, mem_v, wts, *, gdn_c, gdn_ns, ret_c, ret_ns, attn_tq, attn_ns):
    nseq, l, _ = x3.shape
    t = nseq * l
    x = x3.reshape(t, D_MODEL)
    proj_dtype = bf16 if l % (2 * SUBLANES) == 0 else f32
    proj, ba = _in_proj(x, wts["norm_mix"], wts["w_in"], wts["w_in_ba"], out_dtype=proj_dtype)
    proj3 = proj.reshape(nseq, l, MAIN_WIDTH)
    hp = jnp.pad(conv_hist.astype(f32), ((0, 0), (SUBLANES - (CONV_WIDTH - 1), 0), (0, 0)))
    mix_g, s_gdn_new = _gdn_chunk(proj, ba, hp, wts["conv_w_t"], wts["gparams"], wts["gdn_norm"], s_gdn.astype(f32),
                                  nseq=nseq, c=gdn_c, ns=gdn_ns)
    cos, sin = _rotary_tables(l, pos0)
    mix_r, s_ret_new = _ret_chunk(proj3, cos, sin, wts["ret_norm"], s_ret.astype(f32), c=ret_c, ns=ret_ns)
    conv_new = proj3[:, l - (CONV_WIDTH - 1):, :GDN_CONV_DIM].astype(f32)
    act_dtype = bf16 if attn_tq % (2 * SUBLANES) == 0 else f32
    x1, qm = _post_mixer(mix_g, mix_r, wts["w_out"], x, wts["norm_cross"], wts["w_mem_q"], q_dtype=act_dtype)
    if mem_k.ndim == 3:
        x2 = _attention_out(qm, mem_k, mem_v, wts["w_mem_o"], x1, l=l, tq=attn_tq)
    else:
        om = _attention(qm, mem_k, mem_v, l=l, tq=attn_tq, ns=attn_ns)
        x2 = _norm_matmul(om, wts["norm_cross"], wts["w_mem_o"], residual=x1, norm=False, name="mem_o", tn=512)
    y = _ffn(x2, wts["norm_ffn"], wts["w_gate"], wts["w_up"], wts["w_down"], wts["norm_final"])
    return y.reshape(nseq, l, D_MODEL), conv_new, s_gdn_new, s_ret_new


W_PREP_TILE = 256


def _w_in_prep_body(wt_ref, bat_ref, o_ref, ba_ref):
    j = pl.program_id(0)
    first_ret = (GDN_CONV_DIM + GDN_V) // W_PREP_TILE
    is_ret_qk = (j >= first_ret) & (j < first_ret + 2 * RET_QK // W_PREP_TILE)

    @pl.when(is_ret_qk)
    def _():
        half = RET_DK // 2
        src = lax.broadcasted_iota(jnp.int32, (RET_DK, RET_DK), 0)
        dst = lax.broadcasted_iota(jnp.int32, (RET_DK, RET_DK), 1)
        perm = (src == jnp.where(dst < half, 2 * dst, 2 * (dst - half) + 1)).astype(bf16)
        o_ref[...] = _dot(wt_ref[...].T.astype(bf16), perm).astype(bf16)

    @pl.when(jnp.logical_not(is_ret_qk))
    def _():
        o_ref[...] = wt_ref[...].T.astype(bf16)

    @pl.when(j == 0)
    def _():
        ba = bat_ref[...].T
        ba_ref[...] = jnp.concatenate([ba, jnp.zeros((ba.shape[0], LANES - 2 * GDN_HEADS), f32)], axis=1).astype(bf16)


def _w_in_prep(w_in):
    k, n = w_in.shape
    i0 = GDN_CONV_DIM + GDN_V
    n_gate = 2 * GDN_HEADS
    assert RET_DK == W_PREP_TILE and i0 % W_PREP_TILE == 0 and n == MAIN_WIDTH + n_gate and i0 % n_gate == 0
    wt = jnp.swapaxes(w_in, 0, 1)
    src_row = lambda j: (pl.multiple_of(j * W_PREP_TILE + n_gate * (j * W_PREP_TILE >= i0).astype(jnp.int32), SUBLANES), 0)
    return pl.pallas_call(
        _w_in_prep_body,
        grid=(MAIN_WIDTH // W_PREP_TILE,),
        in_specs=[pl.BlockSpec((pl.Element(W_PREP_TILE), pl.Element(k)), src_row),
                  pl.BlockSpec((n_gate, k), lambda j: (i0 // n_gate, 0))],
        out_specs=[pl.BlockSpec((k, W_PREP_TILE), lambda j: (0, j)), pl.BlockSpec((k, LANES), lambda j: (0, 0))],
        out_shape=[jax.ShapeDtypeStruct((k, MAIN_WIDTH), bf16), jax.ShapeDtypeStruct((k, LANES), bf16)],
        compiler_params=_params(("arbitrary",)),
        name="w_in_prep",
    )(wt, wt)


def _prep_weights(norm_mix, w_in, conv_w, gdn_a_log, gdn_dt_bias, gdn_norm, ret_norm, w_out,
                  norm_cross, w_mem_q, w_mem_o, norm_ffn, w_gate, w_up, w_down, norm_final):
    w_main, w_ba = _w_in_prep(w_in.astype(f32))
    gparams = jnp.zeros((SUBLANES, LANES), f32)
    gparams = gparams.at[0, GDN_HEADS:2 * GDN_HEADS].set(gdn_a_log.astype(f32))
    gparams = gparams.at[1, GDN_HEADS:2 * GDN_HEADS].set(gdn_dt_bias.astype(f32))
    return {
        "norm_mix": norm_mix, "w_in": w_main, "w_in_ba": w_ba,
        "conv_w_t": conv_w.astype(f32).T, "gparams": gparams,
        "gdn_norm": gdn_norm, "ret_norm": ret_norm, "w_out": w_out.astype(bf16),
        "norm_cross": norm_cross, "w_mem_q": w_mem_q.astype(bf16), "w_mem_o": w_mem_o.astype(bf16),
        "norm_ffn": norm_ffn, "w_gate": w_gate, "w_up": w_up, "w_down": w_down, "norm_final": norm_final,
    }


def kernel(x_prompt, x_sample, mem_prompt, cache_mem_k, cache_mem_v, state_gdn, state_gdn_conv, state_ret,
           norm_mix, w_in, conv_w, gdn_a_log, gdn_dt_bias, gdn_norm, ret_norm, w_out,
           norm_mem_in, norm_cross, w_mem_q, w_mem_k, w_mem_v, w_mem_o,
           norm_ffn, w_gate, w_up, w_down, norm_final):
    depth = w_in.shape[0]
    assert depth == 1
    batch, seq, _ = x_prompt.shape
    dec_batch, dec_seq, _ = x_sample.shape
    width = MEM_HEADS * MEM_HEAD_DIM
    wts = _prep_weights(norm_mix[0], w_in[0], conv_w[0], gdn_a_log[0], gdn_dt_bias[0], gdn_norm[0], ret_norm[0],
                        w_out[0], norm_cross[0], w_mem_q[0], w_mem_o[0], norm_ffn[0], w_gate[0], w_up[0], w_down[0],
                        norm_final)

    mem2 = mem_prompt.reshape(batch * N_MEM, D_MODEL)
    mk, mv = _memory_kv(mem2, norm_mem_in[0], w_mem_k[0], w_mem_v[0])
    mk4 = mk.reshape(batch, N_MEM, MEM_HEADS, MEM_HEAD_DIM)
    mv4 = mv.reshape(batch, N_MEM, MEM_HEADS, MEM_HEAD_DIM)
    y_p, conv_p, sg_p, sr_p = _layer(
        x_prompt, 0.0,
        jnp.zeros((batch, CONV_WIDTH - 1, GDN_CONV_DIM), f32),
        jnp.zeros((batch, GDN_HEADS, GDN_DK, GDN_DV), f32),
        jnp.zeros((batch, RET_HEADS, RET_DK, RET_DV), f32),
        mk.reshape(batch, N_MEM, width), mv.reshape(batch, N_MEM, width), wts,
        gdn_c=min(GDN_CHUNK, seq), gdn_ns=min(2, batch), ret_c=min(RET_CHUNK, seq), ret_ns=min(2, batch),
        attn_tq=min(512, seq), attn_ns=1)

    y_s, conv_s, sg_s, sr_s = _layer(
        x_sample, float(PAST_LEN), state_gdn_conv[0], state_gdn[0], state_ret[0],
        cache_mem_k[0], cache_mem_v[0], wts,
        gdn_c=dec_seq, gdn_ns=min(2 * GDN_CHUNK // dec_seq, dec_batch),
        ret_c=dec_seq, ret_ns=min(8, dec_batch), attn_tq=dec_seq, attn_ns=min(4, dec_batch))

    pdt = x_prompt.dtype
    sdt = x_sample.dtype
    return (y_p.astype(pdt), y_s.astype(sdt),
            sg_p.astype(pdt)[None], conv_p.astype(pdt)[None], sr_p.astype(pdt)[None],
            mk4[None], mv4[None],
            sg_s.astype(sdt)[None], conv_s.astype(sdt)[None], sr_s.astype(sdt)[None])
```

```python
import functools
import math

import jax
import jax.numpy as jnp
from jax import lax
from jax.experimental import pallas as pl
from jax.experimental.pallas import tpu as pltpu

f32 = jnp.float32
bf16 = jnp.bfloat16

D_MODEL = 2048
GDN_HEADS = 8
GDN_DK = 128
GDN_DV = 128
RET_HEADS = 4
RET_DK = 256
RET_DV = 256
CONV_WIDTH = 4
N_MEM = 256
MEM_HEADS = 4
MEM_HEAD_DIM = D_MODEL // MEM_HEADS
D_FF = -(-8 * D_MODEL // (3 * 256)) * 256
EPS = 1e-6
ROPE_BASE = 10000.0
PAST_LEN = 16384

GDN_QK = GDN_HEADS * GDN_DK
GDN_V = GDN_HEADS * GDN_DV
GDN_CONV_DIM = 2 * GDN_QK + GDN_V
RET_QK = RET_HEADS * RET_DK
RET_V = RET_HEADS * RET_DV

LANES = 128
SUBLANES = 8
VMEM_LIMIT = 56 * 1024 * 1024

GDN_CHUNK = 64
RET_CHUNK = 256

COL_Z = GDN_CONV_DIM // GDN_V
COL_RQ = COL_Z + 1
COL_RK = COL_Z + 2
COL_RV = COL_Z + 3
COL_RG = COL_Z + 4
MAIN_WIDTH = GDN_CONV_DIM + GDN_V + 2 * RET_QK + 2 * RET_V


def _params(sem):
    return pltpu.CompilerParams(dimension_semantics=sem, vmem_limit_bytes=VMEM_LIMIT)


def _dot(a, b):
    return jnp.dot(a, b, preferred_element_type=f32)


def _dot_nt(a, b):
    return lax.dot_general(a, b, (((1,), (1,)), ((), ())), preferred_element_type=f32)


def _dot_exact_lhs(a_bf16, b):
    b1 = b.astype(bf16)
    r1 = b - b1.astype(f32)
    b2 = r1.astype(bf16)
    b3 = (r1 - b2.astype(f32)).astype(bf16)
    return _dot(a_bf16, b1) + (_dot(a_bf16, b2) + _dot(a_bf16, b3))


def _rms(x, g):
    return x * lax.rsqrt(jnp.mean(x * x, axis=-1, keepdims=True) + EPS) * g


def _nm_body(*refs, norm, residual):
    if residual:
        x_ref, g_ref, w_ref, r_ref, o_ref, xn_ref = refs
    else:
        x_ref, g_ref, w_ref, o_ref, xn_ref = refs

    @pl.when(pl.program_id(1) == 0)
    def _():
        x = x_ref[...].astype(f32)
        if norm:
            x = _rms(x, g_ref[...])
        xn_ref[...] = x.astype(bf16)

    acc = _dot(xn_ref[...], w_ref[...].astype(bf16))
    if residual:
        acc = r_ref[...] + acc
    o_ref[...] = acc.astype(o_ref.dtype)


def _norm_matmul(x, gain, w, *, name, residual=None, norm=True, tm=1024, tn=1024, out_dtype=f32):
    m, k = x.shape
    n = w.shape[1]
    tm = min(tm, m)
    tn = min(tn, n)
    assert m % tm == 0 and n % tn == 0
    in_specs = [
        pl.BlockSpec((tm, k), lambda i, j: (i, 0)),
        pl.BlockSpec((1, k), lambda i, j: (0, 0)),
        pl.BlockSpec((k, tn), lambda i, j: (0, j)),
    ]
    args = [x, gain.reshape(1, k).astype(f32), w]
    if residual is not None:
        in_specs.append(pl.BlockSpec((tm, tn), lambda i, j: (i, j)))
        args.append(residual)
    return pl.pallas_call(
        functools.partial(_nm_body, norm=norm, residual=residual is not None),
        grid=(m // tm, n // tn),
        in_specs=in_specs,
        out_specs=pl.BlockSpec((tm, tn), lambda i, j: (i, j)),
        out_shape=jax.ShapeDtypeStruct((m, n), out_dtype),
        scratch_shapes=[pltpu.VMEM((tm, k), bf16)],
        compiler_params=_params(("parallel", "arbitrary")),
        name=name,
    )(*args)


def _memory_kv_body(x_ref, g_ref, wk_ref, wv_ref, k_ref, v_ref, xn_ref):
    @pl.when(pl.program_id(1) == 0)
    def _():
        xn_ref[...] = _rms(x_ref[...], g_ref[...]).astype(bf16)

    xn = xn_ref[...]
    k_ref[...] = _dot(xn, wk_ref[...].astype(bf16))
    v_ref[...] = _dot(xn, wv_ref[...].astype(bf16))


def _memory_kv(x, gain, w_k, w_v, *, tm=1024, tn=512):
    m, k = x.shape
    n = w_k.shape[1]
    tm = min(tm, m)
    assert m % tm == 0 and n % tn == 0 and w_v.shape == w_k.shape
    w_spec = pl.BlockSpec((k, tn), lambda i, j: (0, j))
    o_spec = pl.BlockSpec((tm, tn), lambda i, j: (i, j))
    return pl.pallas_call(
        _memory_kv_body,
        grid=(m // tm, n // tn),
        in_specs=[pl.BlockSpec((tm, k), lambda i, j: (i, 0)), pl.BlockSpec((1, k), lambda i, j: (0, 0)), w_spec, w_spec],
        out_specs=[o_spec, o_spec],
        out_shape=[jax.ShapeDtypeStruct((m, n), f32), jax.ShapeDtypeStruct((m, n), f32)],
        scratch_shapes=[pltpu.VMEM((tm, k), bf16)],
        compiler_params=_params(("parallel", "arbitrary")),
        name="memory_kv",
    )(x, gain.reshape(1, k).astype(f32), w_k, w_v)


def _in_proj_body(x_ref, g_ref, w_ref, wba_ref, o_ref, ba_ref, xn_ref):
    @pl.when(pl.program_id(1) == 0)
    def _():
        xn = _rms(x_ref[...], g_ref[...]).astype(bf16)
        xn_ref[...] = xn
        ba_ref[...] = _dot(xn, wba_ref[...])

    o_ref[...] = _dot(xn_ref[...], w_ref[...]).astype(o_ref.dtype)


def _in_proj(x, gain, w_main, w_ba, *, out_dtype, tm=1024, tn=2048):
    m, k = x.shape
    tm = min(tm, m)
    tn = tn * 2 // jnp.dtype(out_dtype).itemsize
    assert m % tm == 0 and MAIN_WIDTH % tn == 0
    return pl.pallas_call(
        _in_proj_body,
        grid=(m // tm, MAIN_WIDTH // tn),
        in_specs=[
            pl.BlockSpec((tm, k), lambda i, j: (i, 0)),
            pl.BlockSpec((1, k), lambda i, j: (0, 0)),
            pl.BlockSpec((k, tn), lambda i, j: (0, j)),
            pl.BlockSpec((k, LANES), lambda i, j: (0, 0)),
        ],
        out_specs=[pl.BlockSpec((tm, tn), lambda i, j: (i, j)), pl.BlockSpec((tm, LANES), lambda i, j: (i, 0))],
        out_shape=[jax.ShapeDtypeStruct((m, MAIN_WIDTH), out_dtype), jax.ShapeDtypeStruct((m, LANES), f32)],
        scratch_shapes=[pltpu.VMEM((tm, k), bf16)],
        compiler_params=_params(("parallel", "arbitrary")),
        name="in_proj",
    )(x, gain.reshape(1, k).astype(f32), w_main, w_ba)


RING_DEPTH = 3


def _state_ring_slot(s_hbm, buf_ref, sem_ref, ns):
    b = pl.program_id(0)
    nb = pl.num_programs(0)

    def copy(step, slot):
        return pltpu.make_async_copy(s_hbm.at[pl.ds(step * ns, ns)], buf_ref.at[slot], sem_ref.at[slot])

    @pl.when(b == 0)
    def _():
        for d in range(RING_DEPTH - 1):
            @pl.when(d < nb)
            def _():
                copy(d, d).start()

    nxt = b + (RING_DEPTH - 1)

    @pl.when(nxt < nb)
    def _():
        copy(nxt, nxt % RING_DEPTH).start()

    slot = b % RING_DEPTH
    copy(b, slot).wait()
    return slot


def _gdn_chunk_body(u_ref, ba_ref, hp_ref, cw_ref, gp_ref, z_ref, gn_ref, s0_ref, o_ref, s_ref, carry_ref, *ring, c, ns):
    if ring:
        slot = _state_ring_slot(s0_ref, ring[0], ring[1], ns)
        old_state = lambda j, h: ring[0][slot, j, h]
    else:
        old_state = lambda j, h: s_ref[j, h]

    @pl.when(pl.program_id(1) == 0)
    def _():
        if not ring:
            s_ref[...] = s0_ref[...]
        carry_ref[...] = hp_ref[...]

    r = ns * c
    nfac = int(round(math.log2(c)))
    assert 2 ** nfac == c
    heads = range(GDN_HEADS)
    u = u_ref[...].astype(f32)
    cdim = u.shape[-1]
    ext = jnp.concatenate([carry_ref[...], u], axis=1)
    carry_ref[...] = u[:, c - SUBLANES:, :]
    ext2 = ext.reshape(ns * (SUBLANES + c), cdim)
    cw = cw_ref[...]
    acc = ext2 * cw[CONV_WIDTH - 1:CONV_WIDTH, :]
    for sft in range(1, CONV_WIDTH):
        acc = acc + pltpu.roll(ext2, sft, axis=0) * cw[CONV_WIDTH - 1 - sft:CONV_WIDTH - sft, :]
    act = jax.nn.silu(acc.reshape(ns, SUBLANES + c, cdim)[:, SUBLANES:, :].reshape(r, cdim))
    ba = ba_ref[...].reshape(r, LANES)
    gp = gp_ref[...]
    log_decay = -jnp.exp(gp[0:1, :]) * jax.nn.softplus(ba + gp[1:2, :])
    lane = lax.broadcasted_iota(jnp.int32, ba.shape, 1)
    gates = jnp.where(lane < GDN_HEADS, jax.nn.sigmoid(ba), log_decay)
    row = lax.broadcasted_iota(jnp.int32, (r, r), 0)
    col = lax.broadcasted_iota(jnp.int32, (r, r), 1)
    same = (row // c) == (col // c)
    causal = same & (row >= col)
    strict = same & (row > col)
    g_cum = _dot_exact_lhs(causal.astype(bf16), gates)
    g_cum_t = g_cum.T
    seq_of_row = lax.broadcasted_iota(jnp.int32, (r, 1), 0) // c
    last = ((row // c) == (col // c)) & ((col % c) == c - 1)
    g_last_rows = _dot_exact_lhs(last.astype(bf16), g_cum)

    q, k, v, beta, gc, eg, decay, qk, m = ({} for _ in range(9))
    for h in heads:
        sl = slice(h * GDN_DK, (h + 1) * GDN_DK)
        qh = act[:, sl]
        kh = act[:, GDN_QK + h * GDN_DK:GDN_QK + (h + 1) * GDN_DK]
        q[h] = qh * lax.rsqrt(jnp.sum(qh * qh, axis=-1, keepdims=True) + EPS) * (GDN_DK ** -0.5)
        k[h] = kh * lax.rsqrt(jnp.sum(kh * kh, axis=-1, keepdims=True) + EPS)
        v[h] = act[:, 2 * GDN_QK + h * GDN_DV:2 * GDN_QK + (h + 1) * GDN_DV]
        beta[h] = gates[:, h:h + 1]
        gc[h] = g_cum[:, GDN_HEADS + h:GDN_HEADS + h + 1]
        gr = g_cum_t[GDN_HEADS + h:GDN_HEADS + h + 1, :]
        decay[h] = jnp.exp(jnp.where(causal, gc[h] - gr, -jnp.inf))
        kb = k[h].astype(bf16)
        prod = _dot_nt(jnp.concatenate([q[h], k[h]], axis=0).astype(bf16), kb)
        qk[h] = prod[:r] * decay[h]
        m[h] = jnp.where(strict, -(beta[h] * prod[r:] * decay[h]), 0.0)
    p = {h: jnp.where(row == col, 1.0, m[h]) for h in heads}
    for _ in range(nfac - 1):
        for h in heads:
            mb = m[h].astype(bf16)
            m[h] = _dot(mb, mb)
        for h in heads:
            p[h] = p[h] + _dot(p[h].astype(bf16), m[h].astype(bf16))
    u_part, w_part, q_dec = {}, {}, {}
    for h in heads:
        eg[h] = jnp.exp(gc[h])
        rhs = jnp.concatenate([v[h] * beta[h], k[h] * (beta[h] * eg[h])], axis=1)
        sol = _dot(p[h].astype(bf16), rhs.astype(bf16))
        u_part[h] = sol[:, :GDN_DV]
        w_part[h] = sol[:, GDN_DV:]
        q_dec[h] = q[h] * eg[h]
    u, o_state, s_old = {}, {}, {}
    for h in heads:
        us, os_ = [], []
        for j in range(ns):
            rows = slice(j * c, (j + 1) * c)
            s_old[j, h] = old_state(j, h)
            wq = jnp.concatenate([w_part[h][rows], q_dec[h][rows]], axis=0).astype(bf16)
            wqs = _dot(wq, s_old[j, h].astype(bf16))
            us.append(u_part[h][rows] - wqs[:c])
            os_.append(wqs[c:])
        u[h] = us[0] if ns == 1 else jnp.concatenate(us, axis=0)
        o_state[h] = os_[0] if ns == 1 else jnp.concatenate(os_, axis=0)
    outs, s_new = {}, {}
    for h in heads:
        ub = u[h].astype(bf16)
        outs[h] = o_state[h] + _dot(qk[h].astype(bf16), ub)
        gl = g_last_rows[:, GDN_HEADS + h:GDN_HEADS + h + 1]
        kd = (k[h] * jnp.exp(gl - gc[h])).astype(bf16)
        for j in range(ns):
            uj = ub if ns == 1 else jnp.where(seq_of_row == j, u[h], 0.0).astype(bf16)
            s_new[j, h] = s_old[j, h] * jnp.exp(gl[j * c:j * c + 1, :]) + lax.dot_general(
                kd, uj, (((0,), (0,)), ((), ())), preferred_element_type=f32)
    gn = gn_ref[...]
    for h in heads:
        sl = slice(h * GDN_DV, (h + 1) * GDN_DV)
        y = _rms(outs[h], gn) * jax.nn.silu(z_ref[:, :, sl].astype(f32).reshape(r, GDN_DV))
        o_ref[:, :, sl] = y.reshape(ns, c, GDN_DV).astype(o_ref.dtype)
        for j in range(ns):
            s_ref[j, h] = s_new[j, h]


def _gdn_chunk(proj, ba, hp, conv_w_t, gparams, gdn_norm, s0, *, nseq, c, ns):
    t = proj.shape[0]
    l = t // nseq
    nc = l // c
    assert nseq % ns == 0 and l % c == 0
    tok_spec = lambda width, col: pl.BlockSpec((ns, c, width), lambda b, n: (b, n, col))
    st_spec = pl.BlockSpec((ns, GDN_HEADS, GDN_DK, GDN_DV), lambda b, n: (b, 0, 0, 0))
    proj3 = proj.reshape(nseq, l, MAIN_WIDTH)
    ring = nc == 1 and nseq // ns >= RING_DEPTH
    ring_scratch = [pltpu.VMEM((RING_DEPTH, ns, GDN_HEADS, GDN_DK, GDN_DV), f32),
                    pltpu.SemaphoreType.DMA((RING_DEPTH,))] if ring else []
    out, s_new = pl.pallas_call(
        functools.partial(_gdn_chunk_body, c=c, ns=ns),
        grid=(nseq // ns, nc),
        in_specs=[
            tok_spec(GDN_CONV_DIM, 0),
            tok_spec(LANES, 0),
            pl.BlockSpec((ns, SUBLANES, GDN_CONV_DIM), lambda b, n: (b, 0, 0)),
            pl.BlockSpec((CONV_WIDTH, GDN_CONV_DIM), lambda b, n: (0, 0)),
            pl.BlockSpec((SUBLANES, LANES), lambda b, n: (0, 0)),
            tok_spec(GDN_V, COL_Z),
            pl.BlockSpec((1, GDN_DV), lambda b, n: (0, 0)),
            pl.BlockSpec(memory_space=pl.ANY) if ring else st_spec,
        ],
        out_specs=[tok_spec(GDN_V, 0), st_spec],
        out_shape=[
            jax.ShapeDtypeStruct((nseq, l, GDN_V), proj.dtype),
            jax.ShapeDtypeStruct((nseq, GDN_HEADS, GDN_DK, GDN_DV), f32),
        ],
        scratch_shapes=[pltpu.VMEM((ns, SUBLANES, GDN_CONV_DIM), f32)] + ring_scratch,
        compiler_params=_params(("arbitrary" if ring else "parallel", "arbitrary")),
        name="gdn_chunk",
    )(proj3, ba.reshape(nseq, l, LANES), hp, conv_w_t, gparams, proj3, gdn_norm.reshape(1, GDN_DV).astype(f32), s0)
    return out.reshape(t, GDN_V), s_new


def _ret_chunk_body(q_ref, k_ref, v_ref, g_ref, cos_ref, sin_ref, rn_ref, s0_ref, o_ref, s_ref, *ring, c, ns):
    if ring:
        slot = _state_ring_slot(s0_ref, ring[0], ring[1], ns)
        old_state = lambda j, h: ring[0][slot, j, h]
    else:
        old_state = lambda j, h: s_ref[j, h]

        @pl.when(pl.program_id(1) == 0)
        def _():
            s_ref[...] = s0_ref[...]

    half = RET_DK // 2
    cos = cos_ref[...]
    sin = sin_ref[...]
    row = lax.broadcasted_iota(jnp.int32, (c, c), 0)
    col = lax.broadcasted_iota(jnp.int32, (c, c), 1)
    causal = row >= col
    dist = (row - col).astype(f32)
    tpos = lax.broadcasted_iota(jnp.int32, (c, 1), 0).astype(f32)
    src = lax.broadcasted_iota(jnp.int32, (RET_DK, RET_DK), 0)
    dst = lax.broadcasted_iota(jnp.int32, (RET_DK, RET_DK), 1)
    orig = jnp.where(src < half, 2 * src, 2 * (src - half) + 1)
    unperm = (orig == dst).astype(bf16)

    def rot(x):
        xe, xo = x[:, :half], x[:, half:]
        return jnp.concatenate([xe * cos - xo * sin, xo * cos + xe * sin], axis=1)

    heads = range(RET_HEADS)
    lgs = [math.log1p(-(2.0 ** (-5.0 - h))) for h in heads]
    qk, vb, dec = {}, {}, []
    decay = [jnp.exp(jnp.where(causal, dist * lgs[h], -jnp.inf)) for h in heads]
    q_scale = [jnp.exp((tpos + 1.0) * lgs[h]) for h in heads]
    k_scale = [jnp.exp((float(c - 1) - tpos) * lgs[h]) for h in heads]
    for j in range(ns):
        for h in heads:
            sl = slice(h * RET_DK, (h + 1) * RET_DK)
            q = rot(q_ref[j, :, sl].astype(f32))
            k = rot(k_ref[j, :, sl].astype(f32)) * (RET_DK ** -0.5)
            vb[j, h] = v_ref[j, :, sl].astype(bf16)
            qk[j, h] = (_dot_nt(q.astype(bf16), k.astype(bf16)) * decay[h]).astype(bf16)
            dec.append(q * q_scale[h])
            dec.append(k * k_scale[h])
    dec_orig = _dot(jnp.concatenate(dec, axis=0).astype(bf16), unperm)
    rn = rn_ref[...]
    for j in range(ns):
        outs = []
        for h in heads:
            base = 2 * (j * RET_HEADS + h) * c
            q_dec = dec_orig[base:base + c].astype(bf16)
            k_dec = dec_orig[base + c:base + 2 * c].astype(bf16)
            sl = slice(h * RET_DV, (h + 1) * RET_DV)
            s = old_state(j, h)
            o = _dot(qk[j, h], vb[j, h]) + _dot(q_dec, s.astype(bf16))
            outs.append(_rms(o, rn[:, sl]) * jax.nn.silu(g_ref[j, :, sl].astype(f32)))
            s_ref[j, h] = s * math.exp(c * lgs[h]) + lax.dot_general(
                k_dec, vb[j, h], (((0,), (0,)), ((), ())), preferred_element_type=f32)
        o_ref[j] = jnp.concatenate(outs, axis=1).astype(o_ref.dtype)


def _ret_chunk(proj3, cos, sin, ret_norm, s0, *, c, ns):
    nseq, l, _ = proj3.shape
    nc = l // c
    assert nseq % ns == 0 and l % c == 0
    st_spec = pl.BlockSpec((ns, RET_HEADS, RET_DK, RET_DV), lambda b, n: (b, 0, 0, 0))
    col_spec = lambda col: pl.BlockSpec((ns, c, RET_QK), lambda b, n: (b, n, col))
    tab_spec = pl.BlockSpec((c, RET_DK // 2), lambda b, n: (n, 0))
    ring = nc == 1 and nseq // ns >= RING_DEPTH
    ring_scratch = [pltpu.VMEM((RING_DEPTH, ns, RET_HEADS, RET_DK, RET_DV), f32),
                    pltpu.SemaphoreType.DMA((RING_DEPTH,))] if ring else []
    out, s_new = pl.pallas_call(
        functools.partial(_ret_chunk_body, c=c, ns=ns),
        grid=(nseq // ns, nc),
        in_specs=[col_spec(COL_RQ), col_spec(COL_RK), col_spec(COL_RV), col_spec(COL_RG), tab_spec, tab_spec,
                  pl.BlockSpec((1, RET_V), lambda b, n: (0, 0)), pl.BlockSpec(memory_space=pl.ANY) if ring else st_spec],
        out_specs=[col_spec(0), st_spec],
        out_shape=[
            jax.ShapeDtypeStruct((nseq, l, RET_V), proj3.dtype),
            jax.ShapeDtypeStruct((nseq, RET_HEADS, RET_DK, RET_DV), f32),
        ],
        scratch_shapes=ring_scratch,
        compiler_params=_params(("arbitrary" if ring else "parallel", "arbitrary")),
        name="ret_chunk",
    )(proj3, proj3, proj3, proj3, cos, sin, ret_norm.reshape(1, RET_V).astype(f32), s0)
    return out.reshape(nseq * l, RET_V), s_new


def _post_mixer_body(mg_ref, mr_ref, x_ref, wo_ref, g_ref, wq_ref, x1_ref, q_ref):
    wo = wo_ref[...]
    x1 = x_ref[...] + (_dot(mg_ref[...].astype(bf16), wo[:GDN_V]) + _dot(mr_ref[...].astype(bf16), wo[GDN_V:]))
    x1_ref[...] = x1
    q_ref[...] = _dot(_rms(x1, g_ref[...]).astype(bf16), wq_ref[...]).astype(q_ref.dtype)


def _post_mixer(mix_g, mix_r, w_out, x, norm_cross, w_q, *, q_dtype, tm=512):
    t = x.shape[0]
    tm = min(tm, t)
    width = w_q.shape[1]
    assert t % tm == 0
    resident = lambda shape: pl.BlockSpec(shape, lambda i: (0, 0), pipeline_mode=pl.Buffered(1))
    row = lambda w: pl.BlockSpec((tm, w), lambda i: (i, 0))
    return pl.pallas_call(
        _post_mixer_body,
        grid=(t // tm,),
        in_specs=[row(GDN_V), row(RET_V), row(D_MODEL), resident((GDN_V + RET_V, D_MODEL)),
                  pl.BlockSpec((1, D_MODEL), lambda i: (0, 0)), resident((D_MODEL, width))],
        out_specs=[row(D_MODEL), row(width)],
        out_shape=[jax.ShapeDtypeStruct((t, D_MODEL), f32), jax.ShapeDtypeStruct((t, width), q_dtype)],
        compiler_params=_params(("parallel",)),
        name="post_mixer",
    )(mix_g, mix_r, x, w_out, norm_cross.reshape(1, D_MODEL).astype(f32), w_q)


def _attn_body(q_ref, k_ref, v_ref, o_ref):
    for h in range(MEM_HEADS):
        sl = slice(h * MEM_HEAD_DIM, (h + 1) * MEM_HEAD_DIM)
        qh = q_ref[:, sl].astype(bf16)
        kh = k_ref[0, :, sl].astype(bf16)
        vh = v_ref[0, :, sl].astype(bf16)
        s = _dot_nt(qh, kh) * (MEM_HEAD_DIM ** -0.5)
        e = jnp.exp(s - jnp.max(s, axis=-1, keepdims=True))
        p = e / jnp.sum(e, axis=-1, keepdims=True)
        o_ref[:, sl] = _dot(p.astype(bf16), vh).astype(o_ref.dtype)


def _attn_out_body(q_ref, k_ref, v_ref, wo_ref, x_ref, o_ref, om_ref):
    for h in range(MEM_HEADS):
        sl = slice(h * MEM_HEAD_DIM, (h + 1) * MEM_HEAD_DIM)
        qh = q_ref[:, sl].astype(bf16)
        kh = k_ref[0, :, sl].astype(bf16)
        vh = v_ref[0, :, sl].astype(bf16)
        s = _dot_nt(qh, kh) * (MEM_HEAD_DIM ** -0.5)
        e = jnp.exp(s - jnp.max(s, axis=-1, keepdims=True))
        p = e / jnp.sum(e, axis=-1, keepdims=True)
        om_ref[:, sl] = _dot(p.astype(bf16), vh).astype(bf16)
    o_ref[...] = x_ref[...] + _dot(om_ref[...], wo_ref[...].astype(bf16))


def _attention_out(q, mem_k, mem_v, w_o, x, *, l, tq):
    t = q.shape[0]
    nq = l // tq
    width = MEM_HEADS * MEM_HEAD_DIM
    row = lambda w: pl.BlockSpec((tq, w), lambda b, i: (b * nq + i, 0))
    mem_spec = pl.BlockSpec((1, N_MEM, width), lambda b, i: (b, 0, 0))
    return pl.pallas_call(
        _attn_out_body,
        grid=(t // l, nq),
        in_specs=[row(width), mem_spec, mem_spec,
                  pl.BlockSpec((width, D_MODEL), lambda b, i: (0, 0), pipeline_mode=pl.Buffered(1)), row(D_MODEL)],
        out_specs=row(D_MODEL),
        out_shape=jax.ShapeDtypeStruct((t, D_MODEL), f32),
        scratch_shapes=[pltpu.VMEM((tq, width), bf16)],
        compiler_params=_params(("parallel", "arbitrary")),
        name="mem_attn_out",
    )(q, mem_k, mem_v, w_o, x)


def _attn_cache_body(q_ref, k_ref, v_ref, o_ref, *, tq, ns):
    rows = N_MEM * MEM_HEADS
    shape = (MEM_HEADS * tq, rows)
    own_head = (lax.broadcasted_iota(jnp.int32, shape, 0) // tq) == (lax.broadcasted_iota(jnp.int32, shape, 1) % MEM_HEADS)
    for j in range(ns):
        rj = slice(j * tq, (j + 1) * tq)
        k2 = k_ref[j].reshape(rows, MEM_HEAD_DIM).astype(bf16)
        v2 = v_ref[j].reshape(rows, MEM_HEAD_DIM).astype(bf16)
        qs = jnp.concatenate([q_ref[rj, h * MEM_HEAD_DIM:(h + 1) * MEM_HEAD_DIM] for h in range(MEM_HEADS)], axis=0)
        s = _dot_nt(qs.astype(bf16), k2) * (MEM_HEAD_DIM ** -0.5)
        s = jnp.where(own_head, s, -jnp.inf)
        e = jnp.exp(s - jnp.max(s, axis=-1, keepdims=True))
        p = e / jnp.sum(e, axis=-1, keepdims=True)
        o = _dot(p.astype(bf16), v2)
        for h in range(MEM_HEADS):
            o_ref[rj, h * MEM_HEAD_DIM:(h + 1) * MEM_HEAD_DIM] = o[h * tq:(h + 1) * tq].astype(o_ref.dtype)


def _attention(q, mem_k, mem_v, *, l, tq, ns=1):
    t = q.shape[0]
    nq = l // tq
    width = MEM_HEADS * MEM_HEAD_DIM
    if mem_k.ndim == 4:
        assert nq == 1 and (t // l) % ns == 0
        body = functools.partial(_attn_cache_body, tq=tq, ns=ns)
        mem_spec = pl.BlockSpec((ns, N_MEM, MEM_HEADS, MEM_HEAD_DIM), lambda b, i: (b, 0, 0, 0))
    else:
        assert ns == 1
        body = _attn_body
        mem_spec = pl.BlockSpec((1, N_MEM, width), lambda b, i: (b, 0, 0))
    return pl.pallas_call(
        body,
        grid=(t // (l * ns), nq),
        in_specs=[
            pl.BlockSpec((ns * tq, width), lambda b, i: (b * nq + i, 0)),
            mem_spec,
            mem_spec,
        ],
        out_specs=pl.BlockSpec((ns * tq, width), lambda b, i: (b * nq + i, 0)),
        out_shape=jax.ShapeDtypeStruct((t, width), q.dtype),
        compiler_params=_params(("parallel", "arbitrary")),
        name="mem_attn",
    )(q, mem_k, mem_v)


def _ffn_body(x_ref, g_ref, wg_ref, wu_ref, wd_ref, gf_ref, o_ref, xn_ref):
    f = pl.program_id(1)

    @pl.when(f == 0)
    def _():
        x = x_ref[...]
        xn_ref[...] = _rms(x, g_ref[...]).astype(bf16)
        o_ref[...] = x

    xn = xn_ref[...]
    a = _dot(xn, wg_ref[...].astype(bf16))
    u = _dot(xn, wu_ref[...].astype(bf16))
    hidden = (jax.nn.silu(a) * u).astype(bf16)
    o_ref[...] += _dot(hidden, wd_ref[...].astype(bf16))

    @pl.when(f == pl.num_programs(1) - 1)
    def _():
        o_ref[...] = _rms(o_ref[...], gf_ref[...])


def _ffn(x, norm_ffn, w_gate, w_up, w_down, norm_final, *, tm=1024, tf=256):
    t = x.shape[0]
    tm = min(tm, t)
    assert D_FF % tf == 0 and t % tm == 0
    return pl.pallas_call(
        _ffn_body,
        grid=(t // tm, D_FF // tf),
        in_specs=[
            pl.BlockSpec((tm, D_MODEL), lambda i, f: (i, 0)),
            pl.BlockSpec((1, D_MODEL), lambda i, f: (0, 0)),
            pl.BlockSpec((D_MODEL, tf), lambda i, f: (0, f)),
            pl.BlockSpec((D_MODEL, tf), lambda i, f: (0, f)),
            pl.BlockSpec((tf, D_MODEL), lambda i, f: (f, 0)),
            pl.BlockSpec((1, D_MODEL), lambda i, f: (0, 0)),
        ],
        out_specs=pl.BlockSpec((tm, D_MODEL), lambda i, f: (i, 0)),
        out_shape=jax.ShapeDtypeStruct((t, D_MODEL), f32),
        scratch_shapes=[pltpu.VMEM((tm, D_MODEL), bf16)],
        compiler_params=_params(("parallel", "arbitrary")),
        name="ffn",
    )(x, norm_ffn.reshape(1, D_MODEL).astype(f32), w_gate, w_up, w_down, norm_final.reshape(1, D_MODEL).astype(f32))


def _rotary_tables(l, pos0):
    half = RET_DK // 2
    inv = ROPE_BASE ** (-jnp.linspace(0.0, 1.0, half, dtype=f32))
    pos = jnp.arange(l, dtype=f32) + pos0
    ang = pos[:, None] * inv[None, :]
    return jnp.cos(ang), jnp.sin(ang)


def _layer(x3, pos0, conv_hist, s_gdn, s_ret, mem_k, mem_v, wts, *, gdn_c, gdn_ns, ret_c, ret_ns, attn_tq, attn_ns):
    nseq, l, _ = x3.shape
    t = nseq * l
    x = x3.reshape(t, D_MODEL)
    proj_dtype = bf16 if l % (2 * SUBLANES) == 0 else f32
    proj, ba = _in_proj(x, wts["norm_mix"], wts["w_in"], wts["w_in_ba"], out_dtype=proj_dtype)
    proj3 = proj.reshape(nseq, l, MAIN_WIDTH)
    hp = jnp.pad(conv_hist.astype(f32), ((0, 0), (SUBLANES - (CONV_WIDTH - 1), 0), (0, 0)))
    mix_g, s_gdn_new = _gdn_chunk(proj, ba, hp, wts["conv_w_t"], wts["gparams"], wts["gdn_norm"], s_gdn.astype(f32),
                                  nseq=nseq, c=gdn_c, ns=gdn_ns)
    cos, sin = _rotary_tables(l, pos0)
    mix_r, s_ret_new = _ret_chunk(proj3, cos, sin, wts["ret_norm"], s_ret.astype(f32), c=ret_c, ns=ret_ns)
    conv_new = proj3[:, l - (CONV_WIDTH - 1):, :GDN_CONV_DIM].astype(f32)
    act_dtype = bf16 if attn_tq % (2 * SUBLANES) == 0 else f32
    x1, qm = _post_mixer(mix_g, mix_r, wts["w_out"], x, wts["norm_cross"], wts["w_mem_q"], q_dtype=act_dtype)
    if mem_k.ndim == 3:
        x2 = _attention_out(qm, mem_k, mem_v, wts["w_mem_o"], x1, l=l, tq=attn_tq)
    else:
        om = _attention(qm, mem_k, mem_v, l=l, tq=attn_tq, ns=attn_ns)
        x2 = _norm_matmul(om, wts["norm_cross"], wts["w_mem_o"], residual=x1, norm=False, name="mem_o", tn=512)
    y = _ffn(x2, wts["norm_ffn"], wts["w_gate"], wts["w_up"], wts["w_down"], wts["norm_final"])
    return y.reshape(nseq, l, D_MODEL), conv_new, s_gdn_new, s_ret_new


W_PREP_TILE = 256


def _w_in_prep_body(wt_ref, bat_ref, o_ref, ba_ref):
    j = pl.program_id(0)
    first_ret = (GDN_CONV_DIM + GDN_V) // W_PREP_TILE
    is_ret_qk = (j >= first_ret) & (j < first_ret + 2 * RET_QK // W_PREP_TILE)

    @pl.when(is_ret_qk)
    def _():
        half = RET_DK // 2
        src = lax.broadcasted_iota(jnp.int32, (RET_DK, RET_DK), 0)
        dst = lax.broadcasted_iota(jnp.int32, (RET_DK, RET_DK), 1)
        perm = (src == jnp.where(dst < half, 2 * dst, 2 * (dst - half) + 1)).astype(bf16)
        o_ref[...] = _dot(wt_ref[...].T.astype(bf16), perm).astype(bf16)

    @pl.when(jnp.logical_not(is_ret_qk))
    def _():
        o_ref[...] = wt_ref[...].T.astype(bf16)

    @pl.when(j == 0)
    def _():
        ba = bat_ref[...].T
        ba_ref[...] = jnp.concatenate([ba, jnp.zeros((ba.shape[0], LANES - 2 * GDN_HEADS), f32)], axis=1).astype(bf16)


def _w_in_prep(w_in):
    k, n = w_in.shape
    i0 = GDN_CONV_DIM + GDN_V
    n_gate = 2 * GDN_HEADS
    assert RET_DK == W_PREP_TILE and i0 % W_PREP_TILE == 0 and n == MAIN_WIDTH + n_gate and i0 % n_gate == 0
    wt = jnp.swapaxes(w_in, 0, 1)
    src_row = lambda j: (pl.multiple_of(j * W_PREP_TILE + n_gate * (j * W_PREP_TILE >= i0).astype(jnp.int32), SUBLANES), 0)
    return pl.pallas_call(
        _w_in_prep_body,
        grid=(MAIN_WIDTH // W_PREP_TILE,),
        in_specs=[pl.BlockSpec((pl.Element(W_PREP_TILE), pl.Element(k)), src_row),
                  pl.BlockSpec((n_gate, k), lambda j: (i0 // n_gate, 0))],
        out_specs=[pl.BlockSpec((k, W_PREP_TILE), lambda j: (0, j)), pl.BlockSpec((k, LANES), lambda j: (0, 0))],
        out_shape=[jax.ShapeDtypeStruct((k, MAIN_WIDTH), bf16), jax.ShapeDtypeStruct((k, LANES), bf16)],
        compiler_params=_params(("arbitrary",)),
        name="w_in_prep",
    )(wt, wt)


def _prep_weights(norm_mix, w_in, conv_w, gdn_a_log, gdn_dt_bias, gdn_norm, ret_norm, w_out,
                  norm_cross, w_mem_q, w_mem_o, norm_ffn, w_gate, w_up, w_down, norm_final):
    w_main, w_ba = _w_in_prep(w_in.astype(f32))
    gparams = jnp.zeros((SUBLANES, LANES), f32)
    gparams = gparams.at[0, GDN_HEADS:2 * GDN_HEADS].set(gdn_a_log.astype(f32))
    gparams = gparams.at[1, GDN_HEADS:2 * GDN_HEADS].set(gdn_dt_bias.astype(f32))
    return {
        "norm_mix": norm_mix, "w_in": w_main, "w_in_ba": w_ba,
        "conv_w_t": conv_w.astype(f32).T, "gparams": gparams,
        "gdn_norm": gdn_norm, "ret_norm": ret_norm, "w_out": w_out.astype(bf16),
        "norm_cross": norm_cross, "w_mem_q": w_mem_q.astype(bf16), "w_mem_o": w_mem_o.astype(bf16),
        "norm_ffn": norm_ffn, "w_gate": w_gate, "w_up": w_up, "w_down": w_down, "norm_final": norm_final,
    }


def kernel(x_prompt, x_sample, mem_prompt, cache_mem_k, cache_mem_v, state_gdn, state_gdn_conv, state_ret,
           norm_mix, w_in, conv_w, gdn_a_log, gdn_dt_bias, gdn_norm, ret_norm, w_out,
           norm_mem_in, norm_cross, w_mem_q, w_mem_k, w_mem_v, w_mem_o,
           norm_ffn, w_gate, w_up, w_down, norm_final):
    depth = w_in.shape[0]
    assert depth == 1
    batch, seq, _ = x_prompt.shape
    dec_batch, dec_seq, _ = x_sample.shape
    width = MEM_HEADS * MEM_HEAD_DIM
    wts = _prep_weights(norm_mix[0], w_in[0], conv_w[0], gdn_a_log[0], gdn_dt_bias[0], gdn_norm[0], ret_norm[0],
                        w_out[0], norm_cross[0], w_mem_q[0], w_mem_o[0], norm_ffn[0], w_gate[0], w_up[0], w_down[0],
                        norm_final)

    mem2 = mem_prompt.reshape(batch * N_MEM, D_MODEL)
    mk, mv = _memory_kv(mem2, norm_mem_in[0], w_mem_k[0], w_mem_v[0])
    mk4 = mk.reshape(batch, N_MEM, MEM_HEADS, MEM_HEAD_DIM)
    mv4 = mv.reshape(batch, N_MEM, MEM_HEADS, MEM_HEAD_DIM)
    y_p, conv_p, sg_p, sr_p = _layer(
        x_prompt, 0.0,
        jnp.zeros((batch, CONV_WIDTH - 1, GDN_CONV_DIM), f32),
        jnp.zeros((batch, GDN_HEADS, GDN_DK, GDN_DV), f32),
        jnp.zeros((batch, RET_HEADS, RET_DK, RET_DV), f32),
        mk.reshape(batch, N_MEM, width), mv.reshape(batch, N_MEM, width), wts,
        gdn_c=min(GDN_CHUNK, seq), gdn_ns=min(2, batch), ret_c=min(RET_CHUNK, seq), ret_ns=min(2, batch),
        attn_tq=min(512, seq), attn_ns=1)

    y_s, conv_s, sg_s, sr_s = _layer(
        x_sample, float(PAST_LEN), state_gdn_conv[0], state_gdn[0], state_ret[0],
        cache_mem_k[0], cache_mem_v[0], wts,
        gdn_c=dec_seq, gdn_ns=min(2 * GDN_CHUNK // dec_seq, dec_batch),
        ret_c=dec_seq, ret_ns=min(8, dec_batch), attn_tq=dec_seq, attn_ns=min(4, dec_batch))

    pdt = x_prompt.dtype
    sdt = x_sample.dtype
    return (y_p.astype(pdt), y_s.astype(sdt),
            sg_p.astype(pdt)[None], conv_p.astype(pdt)[None], sr_p.astype(pdt)[None],
            mk4[None], mv4[None],
            sg_s.astype(sdt)[None], conv_s.astype(sdt)[None], sr_s.astype(sdt)[None])
```
